```python
import math
import jax, jax.numpy as jnp
from jax import lax
import numpy as np

D_MODEL = 1024
BATCH = 4
SEQ = 8192
DEPTH = 4

CTX_LEN = 256
GRID_W = 64
N_MIXERS = 4
MIXER_DIFF = 0
MIXER_FOURIER = 1
MIXER_NEIGHBOR = 2
MIXER_MLA = 3
Q_BLOCK = 128
ROPE_BASE = 10000.0
NORM_EPS = 1e-6
ADA_CHUNKS = 6
DA_HEAD_DIM = 64
DA_HEADS = D_MODEL // (2 * DA_HEAD_DIM)
FN_GROUPS = 4
NA_HEADS = 16
NA_HEAD_DIM = D_MODEL // NA_HEADS
NA_WIN_ROWS = 8
NA_WIN_COLS = 16
MLA_HEADS = 16
MLA_NOPE_DIM = 64
MLA_ROPE_DIM = 32
MLA_V_DIM = 64
MLA_Q_RANK = 768
MLA_KV_RANK = 256
N_EXPERTS = 16
EXPERT_DIM = 2048
EC_CAPACITY_FACTOR = 2

kernel_name = "hybrid_diffusion_trunk_diffattn_fnet_natten_mla_ecmoe"


def _n_layers_of(kind):
    return len(range(kind, DEPTH, N_MIXERS))


def _rmsnorm(x, g):
    xf = x.astype(jnp.float32)
    y = xf * lax.rsqrt(jnp.mean(xf * xf, axis=-1, keepdims=True) + NORM_EPS)
    return (y * g.astype(jnp.float32)).astype(x.dtype)


def _axial_rope_tables(n_tok, rot_dim):
    t = jnp.arange(n_tok)
    rows = (t // GRID_W).astype(jnp.float32)
    cols = (t % GRID_W).astype(jnp.float32)
    n_freq = rot_dim // 4
    inv_freq = ROPE_BASE ** (-jnp.arange(n_freq, dtype=jnp.float32) / n_freq)
    ang = jnp.concatenate([rows[:, None] * inv_freq, cols[:, None] * inv_freq], axis=-1)
    return jnp.cos(ang), jnp.sin(ang)


def _apply_rope(t, cos, sin):
    half = t.shape[-1] // 2
    t1 = t[..., :half].astype(jnp.float32)
    t2 = t[..., half:].astype(jnp.float32)
    return jnp.concatenate([t1 * cos - t2 * sin, t2 * cos + t1 * sin], axis=-1).astype(t.dtype)


def _to_blocks(t):
    *lead, n, d = t.shape
    return jnp.moveaxis(t.reshape(*lead, n // Q_BLOCK, Q_BLOCK, d), -3, 0)


def _from_blocks(o):
    o = jnp.moveaxis(o, 0, -3)
    *lead, nb, qb, d = o.shape
    return o.reshape(*lead, nb * qb, d)


def _merge_heads(o):
    b, h, n, d = o.shape
    return o.transpose(0, 2, 1, 3).reshape(b, n, h * d)


def _softmax_f32(s):
    return jax.nn.softmax(s.astype(jnp.float32), axis=-1)


def _dense_attention(q, k, v, scale):
    p = _softmax_f32(jnp.einsum('bhqd,bhkd->bhqk', q, k) * scale)
    return jnp.einsum('bhqk,bhkd->bhqd', p.astype(v.dtype), v)


def _diff_attention(h_ctx, h_lat, w_qkv, w_o, lam_q1, lam_k1, lam_q2, lam_k2, subln, lambda_init, with_ctx_out):
    scale = DA_HEAD_DIM ** -0.5

    def proj(h):
        b, n, _ = h.shape
        q, k, v = jnp.split(h @ w_qkv, 3, axis=-1)
        q = q.reshape(b, n, DA_HEADS, 2, DA_HEAD_DIM).transpose(3, 0, 2, 1, 4)
        k = k.reshape(b, n, DA_HEADS, 2, DA_HEAD_DIM).transpose(3, 0, 2, 1, 4)
        v = v.reshape(b, n, DA_HEADS, 2 * DA_HEAD_DIM).transpose(0, 2, 1, 3)
        return q, k, v

    q_c, k_c, v_c = proj(h_ctx)
    q_l, k_l, v_l = proj(h_lat)
    cos, sin = _axial_rope_tables(h_lat.shape[1], DA_HEAD_DIM)
    q_l = _apply_rope(q_l, cos, sin)
    k_l = _apply_rope(k_l, cos, sin)
    lam = (jnp.exp(jnp.sum(lam_q1 * lam_k1)) - jnp.exp(jnp.sum(lam_q2 * lam_k2))).astype(jnp.float32) + lambda_init

    def core(q, k, v):
        p = _softmax_f32(jnp.einsum('ibhqd,ibhkd->ibhqk', q, k) * scale)
        return jnp.einsum('bhqk,bhkd->bhqd', (p[0] - lam * p[1]).astype(v.dtype), v)

    def finish(o):
        return _merge_heads(_rmsnorm(o, subln) * (1.0 - lambda_init)) @ w_o

    k_all = jnp.concatenate([k_c, k_l], axis=3)
    v_all = jnp.concatenate([v_c, v_l], axis=2)
    o_l = _from_blocks(lax.map(lambda qb: core(qb, k_all, v_all), _to_blocks(q_l)))
    y_ctx = finish(core(q_c, k_c, v_c)) if with_ctx_out else None
    return y_ctx, finish(o_l)


def _fourier_group_mix(h, w_o):
    b, n, d = h.shape
    hg = h.astype(jnp.float32).reshape(b, n, FN_GROUPS, d // FN_GROUPS).transpose(0, 2, 1, 3)
    y = jnp.fft.fft2(hg, norm="ortho").real
    return y.transpose(0, 2, 1, 3).reshape(b, n, d).astype(h.dtype) @ w_o


def _fourier_mixer(h_ctx, h_lat, w_o, with_ctx_out):
    y_ctx = _fourier_group_mix(h_ctx, w_o) if with_ctx_out else None
    return y_ctx, _fourier_group_mix(h_lat, w_o)


def _neighborhood_attention(h_ctx, h_lat, w_qkv, w_o, rpb, with_ctx_out):
    b, n, _ = h_lat.shape
    rows = n // GRID_W
    kr = min(NA_WIN_ROWS, rows)
    scale = NA_HEAD_DIM ** -0.5

    def proj(h):
        bh, nh, _ = h.shape
        heads = lambda t: t.reshape(bh, nh, NA_HEADS, NA_HEAD_DIM).transpose(0, 2, 1, 3)
        q, k, v = jnp.split(h @ w_qkv, 3, axis=-1)
        return heads(q), heads(k), heads(v)

    q_c, k_c, v_c = proj(h_ctx)
    q_l, k_l, v_l = proj(h_lat)
    grid = lambda t: t.reshape(b, NA_HEADS, rows, GRID_W, NA_HEAD_DIM)
    k_g, v_g = grid(k_l), grid(v_l)
    q_rows = jnp.moveaxis(grid(q_l), 2, 0)
    cols = np.arange(GRID_W)
    col_start = np.clip(cols - NA_WIN_COLS // 2, 0, GRID_W - NA_WIN_COLS)
    col_idx = col_start[:, None] + np.arange(NA_WIN_COLS)
    col_bias_idx = col_idx - cols[:, None] + NA_WIN_COLS - 1
    rpb_cols = rpb[:, :, col_bias_idx]

    def row_fn(args):
        r, q = args
        r0 = jnp.clip(r - kr // 2, 0, rows - kr)
        kb = lax.dynamic_slice_in_dim(k_g, r0, kr, axis=2)[:, :, :, col_idx]
        vb = lax.dynamic_slice_in_dim(v_g, r0, kr, axis=2)[:, :, :, col_idx]
        bias = rpb_cols[:, r0 + jnp.arange(kr) - r + NA_WIN_ROWS - 1].transpose(0, 2, 1, 3)
        s_win = jnp.einsum('bhqd,bhrqjd->bhqrj', q, kb).astype(jnp.float32) * scale + bias.astype(jnp.float32)[None]
        s_ctx = jnp.einsum('bhqd,bhkd->bhqk', q, k_c).astype(jnp.float32) * scale
        p = _softmax_f32(jnp.concatenate([s_win.reshape(b, NA_HEADS, GRID_W, kr * NA_WIN_COLS), s_ctx], axis=-1))
        p = p.astype(v_c.dtype)
        p_win = p[..., :kr * NA_WIN_COLS].reshape(b, NA_HEADS, GRID_W, kr, NA_WIN_COLS)
        p_ctx = p[..., kr * NA_WIN_COLS:]
        return jnp.einsum('bhqrj,bhrqjd->bhqd', p_win, vb) + jnp.einsum('bhqk,bhkd->bhqd', p_ctx, v_c)

    o_rows = lax.map(row_fn, (jnp.arange(rows), q_rows))
    o_l = jnp.moveaxis(o_rows, 0, 2).reshape(b, NA_HEADS, n, NA_HEAD_DIM)
    y_ctx = _merge_heads(_dense_attention(q_c, k_c, v_c, scale)) @ w_o if with_ctx_out else None
    return y_ctx, _merge_heads(o_l) @ w_o


def _mla(h_ctx, h_lat, w_dq, q_norm, w_uq, w_dkv, kv_norm, w_uk, w_uv, w_o, with_ctx_out):
    scale = (MLA_NOPE_DIM + MLA_ROPE_DIM) ** -0.5

    def proj(h):
        b, n, _ = h.shape
        q = (_rmsnorm(h @ w_dq, q_norm) @ w_uq).reshape(b, n, MLA_HEADS, MLA_NOPE_DIM + MLA_ROPE_DIM).transpose(0, 2, 1, 3)
        ckv = h @ w_dkv
        c_kv = _rmsnorm(ckv[..., :MLA_KV_RANK], kv_norm)
        k_rope = ckv[..., MLA_KV_RANK:]
        k_nope = (c_kv @ w_uk).reshape(b, n, MLA_HEADS, MLA_NOPE_DIM).transpose(0, 2, 1, 3)
        v = (c_kv @ w_uv).reshape(b, n, MLA_HEADS, MLA_V_DIM).transpose(0, 2, 1, 3)
        return q[..., :MLA_NOPE_DIM], q[..., MLA_NOPE_DIM:], k_nope, k_rope, v

    qn_c, qr_c, kn_c, kr_c, v_c = proj(h_ctx)
    qn_l, qr_l, kn_l, kr_l, v_l = proj(h_lat)
    cos, sin = _axial_rope_tables(h_lat.shape[1], MLA_ROPE_DIM)
    qr_l = _apply_rope(qr_l, cos, sin)
    kr_l = _apply_rope(kr_l, cos, sin)

    def core(qn, qr, kn, kr, v):
        s = jnp.einsum('bhqd,bhkd->bhqk', qn, kn) + jnp.einsum('bhqd,bkd->bhqk', qr, kr)
        p = _softmax_f32(s * scale)
        return jnp.einsum('bhqk,bhkd->bhqd', p.astype(v.dtype), v)

    kn_all = jnp.concatenate([kn_c, kn_l], axis=2)
    kr_all = jnp.concatenate([kr_c, kr_l], axis=1)
    v_all = jnp.concatenate([v_c, v_l], axis=2)
    o_l = _from_blocks(lax.map(lambda qs: core(qs[0], qs[1], kn_all, kr_all, v_all), (_to_blocks(qn_l), _to_blocks(qr_l))))
    y_ctx = _merge_heads(core(qn_c, qr_c, kn_c, kr_c, v_c)) @ w_o if with_ctx_out else None
    return y_ctx, _merge_heads(o_l) @ w_o


def _expert_choice_moe(h, w_router, w_gate, w_up, w_down):
    b, n, d = h.shape
    cap = EC_CAPACITY_FACTOR * n // N_EXPERTS
    aff = jax.nn.softmax((h @ w_router).astype(jnp.float32), axis=-1)
    gate, idx = lax.top_k(jnp.swapaxes(aff, 1, 2), cap)
    xs = jax.vmap(lambda hb, ib: hb[ib])(h, idx)
    a = jnp.einsum('becd,edf->becf', xs, w_gate)
    u = jnp.einsum('becd,edf->becf', xs, w_up)
    y = jnp.einsum('becf,efd->becd', jax.nn.silu(a) * u, w_down) * gate[..., None].astype(h.dtype)
    return jax.vmap(lambda ib, yb: jnp.zeros((n, d), yb.dtype).at[ib.reshape(-1)].add(yb.reshape(-1, d)))(idx, y)


def setup_inputs(seed: int = 0) -> dict:
    key = jax.random.key(seed)
    keys = iter(jax.random.split(key, 64))
    f32 = jnp.float32

    def normal(shape, std):
        return std * jax.random.normal(next(keys), shape, f32)

    def lin(shape, scale=1.0):
        return normal(shape, scale * shape[-2] ** -0.5)

    def gain(shape):
        return 1.0 + normal(shape, 0.05)

    d = D_MODEL
    n_a, n_b, n_c, n_d = (_n_layers_of(k) for k in range(N_MIXERS))
    return {
        "x": normal((BATCH, SEQ, d), 1.0),
        "c": normal((BATCH, d), 1.0),
        "ctx": normal((BATCH, CTX_LEN, d), 1.0),
        "c_ctx": normal((d,), 1.0),
        "ada_w": lin((DEPTH, d, ADA_CHUNKS * d), 0.5),
        "ada_b": normal((DEPTH, ADA_CHUNKS * d), 0.02),
        "norm_mix": gain((DEPTH, d)),
        "norm_ffn": gain((DEPTH, d)),
        "norm_final": gain((d,)),
        "da_w_qkv": lin((n_a, d, 3 * d)),
        "da_w_o": lin((n_a, d, d)),
        "da_lambda_q1": normal((n_a, DA_HEAD_DIM), 0.1),
        "da_lambda_k1": normal((n_a, DA_HEAD_DIM), 0.1),
        "da_lambda_q2": normal((n_a, DA_HEAD_DIM), 0.1),
        "da_lambda_k2": normal((n_a, DA_HEAD_DIM), 0.1),
        "da_subln": gain((n_a, 2 * DA_HEAD_DIM)),
        "fn_w_o": lin((n_b, d, d)),
        "na_w_qkv": lin((n_c, d, 3 * d)),
        "na_w_o": lin((n_c, d, d)),
        "na_rpb": normal((n_c, NA_HEADS, 2 * NA_WIN_ROWS - 1, 2 * NA_WIN_COLS - 1), 0.1),
        "mla_w_dq": lin((n_d, d, MLA_Q_RANK)),
        "mla_q_norm": gain((n_d, MLA_Q_RANK)),
        "mla_w_uq": lin((n_d, MLA_Q_RANK, MLA_HEADS * (MLA_NOPE_DIM + MLA_ROPE_DIM))),
        "mla_w_dkv": lin((n_d, d, MLA_KV_RANK + MLA_ROPE_DIM)),
        "mla_kv_norm": gain((n_d, MLA_KV_RANK)),
        "mla_w_uk": lin((n_d, MLA_KV_RANK, MLA_HEADS * MLA_NOPE_DIM)),
        "mla_w_uv": lin((n_d, MLA_KV_RANK, MLA_HEADS * MLA_V_DIM)),
        "mla_w_o": lin((n_d, MLA_HEADS * MLA_V_DIM, d)),
        "moe_w_router": lin((DEPTH, d, N_EXPERTS)),
        "moe_w_gate": lin((DEPTH, N_EXPERTS, d, EXPERT_DIM)),
        "moe_w_up": lin((DEPTH, N_EXPERTS, d, EXPERT_DIM)),
        "moe_w_down": lin((DEPTH, N_EXPERTS, EXPERT_DIM, d)),
    }


def reference(x, c, ctx, c_ctx, ada_w, ada_b, norm_mix, norm_ffn, norm_final,
              da_w_qkv, da_w_o, da_lambda_q1, da_lambda_k1, da_lambda_q2, da_lambda_k2, da_subln,
              fn_w_o, na_w_qkv, na_w_o, na_rpb,
              mla_w_dq, mla_q_norm, mla_w_uq, mla_w_dkv, mla_kv_norm, mla_w_uk, mla_w_uv, mla_w_o,
              moe_w_router, moe_w_gate, moe_w_up, moe_w_down):
    for i in range(DEPTH):
        kind, j = i % N_MIXERS, i // N_MIXERS
        keep_ctx = i < DEPTH - 1
        mod = jax.nn.silu(c) @ ada_w[i] + ada_b[i]
        mod_c = jax.nn.silu(c_ctx) @ ada_w[i] + ada_b[i]
        sh1, sc1, g1, sh2, sc2, g2 = jnp.split(mod[:, None, :], ADA_CHUNKS, axis=-1)
        csh1, csc1, cg1, csh2, csc2, cg2 = jnp.split(mod_c, ADA_CHUNKS, axis=-1)
        h_lat = _rmsnorm(x, norm_mix[i]) * (1.0 + sc1) + sh1
        h_ctx = _rmsnorm(ctx, norm_mix[i]) * (1.0 + csc1) + csh1
        if kind == MIXER_DIFF:
            lambda_init = 0.8 - 0.6 * math.exp(-0.3 * i)
            y_ctx, y_lat = _diff_attention(h_ctx, h_lat, da_w_qkv[j], da_w_o[j], da_lambda_q1[j], da_lambda_k1[j],
                                           da_lambda_q2[j], da_lambda_k2[j], da_subln[j], lambda_init, keep_ctx)
        elif kind == MIXER_FOURIER:
            y_ctx, y_lat = _fourier_mixer(h_ctx, h_lat, fn_w_o[j], keep_ctx)
        elif kind == MIXER_NEIGHBOR:
            y_ctx, y_lat = _neighborhood_attention(h_ctx, h_lat, na_w_qkv[j], na_w_o[j], na_rpb[j], keep_ctx)
        else:
            y_ctx, y_lat = _mla(h_ctx, h_lat, mla_w_dq[j], mla_q_norm[j], mla_w_uq[j], mla_w_dkv[j], mla_kv_norm[j],
                                mla_w_uk[j], mla_w_uv[j], mla_w_o[j], keep_ctx)
        x = x + g1 * y_lat
        h_lat = _rmsnorm(x, norm_ffn[i]) * (1.0 + sc2) + sh2
        x = x + g2 * _expert_choice_moe(h_lat, moe_w_router[i], moe_w_gate[i], moe_w_up[i], moe_w_down[i])
        if keep_ctx:
            ctx = ctx + cg1 * y_ctx
            h_ctx = _rmsnorm(ctx, norm_ffn[i]) * (1.0 + csc2) + csh2
            ctx = ctx + cg2 * _expert_choice_moe(h_ctx, moe_w_router[i], moe_w_gate[i], moe_w_up[i], moe_w_down[i])
    return _rmsnorm(x, norm_final)
```

```python
import functools
import math

import numpy as np
import jax
import jax.numpy as jnp
from jax import lax
from jax.experimental import pallas as pl
from jax.experimental.pallas import tpu as pltpu

F32 = jnp.float32
BF16 = jnp.bfloat16

LANES = 128
VMEM_LIMIT_BYTES = 56 * 1024 * 1024

GRID_W = 64
ROPE_BASE = 10000.0
NORM_EPS = 1e-6
ADA_CHUNKS = 6
DA_HEAD_DIM = 64
NA_HEADS = 16
NA_HEAD_DIM = 64
NA_WIN_ROWS = 8
NA_WIN_COLS = 16
NA_Q_ROWS = 8
NA_K_ROWS = 16
MLA_HEADS = 16
MLA_NOPE_DIM = 64
MLA_ROPE_DIM = 32
MLA_V_DIM = 64
FN_GROUPS = 4
EC_CAPACITY_FACTOR = 2
MASK_VALUE = -1e30


def _cparams(sem):
    return pltpu.CompilerParams(dimension_semantics=sem, vmem_limit_bytes=VMEM_LIMIT_BYTES)


def _rope_lanes(acc, cos, sin_next, sin_prev, shift):
    outs = []
    for g in range(acc.shape[1] // LANES):
        blk = acc[:, g * LANES:(g + 1) * LANES]
        nxt = pltpu.roll(blk, LANES - shift, 1)
        prv = pltpu.roll(blk, shift, 1)
        outs.append(blk * cos + nxt * sin_next + prv * sin_prev)
    return outs[0] if len(outs) == 1 else jnp.concatenate(outs, axis=1)


def _mm_kernel(*refs, norm_cols, has_gain, has_mod, has_colscale, rope_tiles, rope_shift, has_bias, has_res,
               silu_in, precision):
    it = iter(refs)
    a_ref = next(it)
    w_ref = next(it)
    gain_ref = next(it) if has_gain else None
    sc_ref, sh_ref = (next(it), next(it)) if has_mod else (None, None)
    cs_ref = next(it) if has_colscale else None
    cos_ref, sn_ref, sp_ref = (next(it), next(it), next(it)) if rope_tiles else (None, None, None)
    bias_ref = next(it) if has_bias else None
    res_ref, gate_ref = (next(it), next(it)) if has_res else (None, None)
    o_ref = next(it)
    h_scr = next(it, None)
    j = pl.program_id(1)
    hi = precision is not None

    def prologue():
        a = a_ref[...].astype(F32)
        if silu_in:
            a = a * (1.0 / (1.0 + jnp.exp(-a)))
        if has_gain:
            k = a.shape[1]
            if norm_cols < k:
                col = lax.broadcasted_iota(jnp.int32, (1, k), 1)
                sq = jnp.where(col < norm_cols, a * a, 0.0)
            else:
                sq = a * a
            ms = jnp.sum(sq, axis=-1, keepdims=True) * (1.0 / norm_cols)
            y = a * lax.rsqrt(ms + NORM_EPS) * gain_ref[...]
            a = jnp.where(col < norm_cols, y, a) if norm_cols < k else y
        if has_mod:
            a = a * (1.0 + sc_ref[...]) + sh_ref[...]
        return a if hi else a.astype(BF16)

    if h_scr is not None:
        @pl.when(j == 0)
        def _():
            h_scr[...] = prologue()
        h = h_scr[...]
    else:
        h = a_ref[...] if hi else a_ref[...].astype(BF16)
    w = w_ref[...] if hi else w_ref[...].astype(BF16)
    acc = jnp.dot(h, w, preferred_element_type=F32, precision=precision)
    if has_colscale:
        acc = acc * cs_ref[...]

    def finish(v):
        if has_bias:
            v = v + bias_ref[...]
        if has_res:
            v = res_ref[...] + gate_ref[...] * v
        o_ref[...] = v.astype(o_ref.dtype)

    if rope_tiles:
        @pl.when(j < rope_tiles)
        def _():
            finish(_rope_lanes(acc, cos_ref[...], sn_ref[...], sp_ref[...], rope_shift))

        @pl.when(j >= rope_tiles)
        def _():
            finish(acc)
    else:
        finish(acc)


def _fused_mm(a, w, *, w_index=None, a_cols=None, gain=None, norm_cols=None, mod=None, colscale=None, rope=None,
              rope_cols=0, rope_shift=0, bias=None, res=None, gate=None, silu_in=False, rows_per_batch=None,
              out_dtype=F32, tm=1024, tn=512, precision=None, name="mm"):
    m = a.shape[0]
    k0, k = a_cols if a_cols is not None else (0, a.shape[1])
    n = w.shape[-1]
    assert w.shape[-2] == k and k0 % k == 0
    tm = min(tm, m)
    tn = min(tn, n)
    assert m % tm == 0 and n % tn == 0
    rows_per_batch = rows_per_batch or m
    assert rows_per_batch % tm == 0
    tpb = rows_per_batch // tm
    has_prologue = gain is not None or mod is not None or silu_in
    in_specs = [pl.BlockSpec((tm, k), lambda i, j: (i, k0 // k))]
    args = [a]
    if w.ndim == 3:
        in_specs.append(pl.BlockSpec((None, k, tn), lambda i, j: (w_index, 0, j)))
    else:
        in_specs.append(pl.BlockSpec((k, tn), lambda i, j: (0, j)))
    args.append(w)
    if gain is not None:
        in_specs.append(pl.BlockSpec((1, k), lambda i, j: (0, 0)))
        args.append(gain.reshape(1, k).astype(F32))
    if mod is not None:
        for t in mod:
            in_specs.append(pl.BlockSpec((None, 1, k), lambda i, j: (i // tpb, 0, 0)))
            args.append(t)
    if colscale is not None:
        in_specs.append(pl.BlockSpec((1, tn), lambda i, j: (0, j)))
        args.append(colscale.reshape(1, n).astype(F32))
    rope_tiles = 0
    if rope is not None:
        assert rope_cols % tn == 0
        rope_tiles = rope_cols // tn
        for t in rope:
            in_specs.append(pl.BlockSpec((tm, LANES), lambda i, j: (i % tpb, 0)))
            args.append(t)
    if bias is not None:
        in_specs.append(pl.BlockSpec((1, tn), lambda i, j: (0, j)))
        args.append(bias.reshape(1, n).astype(F32))
    if res is not None:
        in_specs.append(pl.BlockSpec((tm, tn), lambda i, j: (i, j)))
        args.append(res)
        in_specs.append(pl.BlockSpec((None, 1, tn), lambda i, j: (i // tpb, 0, j)))
        args.append(gate)
    scratch = [pltpu.VMEM((tm, k), F32 if precision is not None else BF16)] if has_prologue else []
    kern = functools.partial(
        _mm_kernel, norm_cols=norm_cols or k, has_gain=gain is not None, has_mod=mod is not None,
        has_colscale=colscale is not None, rope_tiles=rope_tiles, rope_shift=rope_shift, has_bias=bias is not None,
        has_res=res is not None, silu_in=silu_in, precision=precision)
    return pl.pallas_call(
        kern, grid=(m // tm, n // tn), in_specs=in_specs,
        out_specs=pl.BlockSpec((tm, tn), lambda i, j: (i, j)),
        out_shape=jax.ShapeDtypeStruct((m, n), out_dtype), scratch_shapes=scratch,
        compiler_params=_cparams(("parallel", "arbitrary")), name=name)(*args)


def _attn_kernel(*refs, n_src, tk, slice_mode, diff_mode, lambda_init):
    q_ref = refs[0]
    srcs = [(refs[1 + 2 * s], refs[2 + 2 * s]) for s in range(n_src)]
    pos = 1 + 2 * n_src
    if diff_mode:
        lam_ref, subln_ref = refs[pos], refs[pos + 1]
        pos += 2
    o_ref = refs[pos]
    tq = q_ref.shape[0]
    q = q_ref[...]
    lane = lax.broadcasted_iota(jnp.int32, (1, LANES), 1)
    if slice_mode:
        qs = [q[:, :LANES], q[:, LANES:]]
    else:
        qs = [jnp.where(lane < LANES // 2, q, jnp.zeros_like(q)), jnp.where(lane >= LANES // 2, q, jnp.zeros_like(q))]

    def update(carry, k, v):
        out = []
        for s in range(2):
            m, l, acc = carry[s]
            ks = k[:, s * LANES:(s + 1) * LANES] if slice_mode else k
            sc = lax.dot_general(qs[s], ks, (((1,), (1,)), ((), ())), preferred_element_type=F32)
            m_new = jnp.maximum(m, jnp.max(sc, axis=-1, keepdims=True))
            alpha = jnp.exp(m - m_new)
            p = jnp.exp(sc - m_new)
            l = alpha * l + jnp.sum(p, axis=-1, keepdims=True)
            acc = alpha * acc + jnp.dot(p.astype(BF16), v, preferred_element_type=F32)
            out.append((m_new, l, acc))
        return tuple(out)

    init = tuple((jnp.full((tq, 1), MASK_VALUE, F32), jnp.zeros((tq, 1), F32), jnp.zeros((tq, LANES), F32))
                 for _ in range(2))
    carry = init
    for k_ref, v_ref in srcs:
        nk = k_ref.shape[0]
        if nk <= tk:
            carry = update(carry, k_ref[...], v_ref[...])
        else:
            def body(c, cr, k_ref=k_ref, v_ref=v_ref):
                start = pl.multiple_of(c * tk, tk)
                return update(cr, k_ref[pl.ds(start, tk), :], v_ref[pl.ds(start, tk), :])
            carry = lax.fori_loop(0, nk // tk, body, carry)
    o0 = carry[0][2] / carry[0][1]
    o1 = carry[1][2] / carry[1][1]
    if diff_mode:
        lp = lam_ref[...]
        lam = (jnp.exp(jnp.sum(lp[0:1] * lp[1:2], axis=-1, keepdims=True))
               - jnp.exp(jnp.sum(lp[2:3] * lp[3:4], axis=-1, keepdims=True)) + lambda_init)
        o = o0 - lam * o1
        ms = jnp.mean(o * o, axis=-1, keepdims=True)
        o = o * lax.rsqrt(ms + NORM_EPS) * subln_ref[...] * (1.0 - lambda_init)
    else:
        o = jnp.where(lane < LANES // 2, o0, o1)
    o_ref[...] = o.astype(o_ref.dtype)


def _attention(q_arr, q_blk, srcs, *, batch, n_q, groups, slice_mode=False, diff=None, tq=512, tk=512, name="attn"):
    qw = 2 * LANES if slice_mode else LANES
    tq = min(tq, n_q)
    assert n_q % tq == 0
    nqt = n_q // tq
    in_specs = [pl.BlockSpec((tq, qw), lambda b, g, i: (b * nqt + i, q_blk + g))]
    args = [q_arr]
    for k_arr, k_blk, v_arr, v_blk, nk in srcs:
        assert nk <= tk or nk % tk == 0
        in_specs.append(pl.BlockSpec((nk, qw), lambda b, g, i, k_blk=k_blk: (b, k_blk + g)))
        in_specs.append(pl.BlockSpec((nk, LANES), lambda b, g, i, v_blk=v_blk: (b, v_blk + g)))
        args += [k_arr, v_arr]
    lambda_init = 0.0
    if diff is not None:
        lam_params, subln, lambda_init = diff
        in_specs.append(pl.BlockSpec((4, DA_HEAD_DIM), lambda b, g, i: (0, 0)))
        in_specs.append(pl.BlockSpec((1, LANES), lambda b, g, i: (0, 0)))
        args += [lam_params, subln.reshape(1, LANES)]
    kern = functools.partial(_attn_kernel, n_src=len(srcs), tk=tk, slice_mode=slice_mode, diff_mode=diff is not None,
                             lambda_init=lambda_init)
    return pl.pallas_call(
        kern, grid=(batch, groups, nqt), in_specs=in_specs,
        out_specs=pl.BlockSpec((tq, LANES), lambda b, g, i: (b * nqt + i, g)),
        out_shape=jax.ShapeDtypeStruct((batch * n_q, groups * LANES), BF16),
        compiler_params=_cparams(("parallel", "parallel", "arbitrary")), name=name)(*args)


def _na_kernel(q_ref, kc_ref, vc_ref, kl_ref, vl_ref, bias_ref, o_ref, *, n_rows):
    i = pl.program_id(2)
    k0 = jnp.clip(i * NA_Q_ROWS - NA_WIN_ROWS // 2, 0, n_rows - NA_K_ROWS)
    start = pl.multiple_of(k0 * GRID_W, GRID_W * 4)
    nkw = NA_K_ROWS * GRID_W
    kw = kl_ref[pl.ds(start, nkw), :]
    vw = vl_ref[pl.ds(start, nkw), :]
    kc = kc_ref[...]
    vc = vc_ref[...]
    q = q_ref[...]
    lane = lax.broadcasted_iota(jnp.int32, (1, LANES), 1)
    outs = []
    for s in range(2):
        qs = jnp.where((lane < LANES // 2) if s == 0 else (lane >= LANES // 2), q, jnp.zeros_like(q))
        dn = (((1,), (1,)), ((), ()))
        s_win = lax.dot_general(qs, kw, dn, preferred_element_type=F32) + bias_ref[s]
        s_ctx = lax.dot_general(qs, kc, dn, preferred_element_type=F32)
        m = jnp.maximum(jnp.max(s_win, axis=-1, keepdims=True), jnp.max(s_ctx, axis=-1, keepdims=True))
        p_win = jnp.exp(s_win - m)
        p_ctx = jnp.exp(s_ctx - m)
        l = jnp.sum(p_win, axis=-1, keepdims=True) + jnp.sum(p_ctx, axis=-1, keepdims=True)
        o = (jnp.dot(p_win.astype(BF16), vw, preferred_element_type=F32)
             + jnp.dot(p_ctx.astype(BF16), vc, preferred_element_type=F32))
        outs.append(o / l)
    o_ref[...] = jnp.where(lane < LANES // 2, outs[0], outs[1]).astype(o_ref.dtype)


def _na_bias_tables(rpb, n_rows):
    nb = n_rows // NA_Q_ROWS
    assert nb >= 3
    dc = np.arange(GRID_W)[None, :] - np.arange(GRID_W)[:, None] + NA_WIN_COLS - 1
    onehot = (dc[None] == np.arange(2 * NA_WIN_COLS - 1)[:, None, None]).astype(np.float32)
    toep = jnp.einsum('hrd,dqk->hrqk', rpb.astype(F32), jnp.asarray(onehot), precision=lax.Precision.HIGHEST)
    cols = np.arange(GRID_W)
    col_start = np.clip(cols - NA_WIN_COLS // 2, 0, GRID_W - NA_WIN_COLS)
    col_ok = (cols[None, :] >= col_start[:, None]) & (cols[None, :] < col_start[:, None] + NA_WIN_COLS)
    ridx = np.zeros((3, NA_Q_ROWS, NA_K_ROWS), np.int32)
    ok = np.zeros((3, NA_Q_ROWS, NA_K_ROWS), bool)
    for var, blk in enumerate((0, 1, nb - 1)):
        r_first = blk * NA_Q_ROWS
        k_first = int(np.clip(r_first - NA_WIN_ROWS // 2, 0, n_rows - NA_K_ROWS))
        for a in range(NA_Q_ROWS):
            qr = r_first + a
            r0 = int(np.clip(qr - NA_WIN_ROWS // 2, 0, n_rows - NA_WIN_ROWS))
            for kb in range(NA_K_ROWS):
                kr = k_first + kb
                if r0 <= kr < r0 + NA_WIN_ROWS:
                    ok[var, a, kb] = True
                    ridx[var, a, kb] = kr - qr + NA_WIN_ROWS - 1
    tab = toep[:, ridx]
    tab = jnp.transpose(tab, (0, 1, 2, 4, 3, 5))
    mask = ok[:, :, None, :, None] & col_ok[None, None, :, None, :]
    tab = jnp.where(jnp.asarray(mask)[None], tab, MASK_VALUE)
    h = rpb.shape[0]
    return tab.reshape(h, 3, NA_Q_ROWS * GRID_W, NA_K_ROWS * GRID_W)


def _na_attention(qkv_lat, qkv_ctx, bias_tab, *, batch, n_lat, n_ctx):
    d = NA_HEADS * NA_HEAD_DIM
    groups = d // LANES
    n_rows = n_lat // GRID_W
    nb = n_rows // NA_Q_ROWS
    tq = NA_Q_ROWS * GRID_W

    def var_of(i):
        return jnp.where(i == 0, 0, jnp.where(i == nb - 1, 2, 1))

    in_specs = [
        pl.BlockSpec((tq, LANES), lambda b, g, i: (b * nb + i, g)),
        pl.BlockSpec((n_ctx, LANES), lambda b, g, i: (b, groups + g)),
        pl.BlockSpec((n_ctx, LANES), lambda b, g, i: (b, 2 * groups + g)),
        pl.BlockSpec((n_lat, LANES), lambda b, g, i: (b, groups + g)),
        pl.BlockSpec((n_lat, LANES), lambda b, g, i: (b, 2 * groups + g)),
        pl.BlockSpec((2, None, tq, NA_K_ROWS * GRID_W), lambda b, g, i: (g, var_of(i), 0, 0)),
    ]
    return pl.pallas_call(
        functools.partial(_na_kernel, n_rows=n_rows), grid=(batch, groups, nb), in_specs=in_specs,
        out_specs=pl.BlockSpec((tq, LANES), lambda b, g, i: (b * nb + i, g)),
        out_shape=jax.ShapeDtypeStruct((batch * n_lat, d), BF16),
        compiler_params=_cparams(("parallel", "parallel", "arbitrary")), name="na_attn",
    )(qkv_lat, qkv_ctx, qkv_ctx, qkv_lat, qkv_lat, bias_tab)


def _dft_kernel(c_ref, ms_ref, ab_ref, o_ref, acc_ref, *, scale):
    n = pl.program_id(2)
    d = o_ref.shape[1]
    part = (jnp.dot(c_ref[...], ab_ref[:, :d], preferred_element_type=F32)
            + jnp.dot(ms_ref[...], ab_ref[:, d:], preferred_element_type=F32))

    @pl.when(n == 0)
    def _():
        acc_ref[...] = part

    @pl.when(n > 0)
    def _():
        acc_ref[...] += part

    @pl.when(n == pl.num_programs(2) - 1)
    def _():
        o_ref[...] = (acc_ref[...] * scale).astype(o_ref.dtype)


def _pos_dft(cos_t, msin_t, ab, *, batch, n_tok, d, scale, name):
    t = min(1024, n_tok)
    nt = n_tok // t
    return pl.pallas_call(
        functools.partial(_dft_kernel, scale=scale), grid=(batch, nt, nt),
        in_specs=[pl.BlockSpec((t, t), lambda b, k, n: (k, n)),
                  pl.BlockSpec((t, t), lambda b, k, n: (k, n)),
                  pl.BlockSpec((t, 2 * d), lambda b, k, n: (b * nt + n, 0))],
        out_specs=pl.BlockSpec((t, d), lambda b, k, n: (b * nt + k, 0)),
        out_shape=jax.ShapeDtypeStruct((batch * n_tok, d), BF16),
        scratch_shapes=[pltpu.VMEM((t, d), F32)],
        compiler_params=_cparams(("parallel", "parallel", "arbitrary")), name=name)(cos_t, msin_t, ab)


def _dft_tables(n):
    split = 64 if n % 64 == 0 and n > 64 else 1
    m = jnp.arange(n, dtype=jnp.int32)[None, :]

    def phase(kvals):
        ang = ((kvals[:, None] * m) % n).astype(F32) * (2.0 * math.pi / n)
        return jnp.cos(ang), jnp.sin(ang)

    c1, s1 = phase(jnp.arange(n // split, dtype=jnp.int32) * split)
    c2, s2 = phase(jnp.arange(split, dtype=jnp.int32))
    c = c1[:, None, :] * c2[None, :, :] - s1[:, None, :] * s2[None, :, :]
    s = s1[:, None, :] * c2[None, :, :] + c1[:, None, :] * s2[None, :, :]
    return c.reshape(n, n).astype(BF16), (-s).reshape(n, n).astype(BF16)


def _channel_dft_weight(d):
    cg = d // FN_GROUPS
    idx = np.arange(cg)
    ang = 2.0 * np.pi * ((idx[:, None] * idx[None, :]) % cg) / cg
    eye = np.eye(FN_GROUPS)
    return jnp.asarray(np.concatenate([np.kron(eye, np.cos(ang)), np.kron(eye, np.sin(ang))], axis=1), F32)


def _router_kernel(x_ref, gain_ref, sc_ref, sh_ref, wr_ref, h_ref, aff_ref):
    x = x_ref[...]
    ms = jnp.mean(x * x, axis=-1, keepdims=True)
    h = x * lax.rsqrt(ms + NORM_EPS) * gain_ref[...]
    h = h * (1.0 + sc_ref[...]) + sh_ref[...]
    h_ref[...] = h.astype(h_ref.dtype)
    logits = jnp.dot(h, wr_ref[...], preferred_element_type=F32, precision=lax.Precision.HIGHEST)
    e = jnp.exp(logits - jnp.max(logits, axis=-1, keepdims=True))
    aff_ref[...] = e / jnp.sum(e, axis=-1, keepdims=True)


def _norm_router(x, gain, sc, sh, w_router, layer, *, rows_per_batch, tm=512):
    m, d = x.shape
    e = w_router.shape[-1]
    tm = min(tm, rows_per_batch)
    tpb = rows_per_batch // tm
    return pl.pallas_call(
        _router_kernel, grid=(m // tm,),
        in_specs=[pl.BlockSpec((tm, d), lambda i: (i, 0)),
                  pl.BlockSpec((1, d), lambda i: (0, 0)),
                  pl.BlockSpec((None, 1, d), lambda i: (i // tpb, 0, 0)),
                  pl.BlockSpec((None, 1, d), lambda i: (i // tpb, 0, 0)),
                  pl.BlockSpec((None, d, e), lambda i: (layer, 0, 0))],
        out_specs=[pl.BlockSpec((tm, d), lambda i: (i, 0)), pl.BlockSpec((tm, e), lambda i: (i, 0))],
        out_shape=[jax.ShapeDtypeStruct((m, d), BF16), jax.ShapeDtypeStruct((m, e), F32)],
        compiler_params=_cparams(("parallel",)), name="norm_router")(x, gain.reshape(1, d), sc, sh, w_router)


def _ffn_kernel(x_ref, wg_ref, wu_ref, wd_ref, gate_ref, o_ref):
    f = pl.program_id(2)
    x = x_ref[...]
    a = jnp.dot(x, wg_ref[...].astype(BF16), preferred_element_type=F32)
    u = jnp.dot(x, wu_ref[...].astype(BF16), preferred_element_type=F32)
    act = (a * (1.0 / (1.0 + jnp.exp(-a)))) * u
    y = jnp.dot(act.astype(BF16), wd_ref[...].astype(BF16), preferred_element_type=F32)

    @pl.when(f == 0)
    def _():
        o_ref[...] = y

    @pl.when(f > 0)
    def _():
        o_ref[...] += y

    @pl.when(f == pl.num_programs(2) - 1)
    def _():
        o_ref[...] = o_ref[...] * gate_ref[...]


def _moe_ffn(xs, gate, w_gate, w_up, w_down, layer, *, tf=512):
    b, e, c, d = xs.shape
    f = w_gate.shape[-1]
    tf = min(tf, f)
    return pl.pallas_call(
        _ffn_kernel, grid=(e, b, f // tf),
        in_specs=[pl.BlockSpec((None, None, c, d), lambda ei, bi, fi: (bi, ei, 0, 0)),
                  pl.BlockSpec((None, None, d, tf), lambda ei, bi, fi: (layer, ei, 0, fi)),
                  pl.BlockSpec((None, None, d, tf), lambda ei, bi, fi: (layer, ei, 0, fi)),
                  pl.BlockSpec((None, None, tf, d), lambda ei, bi, fi: (layer, ei, fi, 0)),
                  pl.BlockSpec((None, None, c, 1), lambda ei, bi, fi: (bi, ei, 0, 0))],
        out_specs=pl.BlockSpec((None, None, c, d), lambda ei, bi, fi: (bi, ei, 0, 0)),
        out_shape=jax.ShapeDtypeStruct((b, e, c, d), F32),
        compiler_params=_cparams(("parallel", "parallel", "arbitrary")), name="moe_ffn",
    )(xs, w_gate, w_up, w_down, gate)


def _route(aff, h, batch, n_tok):
    e = aff.shape[-1]
    cap = EC_CAPACITY_FACTOR * n_tok // e
    gate, idx = lax.top_k(jnp.swapaxes(aff.reshape(batch, n_tok, e), 1, 2), cap)
    xs = jax.vmap(lambda hb, ib: hb[ib])(h.reshape(batch, n_tok, -1), idx)
    return gate, idx, xs


def _combine(x, g2, y, idx, batch, n_tok):
    d = x.shape[-1]
    upd = jax.vmap(lambda ib, yb: jnp.zeros((n_tok, d), F32).at[ib.reshape(-1)].add(yb.reshape(-1, d)))(idx, y)
    return (x.reshape(batch, n_tok, d) + g2 * upd).reshape(batch * n_tok, d)


def _final_norm_kernel(x_ref, g_ref, o_ref):
    x = x_ref[...]
    ms = jnp.mean(x * x, axis=-1, keepdims=True)
    o_ref[...] = x * lax.rsqrt(ms + NORM_EPS) * g_ref[...]


def _final_norm(x, g, tm=1024):
    m, d = x.shape
    tm = min(tm, m)
    return pl.pallas_call(
        _final_norm_kernel, grid=(m // tm,),
        in_specs=[pl.BlockSpec((tm, d), lambda i: (i, 0)), pl.BlockSpec((1, d), lambda i: (0, 0))],
        out_specs=pl.BlockSpec((tm, d), lambda i: (i, 0)), out_shape=jax.ShapeDtypeStruct((m, d), F32),
        compiler_params=_cparams(("parallel",)), name="final_norm")(x, g.reshape(1, d))


def _rope_tables(n_tok, rot_dim, lane_offset, group):
    t = jnp.arange(n_tok)
    rows = (t // GRID_W).astype(F32)
    cols = (t % GRID_W).astype(F32)
    n_freq = rot_dim // 4
    inv_freq = ROPE_BASE ** (-jnp.arange(n_freq, dtype=F32) / n_freq)
    ang = jnp.concatenate([rows[:, None] * inv_freq, cols[:, None] * inv_freq], axis=-1)
    cos, sin = jnp.cos(ang), jnp.sin(ang)
    half = rot_dim // 2
    lane = np.arange(LANES) % group - lane_offset
    first = (lane >= 0) & (lane < half)
    second = (lane >= half) & (lane < rot_dim)
    pair = np.where(first, lane, np.where(second, lane - half, 0))
    cos_l = jnp.where(jnp.asarray(first | second)[None], cos[:, pair], 1.0)
    sin_l = sin[:, pair]
    sin_next = jnp.where(jnp.asarray(first)[None], -sin_l, 0.0)
    sin_prev = jnp.where(jnp.asarray(second)[None], sin_l, 0.0)
    return cos_l, sin_next, sin_prev


def kernel(x, c, ctx, c_ctx, ada_w, ada_b, norm_mix, norm_ffn, norm_final, da_w_qkv, da_w_o, da_lambda_q1,
           da_lambda_k1, da_lambda_q2, da_lambda_k2, da_subln, fn_w_o, na_w_qkv, na_w_o, na_rpb, mla_w_dq,
           mla_q_norm, mla_w_uq, mla_w_dkv, mla_kv_norm, mla_w_uk, mla_w_uv, mla_w_o, moe_w_router, moe_w_gate,
           moe_w_up, moe_w_down):
    batch, n_lat, d = x.shape
    n_ctx = ctx.shape[1]
    depth = ada_w.shape[0]
    n_mixers = 4
    xl = x.reshape(batch * n_lat, d)
    xc = ctx.reshape(batch * n_ctx, d)

    cond = jnp.concatenate([c, c_ctx[None], jnp.zeros((8 - batch - 1, d), F32)], axis=0)
    mods = [
        _fused_mm(cond, ada_w, w_index=i, bias=ada_b[i], silu_in=True, precision=lax.Precision.HIGHEST, tn=1024,
                  name="ada_mod")
        for i in range(depth)
    ]

    def chunks(i):
        lat = [mods[i][:batch, k * d:(k + 1) * d].reshape(batch, 1, d) for k in range(ADA_CHUNKS)]
        cx = [mods[i][batch:batch + 1, k * d:(k + 1) * d].reshape(1, 1, d) for k in range(ADA_CHUNKS)]
        return lat, cx

    for i in range(depth):
        kind, j = i % n_mixers, i // n_mixers
        keep_ctx = i < depth - 1
        (sh1, sc1, g1, sh2, sc2, g2), (csh1, csc1, cg1, csh2, csc2, cg2) = chunks(i)
        lat_in = dict(gain=norm_mix[i], mod=(sc1, sh1), rows_per_batch=n_lat)
        ctx_in = dict(gain=norm_mix[i], mod=(csc1, csh1), rows_per_batch=batch * n_ctx, tm=n_ctx)

        if kind == 0:
            lambda_init = 0.8 - 0.6 * math.exp(-0.3 * i)
            scale = jnp.concatenate([jnp.full((d,), DA_HEAD_DIM ** -0.5, F32), jnp.ones((2 * d,), F32)])
            rope = _rope_tables(n_lat, DA_HEAD_DIM, 0, DA_HEAD_DIM)
            qkv_l = _fused_mm(xl, da_w_qkv, w_index=j, colscale=scale, rope=rope, rope_cols=2 * d,
                              rope_shift=DA_HEAD_DIM // 2, out_dtype=BF16, name="da_qkv_lat", **lat_in)
            qkv_c = _fused_mm(xc, da_w_qkv, w_index=j, colscale=scale, out_dtype=BF16, name="da_qkv_ctx", **ctx_in)
            groups = d // LANES
            lam_params = jnp.stack([da_lambda_q1[j], da_lambda_k1[j], da_lambda_q2[j], da_lambda_k2[j]])
            diff = (lam_params, da_subln[j], lambda_init)
            o_l = _attention(qkv_l, 0, [(qkv_c, groups, qkv_c, 2 * groups, n_ctx), (qkv_l, groups, qkv_l, 2 * groups, n_lat)],
                             batch=batch, n_q=n_lat, groups=groups, diff=diff, name="da_attn_lat")
            o_c = _attention(qkv_c, 0, [(qkv_c, groups, qkv_c, 2 * groups, n_ctx)], batch=batch, n_q=n_ctx,
                             groups=groups, diff=diff, name="da_attn_ctx") if keep_ctx else None
            w_o = da_w_o
        elif kind == 1:
            w_cd = _channel_dft_weight(d)
            ab_l = _fused_mm(xl, w_cd, out_dtype=BF16, name="fn_chan_lat", **lat_in)
            scale_l = 1.0 / math.sqrt(n_lat * (d // FN_GROUPS))
            o_l = _pos_dft(*_dft_tables(n_lat), ab_l, batch=batch, n_tok=n_lat, d=d, scale=scale_l, name="fn_pos_lat")
            o_c = None
            if keep_ctx:
                ab_c = _fused_mm(xc, w_cd, out_dtype=BF16, name="fn_chan_ctx", **ctx_in)
                scale_c = 1.0 / math.sqrt(n_ctx * (d // FN_GROUPS))
                o_c = _pos_dft(*_dft_tables(n_ctx), ab_c, batch=batch, n_tok=n_ctx, d=d, scale=scale_c,
                               name="fn_pos_ctx")
            w_o = fn_w_o
        elif kind == 2:
            scale = jnp.concatenate([jnp.full((d,), NA_HEAD_DIM ** -0.5, F32), jnp.ones((2 * d,), F32)])
            qkv_l = _fused_mm(xl, na_w_qkv, w_index=j, colscale=scale, out_dtype=BF16, name="na_qkv_lat", **lat_in)
            qkv_c = _fused_mm(xc, na_w_qkv, w_index=j, colscale=scale, out_dtype=BF16, name="na_qkv_ctx", **ctx_in)
            groups = d // LANES
            bias_tab = _na_bias_tables(na_rpb[j], n_lat // GRID_W)
            o_l = _na_attention(qkv_l, qkv_c, bias_tab, batch=batch, n_lat=n_lat, n_ctx=n_ctx)
            o_c = _attention(qkv_c, 0, [(qkv_c, groups, qkv_c, 2 * groups, n_ctx)], batch=batch, n_q=n_ctx,
                             groups=groups, name="na_attn_ctx") if keep_ctx else None
            w_o = na_w_o
        else:
            q_rank = mla_w_dq.shape[-1]
            kv_rank = mla_w_uk.shape[-2]
            hq = MLA_NOPE_DIM + MLA_ROPE_DIM
            down_w = 3 * 384
            assert q_rank == 768 and kv_rank == 256
            w_down = jnp.concatenate([mla_w_dq[j], mla_w_dkv[j], jnp.zeros((d, down_w - q_rank - kv_rank - MLA_ROPE_DIM), F32)],
                                     axis=1)
            wq = mla_w_uq[j].reshape(q_rank, MLA_HEADS, hq)
            wq = jnp.concatenate([wq, jnp.zeros((q_rank, MLA_HEADS, LANES - hq), F32)], axis=-1).reshape(q_rank, MLA_HEADS * LANES)
            wk_nope = jnp.concatenate([mla_w_uk[j].reshape(kv_rank, MLA_HEADS, MLA_NOPE_DIM),
                                       jnp.zeros((kv_rank, MLA_HEADS, LANES - MLA_NOPE_DIM), F32)], axis=-1)
            eye = jnp.eye(MLA_ROPE_DIM, dtype=F32)[:, None, :]
            wk_rope = jnp.concatenate([jnp.zeros((MLA_ROPE_DIM, MLA_HEADS, MLA_NOPE_DIM), F32),
                                       jnp.broadcast_to(eye, (MLA_ROPE_DIM, MLA_HEADS, MLA_ROPE_DIM)),
                                       jnp.zeros((MLA_ROPE_DIM, MLA_HEADS, LANES - hq), F32)], axis=-1)
            kw = MLA_HEADS * LANES
            vw = MLA_HEADS * MLA_V_DIM
            pad_rows = 384 - kv_rank - MLA_ROPE_DIM
            w_kv = jnp.concatenate([
                jnp.concatenate([wk_nope.reshape(kv_rank, kw), mla_w_uv[j]], axis=1),
                jnp.concatenate([wk_rope.reshape(MLA_ROPE_DIM, kw), jnp.zeros((MLA_ROPE_DIM, vw), F32)], axis=1),
                jnp.zeros((pad_rows, kw + vw), F32)], axis=0)
            kv_gain = jnp.concatenate([mla_kv_norm[j], jnp.ones((384 - kv_rank,), F32)])
            rope = _rope_tables(n_lat, MLA_ROPE_DIM, MLA_NOPE_DIM, LANES)
            qscale = jnp.full((kw,), hq ** -0.5, F32)

            def project(xs, inp, n_tok, rope_tabs, tag):
                low = _fused_mm(xs, w_down, tn=384, name="mla_down_" + tag, **inp)
                rp = dict(rope=rope_tabs, rope_shift=MLA_ROPE_DIM // 2) if rope_tabs is not None else {}
                tmx = inp.get("tm", 1024)
                q = _fused_mm(low, wq, a_cols=(0, q_rank), gain=mla_q_norm[j], colscale=qscale, rope_cols=kw,
                              rows_per_batch=inp["rows_per_batch"], tm=tmx, out_dtype=BF16, name="mla_q_" + tag, **rp)
                kv = _fused_mm(low, w_kv, a_cols=(q_rank, 384), gain=kv_gain, norm_cols=kv_rank, rope_cols=kw,
                               rows_per_batch=inp["rows_per_batch"], tm=tmx, out_dtype=BF16, name="mla_kv_" + tag, **rp)
                return q, kv

            q_l, kv_l = project(xl, lat_in, n_lat, rope, "lat")
            q_c, kv_c = project(xc, ctx_in, n_ctx, None, "ctx")
            groups = MLA_HEADS // 2
            o_l = _attention(q_l, 0, [(kv_c, 0, kv_c, 2 * groups, n_ctx), (kv_l, 0, kv_l, 2 * groups, n_lat)], batch=batch,
                             n_q=n_lat, groups=groups, slice_mode=True, name="mla_attn_lat")
            o_c = _attention(q_c, 0, [(kv_c, 0, kv_c, 2 * groups, n_ctx)], batch=batch, n_q=n_ctx, groups=groups,
                             slice_mode=True, name="mla_attn_ctx") if keep_ctx else None
            w_o = mla_w_o

        xl = _fused_mm(o_l, w_o, w_index=j, res=xl, gate=g1, rows_per_batch=n_lat, name="mix_out_lat")
        if keep_ctx:
            xc = _fused_mm(o_c, w_o, w_index=j, res=xc, gate=jnp.broadcast_to(cg1, (1, 1, d)),
                           rows_per_batch=batch * n_ctx, tm=n_ctx, name="mix_out_ctx")

        h_l, aff_l = _norm_router(xl, norm_ffn[i], sc2, sh2, moe_w_router, i, rows_per_batch=n_lat)
        gate_l, idx_l, xs = _route(aff_l, h_l, batch, n_lat)
        gates = gate_l
        if keep_ctx:
            h_c, aff_c = _norm_router(xc, norm_ffn[i], csc2, csh2, moe_w_router, i, rows_per_batch=batch * n_ctx,
                                      tm=n_ctx)
            gate_c, idx_c, xs_c = _route(aff_c, h_c, batch, n_ctx)
            xs = jnp.concatenate([xs, xs_c], axis=2)
            gates = jnp.concatenate([gate_l, gate_c], axis=2)
        y = _moe_ffn(xs, gates[..., None], moe_w_gate, moe_w_up, moe_w_down, i)
        cap_l = gate_l.shape[-1]
        xl = _combine(xl, g2, y[:, :, :cap_l], idx_l, batch, n_lat)
        if keep_ctx:
            xc = _combine(xc, cg2, y[:, :, cap_l:], idx_c, batch, n_ctx)

    return _final_norm(xl, norm_final).reshape(batch, n_lat, d)
```

```python
import functools
import math

import numpy as np
import jax
import jax.numpy as jnp
from jax import lax
from jax.experimental import pallas as pl
from jax.experimental.pallas import tpu as pltpu

F32 = jnp.float32
BF16 = jnp.bfloat16

LANES = 128
VMEM_LIMIT_BYTES = 56 * 1024 * 1024

GRID_W = 64
ROPE_BASE = 10000.0
NORM_EPS = 1e-6
ADA_CHUNKS = 6
DA_HEAD_DIM = 64
NA_HEADS = 16
NA_HEAD_DIM = 64
NA_WIN_ROWS = 8
NA_WIN_COLS = 16
NA_Q_ROWS = 8
NA_K_ROWS = 16
MLA_HEADS = 16
MLA_NOPE_DIM = 64
MLA_ROPE_DIM = 32
MLA_V_DIM = 64
FN_GROUPS = 4
EC_CAPACITY_FACTOR = 2
MASK_VALUE = -1e30
LOG2E = math.log2(math.e)


def _cparams(sem):
    return pltpu.CompilerParams(dimension_semantics=sem, vmem_limit_bytes=VMEM_LIMIT_BYTES)


def _rope_lanes(acc, cos, sin_next, sin_prev, shift):
    outs = []
    for g in range(acc.shape[1] // LANES):
        blk = acc[:, g * LANES:(g + 1) * LANES]
        nxt = pltpu.roll(blk, LANES - shift, 1)
        prv = pltpu.roll(blk, shift, 1)
        outs.append(blk * cos + nxt * sin_next + prv * sin_prev)
    return outs[0] if len(outs) == 1 else jnp.concatenate(outs, axis=1)


def _mm_kernel(*refs, norm_cols, has_gain, has_mod, has_colscale, rope_tiles, rope_shift, has_bias, has_res,
               silu_in, precision):
    it = iter(refs)
    a_ref = next(it)
    w_ref = next(it)
    gain_ref = next(it) if has_gain else None
    sc_ref, sh_ref = (next(it), next(it)) if has_mod else (None, None)
    cs_ref = next(it) if has_colscale else None
    cos_ref, sn_ref, sp_ref = (next(it), next(it), next(it)) if rope_tiles else (None, None, None)
    bias_ref = next(it) if has_bias else None
    res_ref, gate_ref = (next(it), next(it)) if has_res else (None, None)
    o_ref = next(it)
    h_scr = next(it, None)
    j = pl.program_id(1)
    hi = precision is not None

    def prologue():
        a = a_ref[...].astype(F32)
        if silu_in:
            a = a * (1.0 / (1.0 + jnp.exp(-a)))
        if has_gain:
            k = a.shape[1]
            if norm_cols < k:
                col = lax.broadcasted_iota(jnp.int32, (1, k), 1)
                sq = jnp.where(col < norm_cols, a * a, 0.0)
            else:
                sq = a * a
            ms = jnp.sum(sq, axis=-1, keepdims=True) * (1.0 / norm_cols)
            y = a * lax.rsqrt(ms + NORM_EPS) * gain_ref[...]
            a = jnp.where(col < norm_cols, y, a) if norm_cols < k else y
        if has_mod:
            a = a * (1.0 + sc_ref[...]) + sh_ref[...]
        return a if hi else a.astype(BF16)

    if h_scr is not None:
        @pl.when(j == 0)
        def _():
            h_scr[...] = prologue()
        h = h_scr[...]
    else:
        h = a_ref[...] if hi else a_ref[...].astype(BF16)
    w = w_ref[...] if hi else w_ref[...].astype(BF16)
    acc = jnp.dot(h, w, preferred_element_type=F32, precision=precision)
    if has_colscale:
        acc = acc * cs_ref[...]

    def finish(v):
        if has_bias:
            v = v + bias_ref[...]
        if has_res:
            v = res_ref[...] + gate_ref[...] * v
        o_ref[...] = v.astype(o_ref.dtype)

    if rope_tiles:
        @pl.when(j < rope_tiles)
        def _():
            finish(_rope_lanes(acc, cos_ref[...], sn_ref[...], sp_ref[...], rope_shift))

        @pl.when(j >= rope_tiles)
        def _():
            finish(acc)
    else:
        finish(acc)


def _fused_mm(a, w, *, w_index=None, a_cols=None, gain=None, norm_cols=None, mod=None, colscale=None, rope=None,
              rope_cols=0, rope_shift=0, bias=None, res=None, gate=None, silu_in=False, rows_per_batch=None,
              out_dtype=F32, tm=1024, tn=512, precision=None, name="mm"):
    m = a.shape[0]
    k0, k = a_cols if a_cols is not None else (0, a.shape[1])
    n = w.shape[-1]
    assert w.shape[-2] == k and k0 % k == 0
    tm = min(tm, m)
    tn = min(tn, n)
    assert m % tm == 0 and n % tn == 0
    rows_per_batch = rows_per_batch or m
    assert rows_per_batch % tm == 0
    tpb = rows_per_batch // tm
    has_prologue = gain is not None or mod is not None or silu_in
    in_specs = [pl.BlockSpec((tm, k), lambda i, j: (i, k0 // k))]
    args = [a]
    if w.ndim == 3:
        in_specs.append(pl.BlockSpec((None, k, tn), lambda i, j: (w_index, 0, j)))
    else:
        in_specs.append(pl.BlockSpec((k, tn), lambda i, j: (0, j)))
    args.append(w)
    if gain is not None:
        in_specs.append(pl.BlockSpec((1, k), lambda i, j: (0, 0)))
        args.append(gain.reshape(1, k).astype(F32))
    if mod is not None:
        for t in mod:
            in_specs.append(pl.BlockSpec((None, 1, k), lambda i, j: (i // tpb, 0, 0)))
            args.append(t)
    if colscale is not None:
        in_specs.append(pl.BlockSpec((1, tn), lambda i, j: (0, j)))
        args.append(colscale.reshape(1, n).astype(F32))
    rope_tiles = 0
    if rope is not None:
        assert rope_cols % tn == 0
        rope_tiles = rope_cols // tn
        for t in rope:
            in_specs.append(pl.BlockSpec((tm, LANES), lambda i, j: (i % tpb, 0)))
            args.append(t)
    if bias is not None:
        in_specs.append(pl.BlockSpec((1, tn), lambda i, j: (0, j)))
        args.append(bias.reshape(1, n).astype(F32))
    if res is not None:
        in_specs.append(pl.BlockSpec((tm, tn), lambda i, j: (i, j)))
        args.append(res)
        in_specs.append(pl.BlockSpec((None, 1, tn), lambda i, j: (i // tpb, 0, j)))
        args.append(gate)
    scratch = [pltpu.VMEM((tm, k), F32 if precision is not None else BF16)] if has_prologue else []
    kern = functools.partial(
        _mm_kernel, norm_cols=norm_cols or k, has_gain=gain is not None, has_mod=mod is not None,
        has_colscale=colscale is not None, rope_tiles=rope_tiles, rope_shift=rope_shift, has_bias=bias is not None,
        has_res=res is not None, silu_in=silu_in, precision=precision)
    return pl.pallas_call(
        kern, grid=(m // tm, n // tn), in_specs=in_specs,
        out_specs=pl.BlockSpec((tm, tn), lambda i, j: (i, j)),
        out_shape=jax.ShapeDtypeStruct((m, n), out_dtype), scratch_shapes=scratch,
        compiler_params=_cparams(("parallel", "arbitrary")), name=name)(*args)


def _attn_kernel(*refs, n_chunks, tk, rb, slice_mode, diff_mode, lambda_init):
    q_ref, k_ref, v_ref = refs[:3]
    pos = 3
    if diff_mode:
        lam_ref, subln_ref = refs[pos], refs[pos + 1]
        pos += 2
    o_ref = refs[pos]
    per = [refs[pos + 1 + 9 * s:pos + 10 + 9 * s] for s in range(2)]
    qm_scr = [t[0] for t in per]
    s_scr = [t[1:3] for t in per]
    p_scr = [t[3:5] for t in per]
    m_scr = [t[5] for t in per]
    a_scr = [t[6:8] for t in per]
    acc_scr = [t[8] for t in per]
    tq = q_ref.shape[0]
    q = q_ref[...]
    lane = lax.broadcasted_iota(jnp.int32, (1, LANES), 1)
    for s in range(2):
        if slice_mode:
            qm_scr[s][...] = q[:, s * LANES:(s + 1) * LANES]
        else:
            qm_scr[s][...] = jnp.where((lane < LANES // 2) if s == 0 else (lane >= LANES // 2), q, jnp.zeros_like(q))
        m_scr[s][...] = jnp.full(m_scr[s].shape, MASK_VALUE, F32)
        acc_scr[s][...] = jnp.zeros(acc_scr[s].shape, F32)

    def chunk_rows(c):
        return pl.ds(c * tk if isinstance(c, int) else pl.multiple_of(c * tk, tk), tk)

    def scores(c, slot):
        k = k_ref[chunk_rows(c), :]
        for s in range(2):
            ks = k[:, s * LANES:(s + 1) * LANES] if slice_mode else k
            s_scr[s][slot][...] = lax.dot_general(qm_scr[s][...], ks, (((1,), (1,)), ((), ())),
                                                  preferred_element_type=F32)

    def softmax(slot):
        for s in range(2):
            for r in range(tq // rb):
                rows = slice(r * rb, (r + 1) * rb)
                sb = s_scr[s][slot][rows, :]
                m_old = m_scr[s][rows, :]
                m_new = jnp.maximum(m_old, jnp.max(sb, axis=-1, keepdims=True))
                m_scr[s][rows, :] = m_new
                a_scr[s][slot][rows, :] = jnp.exp2(m_old - m_new)
                p_scr[s][slot][rows, :] = jnp.exp2(sb - jnp.tile(m_new, (1, tk // LANES))).astype(BF16)

    def weighted_sum(c, slot):
        v = v_ref[chunk_rows(c), :]
        vext = jnp.concatenate([v, jnp.ones((tk, LANES), BF16)], axis=1)
        for s in range(2):
            alpha = a_scr[s][slot][...]
            pv = jnp.dot(p_scr[s][slot][...], vext, preferred_element_type=F32)
            acc_scr[s][...] = jnp.concatenate([alpha, alpha], axis=1) * acc_scr[s][...] + pv

    def step(c, slot):
        scores(c, slot)
        softmax(1 - slot)
        weighted_sum(c - 2, slot)

    scores(0, 0)
    if n_chunks > 1:
        scores(1, 1)
    softmax(0)
    if n_chunks > 2:
        @pl.loop(0, (n_chunks - 2) // 2)
        def _(j):
            step(2 * j + 2, 0)
            step(2 * j + 3, 1)
        if n_chunks % 2:
            step(n_chunks - 1, 0)
    if n_chunks > 1:
        softmax((n_chunks - 1) % 2)
        weighted_sum(n_chunks - 2, n_chunks % 2)
    weighted_sum(n_chunks - 1, (n_chunks - 1) % 2)
    o0 = acc_scr[0][:, :LANES] / acc_scr[0][:, LANES:]
    o1 = acc_scr[1][:, :LANES] / acc_scr[1][:, LANES:]
    if diff_mode:
        lp = lam_ref[...]
        lam = (jnp.exp(jnp.sum(lp[0:1] * lp[1:2], axis=-1, keepdims=True))
               - jnp.exp(jnp.sum(lp[2:3] * lp[3:4], axis=-1, keepdims=True)) + lambda_init)
        o = o0 - lam * o1
        ms = jnp.mean(o * o, axis=-1, keepdims=True)
        o = o * lax.rsqrt(ms + NORM_EPS) * subln_ref[...] * (1.0 - lambda_init)
    else:
        o = jnp.where(lane < LANES // 2, o0, o1)
    o_ref[...] = o.astype(o_ref.dtype)


def _attention(q_arr, q_blk, k_arr, k_blk, v_arr, v_blk, *, batch, n_q, n_k, groups, slice_mode=False, diff=None,
               tq=512, tk=768, rb=16, name="attn"):
    qw = 2 * LANES if slice_mode else LANES
    tq = min(tq, n_q)
    tk = min(tk, n_k)
    assert n_q % tq == 0 and n_k % tk == 0 and tk % LANES == 0 and tq % rb == 0
    nqt = n_q // tq
    in_specs = [pl.BlockSpec((tq, qw), lambda b, g, i: (b * nqt + i, q_blk + g)),
                pl.BlockSpec((n_k, qw), lambda b, g, i: (b, k_blk + g)),
                pl.BlockSpec((n_k, LANES), lambda b, g, i: (b, v_blk + g))]
    args = [q_arr, k_arr, v_arr]
    lambda_init = 0.0
    if diff is not None:
        lam_params, subln, lambda_init = diff
        in_specs.append(pl.BlockSpec((4, DA_HEAD_DIM), lambda b, g, i: (0, 0)))
        in_specs.append(pl.BlockSpec((1, LANES), lambda b, g, i: (0, 0)))
        args += [lam_params, subln.reshape(1, LANES)]
    kern = functools.partial(_attn_kernel, n_chunks=n_k // tk, tk=tk, rb=rb, slice_mode=slice_mode,
                             diff_mode=diff is not None, lambda_init=lambda_init)
    scratch = 2 * ([pltpu.VMEM((tq, LANES), BF16)] + 2 * [pltpu.VMEM((tq, tk), F32)] + 2 * [pltpu.VMEM((tq, tk), BF16)]
                   + [pltpu.VMEM((tq, LANES), F32)] + 2 * [pltpu.VMEM((tq, LANES), F32)]
                   + [pltpu.VMEM((tq, 2 * LANES), F32)])
    return pl.pallas_call(
        kern, grid=(batch, groups, nqt), in_specs=in_specs,
        out_specs=pl.BlockSpec((tq, LANES), lambda b, g, i: (b * nqt + i, g)),
        out_shape=jax.ShapeDtypeStruct((batch * n_q, groups * LANES), BF16), scratch_shapes=scratch,
        compiler_params=_cparams(("parallel", "parallel", "arbitrary")), name=name)(*args)


def _na_kernel(q_ref, kc_ref, vc_ref, kl_ref, vl_ref, bias_ref, o_ref, *, n_rows):
    i = pl.program_id(2)
    k0 = jnp.clip(i * NA_Q_ROWS - NA_WIN_ROWS // 2, 0, n_rows - NA_K_ROWS)
    start = pl.multiple_of(k0 * GRID_W, GRID_W * 4)
    nkw = NA_K_ROWS * GRID_W
    kw = kl_ref[pl.ds(start, nkw), :]
    vw = vl_ref[pl.ds(start, nkw), :]
    kc = kc_ref[...]
    vc = vc_ref[...]
    q = q_ref[...]
    lane = lax.broadcasted_iota(jnp.int32, (1, LANES), 1)
    outs = []
    for s in range(2):
        qs = jnp.where((lane < LANES // 2) if s == 0 else (lane >= LANES // 2), q, jnp.zeros_like(q))
        dn = (((1,), (1,)), ((), ()))
        s_win = lax.dot_general(qs, kw, dn, preferred_element_type=F32) + bias_ref[s]
        s_ctx = lax.dot_general(qs, kc, dn, preferred_element_type=F32)
        m = jnp.maximum(jnp.max(s_win, axis=-1, keepdims=True), jnp.max(s_ctx, axis=-1, keepdims=True))
        p_win = jnp.exp2(s_win - m)
        p_ctx = jnp.exp2(s_ctx - m)
        l = jnp.sum(p_win, axis=-1, keepdims=True) + jnp.sum(p_ctx, axis=-1, keepdims=True)
        o = (jnp.dot(p_win.astype(BF16), vw, preferred_element_type=F32)
             + jnp.dot(p_ctx.astype(BF16), vc, preferred_element_type=F32))
        outs.append(o / l)
    o_ref[...] = jnp.where(lane < LANES // 2, outs[0], outs[1]).astype(o_ref.dtype)


def _na_bias_tables(rpb, n_rows):
    nb = n_rows // NA_Q_ROWS
    assert nb >= 3
    dc = np.arange(GRID_W)[None, :] - np.arange(GRID_W)[:, None] + NA_WIN_COLS - 1
    onehot = (dc[None] == np.arange(2 * NA_WIN_COLS - 1)[:, None, None]).astype(np.float32)
    toep = jnp.einsum('hrd,dqk->hrqk', rpb.astype(F32), jnp.asarray(onehot), precision=lax.Precision.HIGHEST)
    cols = np.arange(GRID_W)
    col_start = np.clip(cols - NA_WIN_COLS // 2, 0, GRID_W - NA_WIN_COLS)
    col_ok = (cols[None, :] >= col_start[:, None]) & (cols[None, :] < col_start[:, None] + NA_WIN_COLS)
    ridx = np.zeros((3, NA_Q_ROWS, NA_K_ROWS), np.int32)
    ok = np.zeros((3, NA_Q_ROWS, NA_K_ROWS), bool)
    for var, blk in enumerate((0, 1, nb - 1)):
        r_first = blk * NA_Q_ROWS
        k_first = int(np.clip(r_first - NA_WIN_ROWS // 2, 0, n_rows - NA_K_ROWS))
        for a in range(NA_Q_ROWS):
            qr = r_first + a
            r0 = int(np.clip(qr - NA_WIN_ROWS // 2, 0, n_rows - NA_WIN_ROWS))
            for kb in range(NA_K_ROWS):
                kr = k_first + kb
                if r0 <= kr < r0 + NA_WIN_ROWS:
                    ok[var, a, kb] = True
                    ridx[var, a, kb] = kr - qr + NA_WIN_ROWS - 1
    tab = toep[:, ridx]
    tab = jnp.transpose(tab, (0, 1, 2, 4, 3, 5))
    mask = ok[:, :, None, :, None] & col_ok[None, None, :, None, :]
    tab = jnp.where(jnp.asarray(mask)[None], tab, MASK_VALUE)
    h = rpb.shape[0]
    return tab.reshape(h, 3, NA_Q_ROWS * GRID_W, NA_K_ROWS * GRID_W)


def _na_attention(qkv_lat, qkv_ctx, bias_tab, *, batch, n_lat, n_ctx):
    d = NA_HEADS * NA_HEAD_DIM
    groups = d // LANES
    n_rows = n_lat // GRID_W
    nb = n_rows // NA_Q_ROWS
    tq = NA_Q_ROWS * GRID_W

    def var_of(i):
        return jnp.where(i == 0, 0, jnp.where(i == nb - 1, 2, 1))

    in_specs = [
        pl.BlockSpec((tq, LANES), lambda b, g, i: (b * nb + i, g)),
        pl.BlockSpec((n_ctx, LANES), lambda b, g, i: (b, groups + g)),
        pl.BlockSpec((n_ctx, LANES), lambda b, g, i: (b, 2 * groups + g)),
        pl.BlockSpec((n_lat, LANES), lambda b, g, i: (b, groups + g)),
        pl.BlockSpec((n_lat, LANES), lambda b, g, i: (b, 2 * groups + g)),
        pl.BlockSpec((2, None, tq, NA_K_ROWS * GRID_W), lambda b, g, i: (g, var_of(i), 0, 0)),
    ]
    return pl.pallas_call(
        functools.partial(_na_kernel, n_rows=n_rows), grid=(batch, groups, nb), in_specs=in_specs,
        out_specs=pl.BlockSpec((tq, LANES), lambda b, g, i: (b * nb + i, g)),
        out_shape=jax.ShapeDtypeStruct((batch * n_lat, d), BF16),
        compiler_params=_cparams(("parallel", "parallel", "arbitrary")), name="na_attn",
    )(qkv_lat, qkv_ctx, qkv_ctx, qkv_lat, qkv_lat, bias_tab)


def _dft_kernel(c_ref, ms_ref, ab_ref, o_ref, acc_ref, *, scale):
    n = pl.program_id(2)
    d = o_ref.shape[1]
    part = (jnp.dot(c_ref[...], ab_ref[:, :d], preferred_element_type=F32)
            + jnp.dot(ms_ref[...], ab_ref[:, d:], preferred_element_type=F32))

    @pl.when(n == 0)
    def _():
        acc_ref[...] = part

    @pl.when(n > 0)
    def _():
        acc_ref[...] += part

    @pl.when(n == pl.num_programs(2) - 1)
    def _():
        o_ref[...] = (acc_ref[...] * scale).astype(o_ref.dtype)


def _pos_dft(cos_t, msin_t, ab, *, batch, n_tok, d, scale, name):
    t = min(1024, n_tok)
    nt = n_tok // t
    return pl.pallas_call(
        functools.partial(_dft_kernel, scale=scale), grid=(batch, nt, nt),
        in_specs=[pl.BlockSpec((t, t), lambda b, k, n: (k, n)),
                  pl.BlockSpec((t, t), lambda b, k, n: (k, n)),
                  pl.BlockSpec((t, 2 * d), lambda b, k, n: (b * nt + n, 0))],
        out_specs=pl.BlockSpec((t, d), lambda b, k, n: (b * nt + k, 0)),
        out_shape=jax.ShapeDtypeStruct((batch * n_tok, d), BF16),
        scratch_shapes=[pltpu.VMEM((t, d), F32)],
        compiler_params=_cparams(("parallel", "parallel", "arbitrary")), name=name)(cos_t, msin_t, ab)


def _dft_tables(n):
    split = 64 if n % 64 == 0 and n > 64 else 1
    m = jnp.arange(n, dtype=jnp.int32)[None, :]

    def phase(kvals):
        ang = ((kvals[:, None] * m) % n).astype(F32) * (2.0 * math.pi / n)
        return jnp.cos(ang), jnp.sin(ang)

    c1, s1 = phase(jnp.arange(n // split, dtype=jnp.int32) * split)
    c2, s2 = phase(jnp.arange(split, dtype=jnp.int32))
    c = c1[:, None, :] * c2[None, :, :] - s1[:, None, :] * s2[None, :, :]
    s = s1[:, None, :] * c2[None, :, :] + c1[:, None, :] * s2[None, :, :]
    return c.reshape(n, n).astype(BF16), (-s).reshape(n, n).astype(BF16)


def _channel_dft_weight(d):
    cg = d // FN_GROUPS
    idx = np.arange(cg)
    ang = 2.0 * np.pi * ((idx[:, None] * idx[None, :]) % cg) / cg
    eye = np.eye(FN_GROUPS)
    return jnp.asarray(np.concatenate([np.kron(eye, np.cos(ang)), np.kron(eye, np.sin(ang))], axis=1), F32)


def _router_kernel(x_ref, gain_ref, sc_ref, sh_ref, wr_ref, h_ref, aff_ref):
    x = x_ref[...]
    ms = jnp.mean(x * x, axis=-1, keepdims=True)
    h = x * lax.rsqrt(ms + NORM_EPS) * gain_ref[...]
    h = h * (1.0 + sc_ref[...]) + sh_ref[...]
    h_ref[...] = h.astype(h_ref.dtype)
    logits = jnp.dot(h, wr_ref[...], preferred_element_type=F32, precision=lax.Precision.HIGHEST)
    e = jnp.exp(logits - jnp.max(logits, axis=-1, keepdims=True))
    aff_ref[...] = e / jnp.sum(e, axis=-1, keepdims=True)


def _norm_router(x, gain, sc, sh, w_router, layer, *, rows_per_batch, tm=512):
    m, d = x.shape
    e = w_router.shape[-1]
    tm = min(tm, rows_per_batch)
    tpb = rows_per_batch // tm
    return pl.pallas_call(
        _router_kernel, grid=(m // tm,),
        in_specs=[pl.BlockSpec((tm, d), lambda i: (i, 0)),
                  pl.BlockSpec((1, d), lambda i: (0, 0)),
                  pl.BlockSpec((None, 1, d), lambda i: (i // tpb, 0, 0)),
                  pl.BlockSpec((None, 1, d), lambda i: (i // tpb, 0, 0)),
                  pl.BlockSpec((None, d, e), lambda i: (layer, 0, 0))],
        out_specs=[pl.BlockSpec((tm, d), lambda i: (i, 0)), pl.BlockSpec((tm, e), lambda i: (i, 0))],
        out_shape=[jax.ShapeDtypeStruct((m, d), BF16), jax.ShapeDtypeStruct((m, e), F32)],
        compiler_params=_cparams(("parallel",)), name="norm_router")(x, gain.reshape(1, d), sc, sh, w_router)


def _ffn_kernel(x_ref, wg_ref, wu_ref, wd_ref, gate_ref, o_ref):
    f = pl.program_id(2)
    x = x_ref[...]
    a = jnp.dot(x, wg_ref[...].astype(BF16), preferred_element_type=F32)
    u = jnp.dot(x, wu_ref[...].astype(BF16), preferred_element_type=F32)
    act = (a * (1.0 / (1.0 + jnp.exp(-a)))) * u
    y = jnp.dot(act.astype(BF16), wd_ref[...].astype(BF16), preferred_element_type=F32)

    @pl.when(f == 0)
    def _():
        o_ref[...] = y

    @pl.when(f > 0)
    def _():
        o_ref[...] += y

    @pl.when(f == pl.num_programs(2) - 1)
    def _():
        o_ref[...] = o_ref[...] * gate_ref[...]


def _moe_ffn(xs, gate, w_gate, w_up, w_down, layer, *, tf=512):
    b, e, c, d = xs.shape
    f = w_gate.shape[-1]
    tf = min(tf, f)
    return pl.pallas_call(
        _ffn_kernel, grid=(e, b, f // tf),
        in_specs=[pl.BlockSpec((None, None, c, d), lambda ei, bi, fi: (bi, ei, 0, 0)),
                  pl.BlockSpec((None, None, d, tf), lambda ei, bi, fi: (layer, ei, 0, fi)),
                  pl.BlockSpec((None, None, d, tf), lambda ei, bi, fi: (layer, ei, 0, fi)),
                  pl.BlockSpec((None, None, tf, d), lambda ei, bi, fi: (layer, ei, fi, 0)),
                  pl.BlockSpec((None, None, c, 1), lambda ei, bi, fi: (bi, ei, 0, 0))],
        out_specs=pl.BlockSpec((None, None, c, d), lambda ei, bi, fi: (bi, ei, 0, 0)),
        out_shape=jax.ShapeDtypeStruct((b, e, c, d), F32),
        compiler_params=_cparams(("parallel", "parallel", "arbitrary")), name="moe_ffn",
    )(xs, w_gate, w_up, w_down, gate)


def _route(aff, h, batch, n_tok):
    e = aff.shape[-1]
    cap = EC_CAPACITY_FACTOR * n_tok // e
    gate, idx = lax.top_k(jnp.swapaxes(aff.reshape(batch, n_tok, e), 1, 2), cap)
    xs = jax.vmap(lambda hb, ib: hb[ib])(h.reshape(batch, n_tok, -1), idx)
    return gate, idx, xs


def _combine(x, g2, y, idx, batch, n_tok):
    d = x.shape[-1]
    upd = jax.vmap(lambda ib, yb: jnp.zeros((n_tok, d), F32).at[ib.reshape(-1)].add(yb.reshape(-1, d)))(idx, y)
    return (x.reshape(batch, n_tok, d) + g2 * upd).reshape(batch * n_tok, d)


def _final_norm_kernel(x_ref, g_ref, o_ref):
    x = x_ref[...]
    ms = jnp.mean(x * x, axis=-1, keepdims=True)
    o_ref[...] = x * lax.rsqrt(ms + NORM_EPS) * g_ref[...]


def _final_norm(x, g, tm=1024):
    m, d = x.shape
    tm = min(tm, m)
    return pl.pallas_call(
        _final_norm_kernel, grid=(m // tm,),
        in_specs=[pl.BlockSpec((tm, d), lambda i: (i, 0)), pl.BlockSpec((1, d), lambda i: (0, 0))],
        out_specs=pl.BlockSpec((tm, d), lambda i: (i, 0)), out_shape=jax.ShapeDtypeStruct((m, d), F32),
        compiler_params=_cparams(("parallel",)), name="final_norm")(x, g.reshape(1, d))


def _rope_tables(n_tok, rot_dim, lane_offset, group):
    t = jnp.arange(n_tok)
    rows = (t // GRID_W).astype(F32)
    cols = (t % GRID_W).astype(F32)
    n_freq = rot_dim // 4
    inv_freq = ROPE_BASE ** (-jnp.arange(n_freq, dtype=F32) / n_freq)
    ang = jnp.concatenate([rows[:, None] * inv_freq, cols[:, None] * inv_freq], axis=-1)
    cos, sin = jnp.cos(ang), jnp.sin(ang)
    half = rot_dim // 2
    lane = np.arange(LANES) % group - lane_offset
    first = (lane >= 0) & (lane < half)
    second = (lane >= half) & (lane < rot_dim)
    pair = np.where(first, lane, np.where(second, lane - half, 0))
    cos_l = jnp.where(jnp.asarray(first | second)[None], cos[:, pair], 1.0)
    sin_l = sin[:, pair]
    sin_next = jnp.where(jnp.asarray(first)[None], -sin_l, 0.0)
    sin_prev = jnp.where(jnp.asarray(second)[None], sin_l, 0.0)
    return cos_l, sin_next, sin_prev


def kernel(x, c, ctx, c_ctx, ada_w, ada_b, norm_mix, norm_ffn, norm_final, da_w_qkv, da_w_o, da_lambda_q1,
           da_lambda_k1, da_lambda_q2, da_lambda_k2, da_subln, fn_w_o, na_w_qkv, na_w_o, na_rpb, mla_w_dq,
           mla_q_norm, mla_w_uq, mla_w_dkv, mla_kv_norm, mla_w_uk, mla_w_uv, mla_w_o, moe_w_router, moe_w_gate,
           moe_w_up, moe_w_down):
    batch, n_lat, d = x.shape
    n_ctx = ctx.shape[1]
    depth = ada_w.shape[0]
    n_mixers = 4
    xl = x.reshape(batch * n_lat, d)
    xc = ctx.reshape(batch * n_ctx, d)

    cond = jnp.concatenate([c, c_ctx[None], jnp.zeros((8 - batch - 1, d), F32)], axis=0)
    mods = [
        _fused_mm(cond, ada_w, w_index=i, bias=ada_b[i], silu_in=True, precision=lax.Precision.HIGHEST, tn=1024,
                  name="ada_mod")
        for i in range(depth)
    ]

    def chunks(i):
        lat = [mods[i][:batch, k * d:(k + 1) * d].reshape(batch, 1, d) for k in range(ADA_CHUNKS)]
        cx = [mods[i][batch:batch + 1, k * d:(k + 1) * d].reshape(1, 1, d) for k in range(ADA_CHUNKS)]
        return lat, cx

    def all_keys(t_ctx, t_lat):
        w = t_ctx.shape[-1]
        both = jnp.concatenate([t_ctx.reshape(batch, n_ctx, w), t_lat.reshape(batch, n_lat, w)], axis=1)
        return both.reshape(batch * (n_ctx + n_lat), w)

    for i in range(depth):
        kind, j = i % n_mixers, i // n_mixers
        keep_ctx = i < depth - 1
        (sh1, sc1, g1, sh2, sc2, g2), (csh1, csc1, cg1, csh2, csc2, cg2) = chunks(i)
        lat_in = dict(gain=norm_mix[i], mod=(sc1, sh1), rows_per_batch=n_lat)
        ctx_in = dict(gain=norm_mix[i], mod=(csc1, csh1), rows_per_batch=batch * n_ctx, tm=n_ctx)

        if kind == 0:
            lambda_init = 0.8 - 0.6 * math.exp(-0.3 * i)
            scale = jnp.concatenate([jnp.full((d,), LOG2E * DA_HEAD_DIM ** -0.5, F32), jnp.ones((2 * d,), F32)])
            rope = _rope_tables(n_lat, DA_HEAD_DIM, 0, DA_HEAD_DIM)
            qkv_l = _fused_mm(xl, da_w_qkv, w_index=j, colscale=scale, rope=rope, rope_cols=2 * d,
                              rope_shift=DA_HEAD_DIM // 2, out_dtype=BF16, name="da_qkv_lat", **lat_in)
            qkv_c = _fused_mm(xc, da_w_qkv, w_index=j, colscale=scale, out_dtype=BF16, name="da_qkv_ctx", **ctx_in)
            groups = d // LANES
            lam_params = jnp.stack([da_lambda_q1[j], da_lambda_k1[j], da_lambda_q2[j], da_lambda_k2[j]])
            diff = (lam_params, da_subln[j], lambda_init)
            kv = all_keys(qkv_c[:, d:], qkv_l[:, d:])
            o_l = _attention(qkv_l, 0, kv, 0, kv, groups, batch=batch, n_q=n_lat, n_k=n_ctx + n_lat, groups=groups,
                             diff=diff, name="da_attn_lat")
            o_c = _attention(qkv_c, 0, qkv_c, groups, qkv_c, 2 * groups, batch=batch, n_q=n_ctx, n_k=n_ctx,
                             groups=groups, diff=diff, name="da_attn_ctx") if keep_ctx else None
            w_o = da_w_o
        elif kind == 1:
            w_cd = _channel_dft_weight(d)
            ab_l = _fused_mm(xl, w_cd, out_dtype=BF16, name="fn_chan_lat", **lat_in)
            scale_l = 1.0 / math.sqrt(n_lat * (d // FN_GROUPS))
            o_l = _pos_dft(*_dft_tables(n_lat), ab_l, batch=batch, n_tok=n_lat, d=d, scale=scale_l, name="fn_pos_lat")
            o_c = None
            if keep_ctx:
                ab_c = _fused_mm(xc, w_cd, out_dtype=BF16, name="fn_chan_ctx", **ctx_in)
                scale_c = 1.0 / math.sqrt(n_ctx * (d // FN_GROUPS))
                o_c = _pos_dft(*_dft_tables(n_ctx), ab_c, batch=batch, n_tok=n_ctx, d=d, scale=scale_c,
                               name="fn_pos_ctx")
            w_o = fn_w_o
        elif kind == 2:
            scale = jnp.concatenate([jnp.full((d,), LOG2E * NA_HEAD_DIM ** -0.5, F32), jnp.ones((2 * d,), F32)])
            qkv_l = _fused_mm(xl, na_w_qkv, w_index=j, colscale=scale, out_dtype=BF16, name="na_qkv_lat", **lat_in)
            qkv_c = _fused_mm(xc, na_w_qkv, w_index=j, colscale=scale, out_dtype=BF16, name="na_qkv_ctx", **ctx_in)
            groups = d // LANES
            bias_tab = _na_bias_tables(na_rpb[j] * LOG2E, n_lat // GRID_W)
            o_l = _na_attention(qkv_l, qkv_c, bias_tab, batch=batch, n_lat=n_lat, n_ctx=n_ctx)
            o_c = _attention(qkv_c, 0, qkv_c, groups, qkv_c, 2 * groups, batch=batch, n_q=n_ctx, n_k=n_ctx,
                             groups=groups, name="na_attn_ctx") if keep_ctx else None
            w_o = na_w_o
        else:
            q_rank = mla_w_dq.shape[-1]
            kv_rank = mla_w_uk.shape[-2]
            hq = MLA_NOPE_DIM + MLA_ROPE_DIM
            down_w = 3 * 384
            assert q_rank == 768 and kv_rank == 256
            w_down = jnp.concatenate([mla_w_dq[j], mla_w_dkv[j], jnp.zeros((d, down_w - q_rank - kv_rank - MLA_ROPE_DIM), F32)],
                                     axis=1)
            wq = mla_w_uq[j].reshape(q_rank, MLA_HEADS, hq)
            wq = jnp.concatenate([wq, jnp.zeros((q_rank, MLA_HEADS, LANES - hq), F32)], axis=-1).reshape(q_rank, MLA_HEADS * LANES)
            wk_nope = jnp.concatenate([mla_w_uk[j].reshape(kv_rank, MLA_HEADS, MLA_NOPE_DIM),
                                       jnp.zeros((kv_rank, MLA_HEADS, LANES - MLA_NOPE_DIM), F32)], axis=-1)
            eye = jnp.eye(MLA_ROPE_DIM, dtype=F32)[:, None, :]
            wk_rope = jnp.concatenate([jnp.zeros((MLA_ROPE_DIM, MLA_HEADS, MLA_NOPE_DIM), F32),
                                       jnp.broadcast_to(eye, (MLA_ROPE_DIM, MLA_HEADS, MLA_ROPE_DIM)),
                                       jnp.zeros((MLA_ROPE_DIM, MLA_HEADS, LANES - hq), F32)], axis=-1)
            kw = MLA_HEADS * LANES
            vw = MLA_HEADS * MLA_V_DIM
            pad_rows = 384 - kv_rank - MLA_ROPE_DIM
            w_kv = jnp.concatenate([
                jnp.concatenate([wk_nope.reshape(kv_rank, kw), mla_w_uv[j]], axis=1),
                jnp.concatenate([wk_rope.reshape(MLA_ROPE_DIM, kw), jnp.zeros((MLA_ROPE_DIM, vw), F32)], axis=1),
                jnp.zeros((pad_rows, kw + vw), F32)], axis=0)
            kv_gain = jnp.concatenate([mla_kv_norm[j], jnp.ones((384 - kv_rank,), F32)])
            rope = _rope_tables(n_lat, MLA_ROPE_DIM, MLA_NOPE_DIM, LANES)
            qscale = jnp.full((kw,), LOG2E * hq ** -0.5, F32)

            def project(xs, inp, n_tok, rope_tabs, tag):
                low = _fused_mm(xs, w_down, tn=384, name="mla_down_" + tag, **inp)
                rp = dict(rope=rope_tabs, rope_shift=MLA_ROPE_DIM // 2) if rope_tabs is not None else {}
                tmx = inp.get("tm", 1024)
                q = _fused_mm(low, wq, a_cols=(0, q_rank), gain=mla_q_norm[j], colscale=qscale, rope_cols=kw,
                              rows_per_batch=inp["rows_per_batch"], tm=tmx, out_dtype=BF16, name="mla_q_" + tag, **rp)
                kv = _fused_mm(low, w_kv, a_cols=(q_rank, 384), gain=kv_gain, norm_cols=kv_rank, rope_cols=kw,
                               rows_per_batch=inp["rows_per_batch"], tm=tmx, out_dtype=BF16, name="mla_kv_" + tag, **rp)
                return q, kv

            q_l, kv_l = project(xl, lat_in, n_lat, rope, "lat")
            q_c, kv_c = project(xc, ctx_in, n_ctx, None, "ctx")
            groups = MLA_HEADS // 2
            kv = all_keys(kv_c, kv_l)
            o_l = _attention(q_l, 0, kv, 0, kv, 2 * groups, batch=batch, n_q=n_lat, n_k=n_ctx + n_lat, groups=groups,
                             slice_mode=True, name="mla_attn_lat")
            o_c = _attention(q_c, 0, kv_c, 0, kv_c, 2 * groups, batch=batch, n_q=n_ctx, n_k=n_ctx, groups=groups,
                             slice_mode=True, name="mla_attn_ctx") if keep_ctx else None
            w_o = mla_w_o

        xl = _fused_mm(o_l, w_o, w_index=j, res=xl, gate=g1, rows_per_batch=n_lat, name="mix_out_lat")
        if keep_ctx:
            xc = _fused_mm(o_c, w_o, w_index=j, res=xc, gate=jnp.broadcast_to(cg1, (1, 1, d)),
                           rows_per_batch=batch * n_ctx, tm=n_ctx, name="mix_out_ctx")

        h_l, aff_l = _norm_router(xl, norm_ffn[i], sc2, sh2, moe_w_router, i, rows_per_batch=n_lat)
        gate_l, idx_l, xs = _route(aff_l, h_l, batch, n_lat)
        gates = gate_l
        if keep_ctx:
            h_c, aff_c = _norm_router(xc, norm_ffn[i], csc2, csh2, moe_w_router, i, rows_per_batch=batch * n_ctx,
                                      tm=n_ctx)
            gate_c, idx_c, xs_c = _route(aff_c, h_c, batch, n_ctx)
            xs = jnp.concatenate([xs, xs_c], axis=2)
            gates = jnp.concatenate([gate_l, gate_c], axis=2)
        y = _moe_ffn(xs, gates[..., None], moe_w_gate, moe_w_up, moe_w_down, i)
        cap_l = gate_l.shape[-1]
        xl = _combine(xl, g2, y[:, :, :cap_l], idx_l, batch, n_lat)
        if keep_ctx:
            xc = _combine(xc, cg2, y[:, :, cap_l:], idx_c, batch, n_ctx)

    return _final_norm(xl, norm_final).reshape(batch, n_lat, d)
```

```python
import functools
import math

import numpy as np
import jax
import jax.numpy as jnp
from jax import lax
from jax.experimental import pallas as pl
from jax.experimental.pallas import tpu as pltpu

F32 = jnp.float32
BF16 = jnp.bfloat16

LANES = 128
VMEM_LIMIT_BYTES = 56 * 1024 * 1024

GRID_W = 64
ROPE_BASE = 10000.0
NORM_EPS = 1e-6
ADA_CHUNKS = 6
DA_HEAD_DIM = 64
NA_HEADS = 16
NA_HEAD_DIM = 64
NA_WIN_ROWS = 8
NA_WIN_COLS = 16
NA_Q_ROWS = 8
NA_K_ROWS = 16
MLA_HEADS = 16
MLA_NOPE_DIM = 64
MLA_ROPE_DIM = 32
MLA_V_DIM = 64
FN_GROUPS = 4
EC_CAPACITY_FACTOR = 2
MASK_VALUE = -1e30
LOG2E = math.log2(math.e)


def _cparams(sem):
    return pltpu.CompilerParams(dimension_semantics=sem, vmem_limit_bytes=VMEM_LIMIT_BYTES)


def _rope_lanes(acc, cos, sin_next, sin_prev, shift):
    outs = []
    for g in range(acc.shape[1] // LANES):
        blk = acc[:, g * LANES:(g + 1) * LANES]
        nxt = pltpu.roll(blk, LANES - shift, 1)
        prv = pltpu.roll(blk, shift, 1)
        outs.append(blk * cos + nxt * sin_next + prv * sin_prev)
    return outs[0] if len(outs) == 1 else jnp.concatenate(outs, axis=1)


def _mm_kernel(*refs, norm_cols, has_gain, has_mod, has_colscale, rope_tiles, rope_shift, has_bias, has_res,
               silu_in, precision):
    it = iter(refs)
    a_ref = next(it)
    w_ref = next(it)
    gain_ref = next(it) if has_gain else None
    sc_ref, sh_ref = (next(it), next(it)) if has_mod else (None, None)
    cs_ref = next(it) if has_colscale else None
    cos_ref, sn_ref, sp_ref = (next(it), next(it), next(it)) if rope_tiles else (None, None, None)
    bias_ref = next(it) if has_bias else None
    res_ref, gate_ref = (next(it), next(it)) if has_res else (None, None)
    o_ref = next(it)
    h_scr = next(it, None)
    j = pl.program_id(1)
    hi = precision is not None

    def prologue():
        a = a_ref[...].astype(F32)
        if silu_in:
            a = a * (1.0 / (1.0 + jnp.exp(-a)))
        if has_gain:
            k = a.shape[1]
            if norm_cols < k:
                col = lax.broadcasted_iota(jnp.int32, (1, k), 1)
                sq = jnp.where(col < norm_cols, a * a, 0.0)
            else:
                sq = a * a
            ms = jnp.sum(sq, axis=-1, keepdims=True) * (1.0 / norm_cols)
            y = a * lax.rsqrt(ms + NORM_EPS) * gain_ref[...]
            a = jnp.where(col < norm_cols, y, a) if norm_cols < k else y
        if has_mod:
            a = a * (1.0 + sc_ref[...]) + sh_ref[...]
        return a if hi else a.astype(BF16)

    if h_scr is not None:
        @pl.when(j == 0)
        def _():
            h_scr[...] = prologue()
        h = h_scr[...]
    else:
        h = a_ref[...] if hi else a_ref[...].astype(BF16)
    w = w_ref[...] if hi else w_ref[...].astype(BF16)
    acc = jnp.dot(h, w, preferred_element_type=F32, precision=precision)
    if has_colscale:
        acc = acc * cs_ref[...]

    def finish(v):
        if has_bias:
            v = v + bias_ref[...]
        if has_res:
            v = res_ref[...] + gate_ref[...] * v
        o_ref[...] = v.astype(o_ref.dtype)

    if rope_tiles:
        @pl.when(j < rope_tiles)
        def _():
            finish(_rope_lanes(acc, cos_ref[...], sn_ref[...], sp_ref[...], rope_shift))

        @pl.when(j >= rope_tiles)
        def _():
            finish(acc)
    else:
        finish(acc)


def _fused_mm(a, w, *, w_index=None, a_cols=None, gain=None, norm_cols=None, mod=None, colscale=None, rope=None,
              rope_cols=0, rope_shift=0, bias=None, res=None, gate=None, silu_in=False, rows_per_batch=None,
              out_dtype=F32, tm=1024, tn=512, precision=None, name="mm"):
    m = a.shape[0]
    k0, k = a_cols if a_cols is not None else (0, a.shape[1])
    n = w.shape[-1]
    assert w.shape[-2] == k and k0 % k == 0
    tm = min(tm, m)
    tn = min(tn, n)
    assert m % tm == 0 and n % tn == 0
    rows_per_batch = rows_per_batch or m
    assert rows_per_batch % tm == 0
    tpb = rows_per_batch // tm
    has_prologue = gain is not None or mod is not None or silu_in
    in_specs = [pl.BlockSpec((tm, k), lambda i, j: (i, k0 // k))]
    args = [a]
    if w.ndim == 3:
        in_specs.append(pl.BlockSpec((None, k, tn), lambda i, j: (w_index, 0, j)))
    else:
        in_specs.append(pl.BlockSpec((k, tn), lambda i, j: (0, j)))
    args.append(w)
    if gain is not None:
        in_specs.append(pl.BlockSpec((1, k), lambda i, j: (0, 0)))
        args.append(gain.reshape(1, k).astype(F32))
    if mod is not None:
        for t in mod:
            in_specs.append(pl.BlockSpec((None, 1, k), lambda i, j: (i // tpb, 0, 0)))
            args.append(t)
    if colscale is not None:
        in_specs.append(pl.BlockSpec((1, tn), lambda i, j: (0, j)))
        args.append(colscale.reshape(1, n).astype(F32))
    rope_tiles = 0
    if rope is not None:
        assert rope_cols % tn == 0
        rope_tiles = rope_cols // tn
        for t in rope:
            in_specs.append(pl.BlockSpec((tm, LANES), lambda i, j: (i % tpb, 0)))
            args.append(t)
    if bias is not None:
        in_specs.append(pl.BlockSpec((1, tn), lambda i, j: (0, j)))
        args.append(bias.reshape(1, n).astype(F32))
    if res is not None:
        in_specs.append(pl.BlockSpec((tm, tn), lambda i, j: (i, j)))
        args.append(res)
        in_specs.append(pl.BlockSpec((None, 1, tn), lambda i, j: (i // tpb, 0, j)))
        args.append(gate)
    scratch = [pltpu.VMEM((tm, k), F32 if precision is not None else BF16)] if has_prologue else []
    kern = functools.partial(
        _mm_kernel, norm_cols=norm_cols or k, has_gain=gain is not None, has_mod=mod is not None,
        has_colscale=colscale is not None, rope_tiles=rope_tiles, rope_shift=rope_shift, has_bias=bias is not None,
        has_res=res is not None, silu_in=silu_in, precision=precision)
    return pl.pallas_call(
        kern, grid=(m // tm, n // tn), in_specs=in_specs,
        out_specs=pl.BlockSpec((tm, tn), lambda i, j: (i, j)),
        out_shape=jax.ShapeDtypeStruct((m, n), out_dtype), scratch_shapes=scratch,
        compiler_params=_cparams(("parallel", "arbitrary")), name=name)(*args)


def _attn_kernel(*refs, n_chunks, tk, rb, slice_mode, diff_mode, lambda_init):
    q_ref, k_ref, v_ref = refs[:3]
    pos = 3
    if diff_mode:
        lam_ref, subln_ref = refs[pos], refs[pos + 1]
        pos += 2
    o_ref = refs[pos]
    per = [refs[pos + 1 + 9 * s:pos + 10 + 9 * s] for s in range(2)]
    qm_scr = [t[0] for t in per]
    s_scr = [t[1:3] for t in per]
    p_scr = [t[3:5] for t in per]
    m_scr = [t[5] for t in per]
    a_scr = [t[6:8] for t in per]
    acc_scr = [t[8] for t in per]
    tq = q_ref.shape[0]
    q = q_ref[...]
    lane = lax.broadcasted_iota(jnp.int32, (1, LANES), 1)
    for s in range(2):
        if slice_mode:
            qm_scr[s][...] = q[:, s * LANES:(s + 1) * LANES]
        else:
            qm_scr[s][...] = jnp.where((lane < LANES // 2) if s == 0 else (lane >= LANES // 2), q, jnp.zeros_like(q))
        m_scr[s][...] = jnp.full(m_scr[s].shape, MASK_VALUE, F32)
        acc_scr[s][...] = jnp.zeros(acc_scr[s].shape, F32)

    def chunk_rows(c):
        return pl.ds(c * tk if isinstance(c, int) else pl.multiple_of(c * tk, tk), tk)

    def scores(c, slot):
        k = k_ref[chunk_rows(c), :]
        for s in range(2):
            ks = k[:, s * LANES:(s + 1) * LANES] if slice_mode else k
            s_scr[s][slot][...] = lax.dot_general(qm_scr[s][...], ks, (((1,), (1,)), ((), ())),
                                                  preferred_element_type=F32)

    def softmax(slot):
        for s in range(2):
            for r in range(tq // rb):
                rows = slice(r * rb, (r + 1) * rb)
                sb = s_scr[s][slot][rows, :]
                m_old = m_scr[s][rows, :]
                m_new = jnp.maximum(m_old, jnp.max(sb, axis=-1, keepdims=True))
                m_scr[s][rows, :] = m_new
                a_scr[s][slot][rows, :] = jnp.exp2(m_old - m_new)
                p_scr[s][slot][rows, :] = jnp.exp2(sb - jnp.tile(m_new, (1, tk // LANES))).astype(BF16)

    def weighted_sum(c, slot):
        v = v_ref[chunk_rows(c), :]
        vext = jnp.concatenate([v, jnp.ones((tk, LANES), BF16)], axis=1)
        for s in range(2):
            alpha = a_scr[s][slot][...]
            pv = jnp.dot(p_scr[s][slot][...], vext, preferred_element_type=F32)
            acc_scr[s][...] = jnp.concatenate([alpha, alpha], axis=1) * acc_scr[s][...] + pv

    def step(c, slot):
        scores(c, slot)
        softmax(1 - slot)
        weighted_sum(c - 2, slot)

    scores(0, 0)
    if n_chunks > 1:
        scores(1, 1)
    softmax(0)
    if n_chunks > 2:
        @pl.loop(0, (n_chunks - 2) // 2)
        def _(j):
            step(2 * j + 2, 0)
            step(2 * j + 3, 1)
        if n_chunks % 2:
            step(n_chunks - 1, 0)
    if n_chunks > 1:
        softmax((n_chunks - 1) % 2)
        weighted_sum(n_chunks - 2, n_chunks % 2)
    weighted_sum(n_chunks - 1, (n_chunks - 1) % 2)
    o0 = acc_scr[0][:, :LANES] / acc_scr[0][:, LANES:]
    o1 = acc_scr[1][:, :LANES] / acc_scr[1][:, LANES:]
    if diff_mode:
        lp = lam_ref[...]
        lam = (jnp.exp(jnp.sum(lp[0:1] * lp[1:2], axis=-1, keepdims=True))
               - jnp.exp(jnp.sum(lp[2:3] * lp[3:4], axis=-1, keepdims=True)) + lambda_init)
        o = o0 - lam * o1
        ms = jnp.mean(o * o, axis=-1, keepdims=True)
        o = o * lax.rsqrt(ms + NORM_EPS) * subln_ref[...] * (1.0 - lambda_init)
    else:
        o = jnp.where(lane < LANES // 2, o0, o1)
    o_ref[...] = o.astype(o_ref.dtype)


def _attention(q_arr, q_blk, k_arr, k_blk, v_arr, v_blk, *, batch, n_q, n_k, groups, slice_mode=False, diff=None,
               tq=512, tk=768, rb=16, name="attn"):
    qw = 2 * LANES if slice_mode else LANES
    tq = min(tq, n_q)
    tk = min(tk, n_k)
    assert n_q % tq == 0 and n_k % tk == 0 and tk % LANES == 0 and tq % rb == 0
    nqt = n_q // tq
    in_specs = [pl.BlockSpec((tq, qw), lambda b, g, i: (b * nqt + i, q_blk + g)),
                pl.BlockSpec((n_k, qw), lambda b, g, i: (b, k_blk + g)),
                pl.BlockSpec((n_k, LANES), lambda b, g, i: (b, v_blk + g))]
    args = [q_arr, k_arr, v_arr]
    lambda_init = 0.0
    if diff is not None:
        lam_params, subln, lambda_init = diff
        in_specs.append(pl.BlockSpec((4, DA_HEAD_DIM), lambda b, g, i: (0, 0)))
        in_specs.append(pl.BlockSpec((1, LANES), lambda b, g, i: (0, 0)))
        args += [lam_params, subln.reshape(1, LANES)]
    kern = functools.partial(_attn_kernel, n_chunks=n_k // tk, tk=tk, rb=rb, slice_mode=slice_mode,
                             diff_mode=diff is not None, lambda_init=lambda_init)
    scratch = 2 * ([pltpu.VMEM((tq, LANES), BF16)] + 2 * [pltpu.VMEM((tq, tk), F32)] + 2 * [pltpu.VMEM((tq, tk), BF16)]
                   + [pltpu.VMEM((tq, LANES), F32)] + 2 * [pltpu.VMEM((tq, LANES), F32)]
                   + [pltpu.VMEM((tq, 2 * LANES), F32)])
    return pl.pallas_call(
        kern, grid=(batch, groups, nqt), in_specs=in_specs,
        out_specs=pl.BlockSpec((tq, LANES), lambda b, g, i: (b * nqt + i, g)),
        out_shape=jax.ShapeDtypeStruct((batch * n_q, groups * LANES), BF16), scratch_shapes=scratch,
        compiler_params=_cparams(("parallel", "parallel", "arbitrary")), name=name)(*args)


def _na_kernel(q_ref, kc_ref, vc_ref, kl_ref, vl_ref, bias_ref, o_ref, *, n_rows):
    i = pl.program_id(2)
    k0 = jnp.clip(i * NA_Q_ROWS - NA_WIN_ROWS // 2, 0, n_rows - NA_K_ROWS)
    start = pl.multiple_of(k0 * GRID_W, GRID_W * 4)
    nkw = NA_K_ROWS * GRID_W
    kw = kl_ref[pl.ds(start, nkw), :]
    vw = vl_ref[pl.ds(start, nkw), :]
    kc = kc_ref[...]
    vc = vc_ref[...]
    q = q_ref[...]
    lane = lax.broadcasted_iota(jnp.int32, (1, LANES), 1)
    outs = []
    for s in range(2):
        qs = jnp.where((lane < LANES // 2) if s == 0 else (lane >= LANES // 2), q, jnp.zeros_like(q))
        dn = (((1,), (1,)), ((), ()))
        s_win = lax.dot_general(qs, kw, dn, preferred_element_type=F32) + bias_ref[s]
        s_ctx = lax.dot_general(qs, kc, dn, preferred_element_type=F32)
        m = jnp.maximum(jnp.max(s_win, axis=-1, keepdims=True), jnp.max(s_ctx, axis=-1, keepdims=True))
        p_win = jnp.exp2(s_win - m)
        p_ctx = jnp.exp2(s_ctx - m)
        l = jnp.sum(p_win, axis=-1, keepdims=True) + jnp.sum(p_ctx, axis=-1, keepdims=True)
        o = (jnp.dot(p_win.astype(BF16), vw, preferred_element_type=F32)
             + jnp.dot(p_ctx.astype(BF16), vc, preferred_element_type=F32))
        outs.append(o / l)
    o_ref[...] = jnp.where(lane < LANES // 2, outs[0], outs[1]).astype(o_ref.dtype)


def _na_bias_tables(rpb, n_rows):
    nb = n_rows // NA_Q_ROWS
    assert nb >= 3
    dc = np.arange(GRID_W)[None, :] - np.arange(GRID_W)[:, None] + NA_WIN_COLS - 1
    onehot = (dc[None] == np.arange(2 * NA_WIN_COLS - 1)[:, None, None]).astype(np.float32)
    toep = jnp.einsum('hrd,dqk->hrqk', rpb.astype(F32), jnp.asarray(onehot), precision=lax.Precision.HIGHEST)
    cols = np.arange(GRID_W)
    col_start = np.clip(cols - NA_WIN_COLS // 2, 0, GRID_W - NA_WIN_COLS)
    col_ok = (cols[None, :] >= col_start[:, None]) & (cols[None, :] < col_start[:, None] + NA_WIN_COLS)
    ridx = np.zeros((3, NA_Q_ROWS, NA_K_ROWS), np.int32)
    ok = np.zeros((3, NA_Q_ROWS, NA_K_ROWS), bool)
    for var, blk in enumerate((0, 1, nb - 1)):
        r_first = blk * NA_Q_ROWS
        k_first = int(np.clip(r_first - NA_WIN_ROWS // 2, 0, n_rows - NA_K_ROWS))
        for a in range(NA_Q_ROWS):
            qr = r_first + a
            r0 = int(np.clip(qr - NA_WIN_ROWS // 2, 0, n_rows - NA_WIN_ROWS))
            for kb in range(NA_K_ROWS):
                kr = k_first + kb
                if r0 <= kr < r0 + NA_WIN_ROWS:
                    ok[var, a, kb] = True
                    ridx[var, a, kb] = kr - qr + NA_WIN_ROWS - 1
    tab = toep[:, ridx]
    tab = jnp.transpose(tab, (0, 1, 2, 4, 3, 5))
    mask = ok[:, :, None, :, None] & col_ok[None, None, :, None, :]
    tab = jnp.where(jnp.asarray(mask)[None], tab, MASK_VALUE)
    h = rpb.shape[0]
    return tab.reshape(h, 3, NA_Q_ROWS * GRID_W, NA_K_ROWS * GRID_W)


def _na_attention(qkv_lat, qkv_ctx, bias_tab, *, batch, n_lat, n_ctx):
    d = NA_HEADS * NA_HEAD_DIM
    groups = d // LANES
    n_rows = n_lat // GRID_W
    nb = n_rows // NA_Q_ROWS
    tq = NA_Q_ROWS * GRID_W

    def var_of(i):
        return jnp.where(i == 0, 0, jnp.where(i == nb - 1, 2, 1))

    in_specs = [
        pl.BlockSpec((tq, LANES), lambda b, g, i: (b * nb + i, g)),
        pl.BlockSpec((n_ctx, LANES), lambda b, g, i: (b, groups + g)),
        pl.BlockSpec((n_ctx, LANES), lambda b, g, i: (b, 2 * groups + g)),
        pl.BlockSpec((n_lat, LANES), lambda b, g, i: (b, groups + g)),
        pl.BlockSpec((n_lat, LANES), lambda b, g, i: (b, 2 * groups + g)),
        pl.BlockSpec((2, None, tq, NA_K_ROWS * GRID_W), lambda b, g, i: (g, var_of(i), 0, 0)),
    ]
    return pl.pallas_call(
        functools.partial(_na_kernel, n_rows=n_rows), grid=(batch, groups, nb), in_specs=in_specs,
        out_specs=pl.BlockSpec((tq, LANES), lambda b, g, i: (b * nb + i, g)),
        out_shape=jax.ShapeDtypeStruct((batch * n_lat, d), BF16),
        compiler_params=_cparams(("parallel", "parallel", "arbitrary")), name="na_attn",
    )(qkv_lat, qkv_ctx, qkv_ctx, qkv_lat, qkv_lat, bias_tab)


def _dft_kernel(c_ref, ms_ref, ab_ref, o_ref, acc_ref, *, scale):
    n = pl.program_id(2)
    d = o_ref.shape[1]
    part = (jnp.dot(c_ref[...], ab_ref[:, :d], preferred_element_type=F32)
            + jnp.dot(ms_ref[...], ab_ref[:, d:], preferred_element_type=F32))

    @pl.when(n == 0)
    def _():
        acc_ref[...] = part

    @pl.when(n > 0)
    def _():
        acc_ref[...] += part

    @pl.when(n == pl.num_programs(2) - 1)
    def _():
        o_ref[...] = (acc_ref[...] * scale).astype(o_ref.dtype)


def _pos_dft(cos_t, msin_t, ab, *, batch, n_tok, d, scale, name):
    t = min(1024, n_tok)
    nt = n_tok // t
    return pl.pallas_call(
        functools.partial(_dft_kernel, scale=scale), grid=(batch, nt, nt),
        in_specs=[pl.BlockSpec((t, t), lambda b, k, n: (k, n)),
                  pl.BlockSpec((t, t), lambda b, k, n: (k, n)),
                  pl.BlockSpec((t, 2 * d), lambda b, k, n: (b * nt + n, 0))],
        out_specs=pl.BlockSpec((t, d), lambda b, k, n: (b * nt + k, 0)),
        out_shape=jax.ShapeDtypeStruct((batch * n_tok, d), BF16),
        scratch_shapes=[pltpu.VMEM((t, d), F32)],
        compiler_params=_cparams(("parallel", "parallel", "arbitrary")), name=name)(cos_t, msin_t, ab)


def _dft_tables(n):
    split = 64 if n % 64 == 0 and n > 64 else 1
    m = jnp.arange(n, dtype=jnp.int32)[None, :]

    def phase(kvals):
        ang = ((kvals[:, None] * m) % n).astype(F32) * (2.0 * math.pi / n)
        return jnp.cos(ang), jnp.sin(ang)

    c1, s1 = phase(jnp.arange(n // split, dtype=jnp.int32) * split)
    c2, s2 = phase(jnp.arange(split, dtype=jnp.int32))
    c = c1[:, None, :] * c2[None, :, :] - s1[:, None, :] * s2[None, :, :]
    s = s1[:, None, :] * c2[None, :, :] + c1[:, None, :] * s2[None, :, :]
    return c.reshape(n, n).astype(BF16), (-s).reshape(n, n).astype(BF16)


def _channel_dft_weight(d):
    cg = d // FN_GROUPS
    idx = np.arange(cg)
    ang = 2.0 * np.pi * ((idx[:, None] * idx[None, :]) % cg) / cg
    eye = np.eye(FN_GROUPS)
    return jnp.asarray(np.concatenate([np.kron(eye, np.cos(ang)), np.kron(eye, np.sin(ang))], axis=1), F32)


def _router_kernel(x_ref, gain_ref, sc_ref, sh_ref, wr_ref, h_ref, aff_ref):
    x = x_ref[...]
    ms = jnp.mean(x * x, axis=-1, keepdims=True)
    h = x * lax.rsqrt(ms + NORM_EPS) * gain_ref[...]
    h = h * (1.0 + sc_ref[...]) + sh_ref[...]
    h_ref[...] = h.astype(h_ref.dtype)
    logits = jnp.dot(h, wr_ref[...], preferred_element_type=F32, precision=lax.Precision.HIGHEST)
    e = jnp.exp(logits - jnp.max(logits, axis=-1, keepdims=True))
    aff_ref[...] = e / jnp.sum(e, axis=-1, keepdims=True)


def _norm_router(x, gain, sc, sh, w_router, layer, *, rows_per_batch, tm=512):
    m, d = x.shape
    e = w_router.shape[-1]
    tm = min(tm, rows_per_batch)
    tpb = rows_per_batch // tm
    return pl.pallas_call(
        _router_kernel, grid=(m // tm,),
        in_specs=[pl.BlockSpec((tm, d), lambda i: (i, 0)),
                  pl.BlockSpec((1, d), lambda i: (0, 0)),
                  pl.BlockSpec((None, 1, d), lambda i: (i // tpb, 0, 0)),
                  pl.BlockSpec((None, 1, d), lambda i: (i // tpb, 0, 0)),
                  pl.BlockSpec((None, d, e), lambda i: (layer, 0, 0))],
        out_specs=[pl.BlockSpec((tm, d), lambda i: (i, 0)), pl.BlockSpec((tm, e), lambda i: (i, 0))],
        out_shape=[jax.ShapeDtypeStruct((m, d), BF16), jax.ShapeDtypeStruct((m, e), F32)],
        compiler_params=_cparams(("parallel",)), name="norm_router")(x, gain.reshape(1, d), sc, sh, w_router)


def _ffn_kernel(*refs, n_streams):
    x_refs = refs[:n_streams]
    wg_ref, wu_ref, wd_ref = refs[n_streams:n_streams + 3]
    gate_refs = refs[n_streams + 3:2 * n_streams + 3]
    o_refs = refs[2 * n_streams + 3:3 * n_streams + 3]
    acc_ref = refs[3 * n_streams + 3]
    f = pl.program_id(2)
    x = x_refs[0][...] if n_streams == 1 else jnp.concatenate([r[...] for r in x_refs], axis=0)
    a = jnp.dot(x, wg_ref[...].astype(BF16), preferred_element_type=F32)
    u = jnp.dot(x, wu_ref[...].astype(BF16), preferred_element_type=F32)
    act = (a * (1.0 / (1.0 + jnp.exp(-a)))) * u
    y = jnp.dot(act.astype(BF16), wd_ref[...].astype(BF16), preferred_element_type=F32)

    @pl.when(f == 0)
    def _():
        acc_ref[...] = y

    @pl.when(f > 0)
    def _():
        acc_ref[...] += y

    @pl.when(f == pl.num_programs(2) - 1)
    def _():
        row = 0
        for o_ref, g_ref in zip(o_refs, gate_refs):
            c = o_ref.shape[0]
            o_ref[...] = acc_ref[row:row + c, :] * g_ref[...]
            row += c


def _moe_ffn(xs, gates, w_gate, w_up, w_down, layer, *, tf=512):
    b, e, _, d = xs[0].shape
    f = w_gate.shape[-1]
    tf = min(tf, f)
    slot = lambda c, w: pl.BlockSpec((None, None, c, w), lambda ei, bi, fi: (bi, ei, 0, 0))
    caps = [t.shape[2] for t in xs]
    in_specs = ([slot(c, d) for c in caps]
                + [pl.BlockSpec((None, None, d, tf), lambda ei, bi, fi: (layer, ei, 0, fi)),
                   pl.BlockSpec((None, None, d, tf), lambda ei, bi, fi: (layer, ei, 0, fi)),
                   pl.BlockSpec((None, None, tf, d), lambda ei, bi, fi: (layer, ei, fi, 0))]
                + [slot(c, 1) for c in caps])
    return pl.pallas_call(
        functools.partial(_ffn_kernel, n_streams=len(xs)), grid=(e, b, f // tf), in_specs=in_specs,
        out_specs=[slot(c, d) for c in caps],
        out_shape=[jax.ShapeDtypeStruct((b, e, c, d), F32) for c in caps],
        scratch_shapes=[pltpu.VMEM((sum(caps), d), F32)],
        compiler_params=_cparams(("parallel", "parallel", "arbitrary")), name="moe_ffn",
    )(*xs, w_gate, w_up, w_down, *gates)


def _route(aff, h, batch, n_tok):
    e = aff.shape[-1]
    cap = EC_CAPACITY_FACTOR * n_tok // e
    gate, idx = lax.top_k(jnp.swapaxes(aff.reshape(batch, n_tok, e), 1, 2), cap)
    idx, gate = lax.sort((idx, gate), dimension=2, num_keys=1)
    xs = jax.vmap(lambda hb, ib: hb[ib])(h.reshape(batch, n_tok, -1), idx)
    return gate, idx, xs


COMBINE_TOKENS = 512
COMBINE_WINDOW = 128


def _combine_kernel(lo_ref, nr_ref, x_ref, g_ref, pos_ref, y_hbm, o_ref, stage, sem, acc, *, n_tiles, n_exp, cap):
    b = pl.program_id(0)
    t = pl.program_id(1)
    w = COMBINE_WINDOW
    tile = b * n_tiles + t
    lane = lax.broadcasted_iota(jnp.int32, (1, w), 1)
    acc[...] = jnp.zeros(acc.shape, F32)

    def window(e, r):
        first = lo_ref[tile * n_exp + e] + r * w
        return first, pl.multiple_of(jnp.minimum(first, cap - w), 8)

    def copy(e, start):
        return pltpu.make_async_copy(y_hbm.at[b, e, pl.ds(start, w), :], stage.at[pl.ds(e * w, w), :], sem.at[e])

    @pl.loop(0, nr_ref[tile])
    def _(r):
        for e in range(n_exp):
            copy(e, window(e, r)[1]).start()
        pos = pos_ref[...]
        blocks = []
        for e in range(n_exp):
            first, start = window(e, r)
            col = pos[:, e:e + 1]
            hit = jnp.logical_and(col - start == lane, col >= first)
            blocks.append(jnp.where(hit, 1.0, 0.0).astype(BF16))
        onehot = jnp.concatenate(blocks, axis=1)
        for e in range(n_exp):
            copy(e, window(e, r)[1]).wait()
        y = stage[...]
        y_hi = y.astype(BF16)
        y_lo = (y - y_hi.astype(F32)).astype(BF16)
        acc[...] += (jnp.dot(onehot, y_hi, preferred_element_type=F32)
                     + jnp.dot(onehot, y_lo, preferred_element_type=F32))

    o_ref[...] = x_ref[...] + g_ref[...] * acc[...]


def _combine_lat(x, g2, y, idx, batch, n_tok):
    d = x.shape[-1]
    _, n_exp, cap = idx.shape
    tt, w = COMBINE_TOKENS, COMBINE_WINDOW
    assert n_tok % tt == 0 and cap % 8 == 0 and cap >= w
    nt = n_tok // tt
    bounds = jnp.arange(nt + 1, dtype=jnp.int32) * tt
    below = jnp.sum((idx[..., None] < bounds).astype(jnp.int32), axis=2)
    lo = (below[..., :-1] // 8) * 8
    rounds = jnp.max((below[..., 1:] - lo + w - 1) // w, axis=1)
    rounds = jnp.where(jnp.max(below[..., 1:] - below[..., :-1], axis=1) > 0, rounds, 0)
    lo_flat = jnp.transpose(lo, (0, 2, 1)).reshape(-1)
    bi = jnp.arange(batch)[:, None, None]
    ei = jnp.arange(n_exp)[None, :, None]
    pos = jnp.full((batch, n_tok, n_exp), -1, jnp.int32).at[bi, idx, ei].set(
        jnp.broadcast_to(jnp.arange(cap, dtype=jnp.int32), idx.shape))
    grid_spec = pltpu.PrefetchScalarGridSpec(
        num_scalar_prefetch=2, grid=(batch, nt),
        in_specs=[pl.BlockSpec((tt, d), lambda b, t, lo_r, nr_r: (b * nt + t, 0)),
                  pl.BlockSpec((None, 1, d), lambda b, t, lo_r, nr_r: (b, 0, 0)),
                  pl.BlockSpec((tt, n_exp), lambda b, t, lo_r, nr_r: (b * nt + t, 0)),
                  pl.BlockSpec(memory_space=pl.ANY)],
        out_specs=pl.BlockSpec((tt, d), lambda b, t, lo_r, nr_r: (b * nt + t, 0)),
        scratch_shapes=[pltpu.VMEM((n_exp * w, d), F32), pltpu.SemaphoreType.DMA((n_exp,)), pltpu.VMEM((tt, d), F32)])
    return pl.pallas_call(
        functools.partial(_combine_kernel, n_tiles=nt, n_exp=n_exp, cap=cap), grid_spec=grid_spec,
        out_shape=jax.ShapeDtypeStruct(x.shape, F32),
        compiler_params=_cparams(("parallel", "arbitrary")), name="moe_combine",
    )(lo_flat, rounds.reshape(-1), x, g2, pos.reshape(batch * n_tok, n_exp), y)


def _combine(x, g2, y, idx, batch, n_tok):
    d = x.shape[-1]
    upd = jax.vmap(lambda ib, yb: jnp.zeros((n_tok, d), F32).at[ib.reshape(-1)].add(yb.reshape(-1, d)))(idx, y)
    return (x.reshape(batch, n_tok, d) + g2 * upd).reshape(batch * n_tok, d)


def _final_norm_kernel(x_ref, g_ref, o_ref):
    x = x_ref[...]
    ms = jnp.mean(x * x, axis=-1, keepdims=True)
    o_ref[...] = x * lax.rsqrt(ms + NORM_EPS) * g_ref[...]


def _final_norm(x, g, tm=1024):
    m, d = x.shape
    tm = min(tm, m)
    return pl.pallas_call(
        _final_norm_kernel, grid=(m // tm,),
        in_specs=[pl.BlockSpec((tm, d), lambda i: (i, 0)), pl.BlockSpec((1, d), lambda i: (0, 0))],
        out_specs=pl.BlockSpec((tm, d), lambda i: (i, 0)), out_shape=jax.ShapeDtypeStruct((m, d), F32),
        compiler_params=_cparams(("parallel",)), name="final_norm")(x, g.reshape(1, d))


def _rope_tables(n_tok, rot_dim, lane_offset, group):
    t = jnp.arange(n_tok)
    rows = (t // GRID_W).astype(F32)
    cols = (t % GRID_W).astype(F32)
    n_freq = rot_dim // 4
    inv_freq = ROPE_BASE ** (-jnp.arange(n_freq, dtype=F32) / n_freq)
    ang = jnp.concatenate([rows[:, None] * inv_freq, cols[:, None] * inv_freq], axis=-1)
    cos, sin = jnp.cos(ang), jnp.sin(ang)
    half = rot_dim // 2
    lane = np.arange(LANES) % group - lane_offset
    first = (lane >= 0) & (lane < half)
    second = (lane >= half) & (lane < rot_dim)
    pair = np.where(first, lane, np.where(second, lane - half, 0))
    cos_l = jnp.where(jnp.asarray(first | second)[None], cos[:, pair], 1.0)
    sin_l = sin[:, pair]
    sin_next = jnp.where(jnp.asarray(first)[None], -sin_l, 0.0)
    sin_prev = jnp.where(jnp.asarray(second)[None], sin_l, 0.0)
    return cos_l, sin_next, sin_prev


def kernel(x, c, ctx, c_ctx, ada_w, ada_b, norm_mix, norm_ffn, norm_final, da_w_qkv, da_w_o, da_lambda_q1,
           da_lambda_k1, da_lambda_q2, da_lambda_k2, da_subln, fn_w_o, na_w_qkv, na_w_o, na_rpb, mla_w_dq,
           mla_q_norm, mla_w_uq, mla_w_dkv, mla_kv_norm, mla_w_uk, mla_w_uv, mla_w_o, moe_w_router, moe_w_gate,
           moe_w_up, moe_w_down):
    batch, n_lat, d = x.shape
    n_ctx = ctx.shape[1]
    depth = ada_w.shape[0]
    n_mixers = 4
    xl = x.reshape(batch * n_lat, d)
    xc = ctx.reshape(batch * n_ctx, d)

    cond = jnp.concatenate([c, c_ctx[None], jnp.zeros((8 - batch - 1, d), F32)], axis=0)
    mods = [
        _fused_mm(cond, ada_w, w_index=i, bias=ada_b[i], silu_in=True, precision=lax.Precision.HIGHEST, tn=1024,
                  name="ada_mod")
        for i in range(depth)
    ]

    def chunks(i):
        lat = [mods[i][:batch, k * d:(k + 1) * d].reshape(batch, 1, d) for k in range(ADA_CHUNKS)]
        cx = [mods[i][batch:batch + 1, k * d:(k + 1) * d].reshape(1, 1, d) for k in range(ADA_CHUNKS)]
        return lat, cx

    def all_keys(t_ctx, t_lat):
        w = t_ctx.shape[-1]
        both = jnp.concatenate([t_ctx.reshape(batch, n_ctx, w), t_lat.reshape(batch, n_lat, w)], axis=1)
        return both.reshape(batch * (n_ctx + n_lat), w)

    for i in range(depth):
        kind, j = i % n_mixers, i // n_mixers
        keep_ctx = i < depth - 1
        (sh1, sc1, g1, sh2, sc2, g2), (csh1, csc1, cg1, csh2, csc2, cg2) = chunks(i)
        lat_in = dict(gain=norm_mix[i], mod=(sc1, sh1), rows_per_batch=n_lat)
        ctx_in = dict(gain=norm_mix[i], mod=(csc1, csh1), rows_per_batch=batch * n_ctx, tm=n_ctx)

        if kind == 0:
            lambda_init = 0.8 - 0.6 * math.exp(-0.3 * i)
            scale = jnp.concatenate([jnp.full((d,), LOG2E * DA_HEAD_DIM ** -0.5, F32), jnp.ones((2 * d,), F32)])
            rope = _rope_tables(n_lat, DA_HEAD_DIM, 0, DA_HEAD_DIM)
            qkv_l = _fused_mm(xl, da_w_qkv, w_index=j, colscale=scale, rope=rope, rope_cols=2 * d,
                              rope_shift=DA_HEAD_DIM // 2, out_dtype=BF16, name="da_qkv_lat", **lat_in)
            qkv_c = _fused_mm(xc, da_w_qkv, w_index=j, colscale=scale, out_dtype=BF16, name="da_qkv_ctx", **ctx_in)
            groups = d // LANES
            lam_params = jnp.stack([da_lambda_q1[j], da_lambda_k1[j], da_lambda_q2[j], da_lambda_k2[j]])
            diff = (lam_params, da_subln[j], lambda_init)
            kv = all_keys(qkv_c[:, d:], qkv_l[:, d:])
            o_l = _attention(qkv_l, 0, kv, 0, kv, groups, batch=batch, n_q=n_lat, n_k=n_ctx + n_lat, groups=groups,
                             diff=diff, name="da_attn_lat")
            o_c = _attention(qkv_c, 0, qkv_c, groups, qkv_c, 2 * groups, batch=batch, n_q=n_ctx, n_k=n_ctx,
                             groups=groups, diff=diff, name="da_attn_ctx") if keep_ctx else None
            w_o = da_w_o
        elif kind == 1:
            w_cd = _channel_dft_weight(d)
            ab_l = _fused_mm(xl, w_cd, out_dtype=BF16, name="fn_chan_lat", **lat_in)
            scale_l = 1.0 / math.sqrt(n_lat * (d // FN_GROUPS))
            o_l = _pos_dft(*_dft_tables(n_lat), ab_l, batch=batch, n_tok=n_lat, d=d, scale=scale_l, name="fn_pos_lat")
            o_c = None
            if keep_ctx:
                ab_c = _fused_mm(xc, w_cd, out_dtype=BF16, name="fn_chan_ctx", **ctx_in)
                scale_c = 1.0 / math.sqrt(n_ctx * (d // FN_GROUPS))
                o_c = _pos_dft(*_dft_tables(n_ctx), ab_c, batch=batch, n_tok=n_ctx, d=d, scale=scale_c,
                               name="fn_pos_ctx")
            w_o = fn_w_o
        elif kind == 2:
            scale = jnp.concatenate([jnp.full((d,), LOG2E * NA_HEAD_DIM ** -0.5, F32), jnp.ones((2 * d,), F32)])
            qkv_l = _fused_mm(xl, na_w_qkv, w_index=j, colscale=scale, out_dtype=BF16, name="na_qkv_lat", **lat_in)
            qkv_c = _fused_mm(xc, na_w_qkv, w_index=j, colscale=scale, out_dtype=BF16, name="na_qkv_ctx", **ctx_in)
            groups = d // LANES
            bias_tab = _na_bias_tables(na_rpb[j] * LOG2E, n_lat // GRID_W)
            o_l = _na_attention(qkv_l, qkv_c, bias_tab, batch=batch, n_lat=n_lat, n_ctx=n_ctx)
            o_c = _attention(qkv_c, 0, qkv_c, groups, qkv_c, 2 * groups, batch=batch, n_q=n_ctx, n_k=n_ctx,
                             groups=groups, name="na_attn_ctx") if keep_ctx else None
            w_o = na_w_o
        else:
            q_rank = mla_w_dq.shape[-1]
            kv_rank = mla_w_uk.shape[-2]
            hq = MLA_NOPE_DIM + MLA_ROPE_DIM
            down_w = 3 * 384
            assert q_rank == 768 and kv_rank == 256
            w_down = jnp.concatenate([mla_w_dq[j], mla_w_dkv[j], jnp.zeros((d, down_w - q_rank - kv_rank - MLA_ROPE_DIM), F32)],
                                     axis=1)
            wq = mla_w_uq[j].reshape(q_rank, MLA_HEADS, hq)
            wq = jnp.concatenate([wq, jnp.zeros((q_rank, MLA_HEADS, LANES - hq), F32)], axis=-1).reshape(q_rank, MLA_HEADS * LANES)
            wk_nope = jnp.concatenate([mla_w_uk[j].reshape(kv_rank, MLA_HEADS, MLA_NOPE_DIM),
                                       jnp.zeros((kv_rank, MLA_HEADS, LANES - MLA_NOPE_DIM), F32)], axis=-1)
            eye = jnp.eye(MLA_ROPE_DIM, dtype=F32)[:, None, :]
            wk_rope = jnp.concatenate([jnp.zeros((MLA_ROPE_DIM, MLA_HEADS, MLA_NOPE_DIM), F32),
                                       jnp.broadcast_to(eye, (MLA_ROPE_DIM, MLA_HEADS, MLA_ROPE_DIM)),
                                       jnp.zeros((MLA_ROPE_DIM, MLA_HEADS, LANES - hq), F32)], axis=-1)
            kw = MLA_HEADS * LANES
            vw = MLA_HEADS * MLA_V_DIM
            pad_rows = 384 - kv_rank - MLA_ROPE_DIM
            w_kv = jnp.concatenate([
                jnp.concatenate([wk_nope.reshape(kv_rank, kw), mla_w_uv[j]], axis=1),
                jnp.concatenate([wk_rope.reshape(MLA_ROPE_DIM, kw), jnp.zeros((MLA_ROPE_DIM, vw), F32)], axis=1),
                jnp.zeros((pad_rows, kw + vw), F32)], axis=0)
            kv_gain = jnp.concatenate([mla_kv_norm[j], jnp.ones((384 - kv_rank,), F32)])
            rope = _rope_tables(n_lat, MLA_ROPE_DIM, MLA_NOPE_DIM, LANES)
            qscale = jnp.full((kw,), LOG2E * hq ** -0.5, F32)

            def project(xs, inp, n_tok, rope_tabs, tag):
                low = _fused_mm(xs, w_down, tn=384, name="mla_down_" + tag, **inp)
                rp = dict(rope=rope_tabs, rope_shift=MLA_ROPE_DIM // 2) if rope_tabs is not None else {}
                tmx = inp.get("tm", 1024)
                q = _fused_mm(low, wq, a_cols=(0, q_rank), gain=mla_q_norm[j], colscale=qscale, rope_cols=kw,
                              rows_per_batch=inp["rows_per_batch"], tm=tmx, out_dtype=BF16, name="mla_q_" + tag, **rp)
                kv = _fused_mm(low, w_kv, a_cols=(q_rank, 384), gain=kv_gain, norm_cols=kv_rank, rope_cols=kw,
                               rows_per_batch=inp["rows_per_batch"], tm=tmx, out_dtype=BF16, name="mla_kv_" + tag, **rp)
                return q, kv

            q_l, kv_l = project(xl, lat_in, n_lat, rope, "lat")
            q_c, kv_c = project(xc, ctx_in, n_ctx, None, "ctx")
            groups = MLA_HEADS // 2
            kv = all_keys(kv_c, kv_l)
            o_l = _attention(q_l, 0, kv, 0, kv, 2 * groups, batch=batch, n_q=n_lat, n_k=n_ctx + n_lat, groups=groups,
                             slice_mode=True, name="mla_attn_lat")
            o_c = _attention(q_c, 0, kv_c, 0, kv_c, 2 * groups, batch=batch, n_q=n_ctx, n_k=n_ctx, groups=groups,
                             slice_mode=True, name="mla_attn_ctx") if keep_ctx else None
            w_o = mla_w_o

        xl = _fused_mm(o_l, w_o, w_index=j, res=xl, gate=g1, rows_per_batch=n_lat, name="mix_out_lat")
        if keep_ctx:
            xc = _fused_mm(o_c, w_o, w_index=j, res=xc, gate=jnp.broadcast_to(cg1, (1, 1, d)),
                           rows_per_batch=batch * n_ctx, tm=n_ctx, name="mix_out_ctx")

        h_l, aff_l = _norm_router(xl, norm_ffn[i], sc2, sh2, moe_w_router, i, rows_per_batch=n_lat)
        gate_l, idx_l, xs_l = _route(aff_l, h_l, batch, n_lat)
        xs, gates = [xs_l], [gate_l[..., None]]
        if keep_ctx:
            h_c, aff_c = _norm_router(xc, norm_ffn[i], csc2, csh2, moe_w_router, i, rows_per_batch=batch * n_ctx,
                                      tm=n_ctx)
            gate_c, idx_c, xs_c = _route(aff_c, h_c, batch, n_ctx)
            xs.append(xs_c)
            gates.append(gate_c[..., None])
        ys = _moe_ffn(xs, gates, moe_w_gate, moe_w_up, moe_w_down, i)
        xl = _combine_lat(xl, g2, ys[0], idx_l, batch, n_lat)
        if keep_ctx:
            xc = _combine(xc, cg2, ys[1], idx_c, batch, n_ctx)

    return _final_norm(xl, norm_final).reshape(batch, n_lat, d)
```

```python
import functools
import math

import numpy as np
import jax
import jax.numpy as jnp
from jax import lax
from jax.experimental import pallas as pl
from jax.experimental.pallas import tpu as pltpu

F32 = jnp.float32
BF16 = jnp.bfloat16

LANES = 128
VMEM_LIMIT_BYTES = 56 * 1024 * 1024

GRID_W = 64
ROPE_BASE = 10000.0
NORM_EPS = 1e-6
ADA_CHUNKS = 6
DA_HEAD_DIM = 64
NA_HEADS = 16
NA_HEAD_DIM = 64
NA_WIN_ROWS = 8
NA_WIN_COLS = 16
NA_Q_ROWS = 8
NA_K_ROWS = 16
MLA_HEADS = 16
MLA_NOPE_DIM = 64
MLA_ROPE_DIM = 32
MLA_V_DIM = 64
FN_GROUPS = 4
EC_CAPACITY_FACTOR = 2
MASK_VALUE = -1e30
LOG2E = math.log2(math.e)


def _cparams(sem):
    return pltpu.CompilerParams(dimension_semantics=sem, vmem_limit_bytes=VMEM_LIMIT_BYTES)


def _rope_lanes(acc, cos, sin):
    outs = []
    for g in range(acc.shape[1] // LANES):
        blk = acc[:, g * LANES:(g + 1) * LANES]
        outs.append(blk * cos + pltpu.roll(blk, LANES // 2, 1) * sin)
    return outs[0] if len(outs) == 1 else jnp.concatenate(outs, axis=1)


def _mm_kernel(*refs, norm_cols, has_gain, has_mod, has_colscale, rope_tiles, has_bias, has_res,
               silu_in, precision):
    it = iter(refs)
    a_ref = next(it)
    w_ref = next(it)
    gain_ref = next(it) if has_gain else None
    sc_ref, sh_ref = (next(it), next(it)) if has_mod else (None, None)
    cs_ref = next(it) if has_colscale else None
    cos_ref, sin_ref = (next(it), next(it)) if rope_tiles else (None, None)
    bias_ref = next(it) if has_bias else None
    res_ref, gate_ref = (next(it), next(it)) if has_res else (None, None)
    o_ref = next(it)
    h_scr = next(it, None)
    j = pl.program_id(1)
    hi = precision is not None

    def prologue():
        a = a_ref[...].astype(F32)
        if silu_in:
            a = a * (1.0 / (1.0 + jnp.exp(-a)))
        if has_gain:
            k = a.shape[1]
            if norm_cols < k:
                col = lax.broadcasted_iota(jnp.int32, (1, k), 1)
                sq = jnp.where(col < norm_cols, a * a, 0.0)
            else:
                sq = a * a
            ms = jnp.sum(sq, axis=-1, keepdims=True) * (1.0 / norm_cols)
            y = a * lax.rsqrt(ms + NORM_EPS) * gain_ref[...]
            a = jnp.where(col < norm_cols, y, a) if norm_cols < k else y
        if has_mod:
            a = a * (1.0 + sc_ref[...]) + sh_ref[...]
        return a if hi else a.astype(BF16)

    if h_scr is not None:
        @pl.when(j == 0)
        def _():
            h_scr[...] = prologue()
        h = h_scr[...]
    else:
        h = a_ref[...] if hi else a_ref[...].astype(BF16)
    w = w_ref[...] if hi else w_ref[...].astype(BF16)
    acc = jnp.dot(h, w, preferred_element_type=F32, precision=precision)
    if has_colscale:
        acc = acc * cs_ref[...]

    def finish(v):
        if has_bias:
            v = v + bias_ref[...]
        if has_res:
            v = res_ref[...] + gate_ref[...] * v
        o_ref[...] = v.astype(o_ref.dtype)

    if rope_tiles:
        @pl.when(j < rope_tiles)
        def _():
            finish(_rope_lanes(acc, cos_ref[...], sin_ref[...]))

        @pl.when(j >= rope_tiles)
        def _():
            finish(acc)
    else:
        finish(acc)


def _fused_mm(a, w, *, w_index=None, a_cols=None, gain=None, norm_cols=None, mod=None, colscale=None, rope=None,
              rope_cols=0, bias=None, res=None, gate=None, silu_in=False, rows_per_batch=None,
              out_dtype=F32, tm=1024, tn=1024, precision=None, name="mm"):
    m = a.shape[0]
    k0, k = a_cols if a_cols is not None else (0, a.shape[1])
    if precision is None and w.dtype != BF16:
        w = (w[w_index] if w.ndim == 3 else w).astype(BF16)
    n = w.shape[-1]
    assert w.shape[-2] == k and k0 % k == 0
    tm = min(tm, m)
    tn = min(tn, n)
    assert m % tm == 0 and n % tn == 0
    rows_per_batch = rows_per_batch or m
    assert rows_per_batch % tm == 0
    tpb = rows_per_batch // tm
    has_prologue = gain is not None or mod is not None or silu_in
    in_specs = [pl.BlockSpec((tm, k), lambda i, j: (i, k0 // k))]
    args = [a]
    if w.ndim == 3:
        in_specs.append(pl.BlockSpec((None, k, tn), lambda i, j: (w_index, 0, j)))
    else:
        in_specs.append(pl.BlockSpec((k, tn), lambda i, j: (0, j)))
    args.append(w)
    if gain is not None:
        in_specs.append(pl.BlockSpec((1, k), lambda i, j: (0, 0)))
        args.append(gain.reshape(1, k).astype(F32))
    if mod is not None:
        for t in mod:
            in_specs.append(pl.BlockSpec((None, 1, k), lambda i, j: (i // tpb, 0, 0)))
            args.append(t)
    if colscale is not None:
        in_specs.append(pl.BlockSpec((1, tn), lambda i, j: (0, j)))
        args.append(colscale.reshape(1, n).astype(F32))
    rope_tiles = 0
    if rope is not None:
        assert rope_cols % tn == 0
        rope_tiles = rope_cols // tn
        for t in rope:
            in_specs.append(pl.BlockSpec((tm, LANES), lambda i, j: (i % tpb, 0)))
            args.append(t)
    if bias is not None:
        in_specs.append(pl.BlockSpec((1, tn), lambda i, j: (0, j)))
        args.append(bias.reshape(1, n).astype(F32))
    if res is not None:
        in_specs.append(pl.BlockSpec((tm, tn), lambda i, j: (i, j)))
        args.append(res)
        in_specs.append(pl.BlockSpec((None, 1, tn), lambda i, j: (i // tpb, 0, j)))
        args.append(gate)
    scratch = [pltpu.VMEM((tm, k), F32 if precision is not None else BF16)] if has_prologue else []
    kern = functools.partial(
        _mm_kernel, norm_cols=norm_cols or k, has_gain=gain is not None, has_mod=mod is not None,
        has_colscale=colscale is not None, rope_tiles=rope_tiles, has_bias=bias is not None,
        has_res=res is not None, silu_in=silu_in, precision=precision)
    return pl.pallas_call(
        kern, grid=(m // tm, n // tn), in_specs=in_specs,
        out_specs=pl.BlockSpec((tm, tn), lambda i, j: (i, j)),
        out_shape=jax.ShapeDtypeStruct((m, n), out_dtype), scratch_shapes=scratch,
        compiler_params=_cparams(("parallel", "arbitrary")), name=name)(*args)


def _attn_kernel(*refs, n_chunks, tk, rb, unroll, slice_mode, sub_width, diff_mode, lambda_init):
    q_ref, k_ref, v_ref = refs[:3]
    pos = 3
    if diff_mode:
        lam_ref, subln_ref = refs[pos], refs[pos + 1]
        pos += 2
    o_ref = refs[pos]
    per = [refs[pos + 1 + 9 * s:pos + 10 + 9 * s] for s in range(2)]
    qm_scr = [t[0] for t in per]
    s_scr = [t[1:3] for t in per]
    p_scr = [t[3:5] for t in per]
    m_scr = [t[5] for t in per]
    a_scr = [t[6:8] for t in per]
    acc_scr = [t[8] for t in per]
    tq = q_ref.shape[0]
    q = q_ref[...]
    lane = lax.broadcasted_iota(jnp.int32, (1, LANES), 1)
    for s in range(2):
        if slice_mode:
            qm_scr[s][...] = q[:, s * LANES:(s + 1) * LANES]
        else:
            qm_scr[s][...] = jnp.where((lane // sub_width) % 2 == s, q, jnp.zeros_like(q))
        m_scr[s][...] = jnp.full(m_scr[s].shape, MASK_VALUE, F32)
        acc_scr[s][...] = jnp.zeros(acc_scr[s].shape, F32)

    def chunk_rows(c):
        return pl.ds(c * tk if isinstance(c, int) else pl.multiple_of(c * tk, tk), tk)

    def scores(c, slot):
        k = k_ref[chunk_rows(c), :]
        for s in range(2):
            ks = k[:, s * LANES:(s + 1) * LANES] if slice_mode else k
            s_scr[s][slot][...] = lax.dot_general(qm_scr[s][...], ks, (((1,), (1,)), ((), ())),
                                                  preferred_element_type=F32)

    def softmax(slot):
        for s in range(2):
            for r in range(tq // rb):
                rows = slice(r * rb, (r + 1) * rb)
                sb = s_scr[s][slot][rows, :]
                m_old = m_scr[s][rows, :]
                m_new = jnp.maximum(m_old, jnp.max(sb, axis=-1, keepdims=True))
                m_scr[s][rows, :] = m_new
                a_scr[s][slot][rows, :] = jnp.exp2(m_old - m_new)
                p_scr[s][slot][rows, :] = jnp.exp2(sb - jnp.tile(m_new, (1, tk // LANES))).astype(BF16)

    def weighted_sum(c, slot):
        v = v_ref[chunk_rows(c), :]
        vext = jnp.concatenate([v, jnp.ones((tk, LANES), BF16)], axis=1)
        for s in range(2):
            alpha = a_scr[s][slot][...]
            pv = jnp.dot(p_scr[s][slot][...], vext, preferred_element_type=F32)
            acc_scr[s][...] = jnp.concatenate([alpha, alpha], axis=1) * acc_scr[s][...] + pv

    def step(c, slot):
        scores(c, slot)
        softmax(1 - slot)
        weighted_sum(c - 2, slot)

    scores(0, 0)
    if n_chunks > 1:
        scores(1, 1)
    softmax(0)
    if n_chunks > 2:
        n_loop = (n_chunks - 2) // unroll
        if n_loop:
            @pl.loop(0, n_loop)
            def _(j):
                for u in range(unroll):
                    step(unroll * j + 2 + u, u % 2)
        for c in range(2 + n_loop * unroll, n_chunks):
            step(c, c % 2)
    if n_chunks > 1:
        softmax((n_chunks - 1) % 2)
        weighted_sum(n_chunks - 2, n_chunks % 2)
    weighted_sum(n_chunks - 1, (n_chunks - 1) % 2)
    o0 = acc_scr[0][:, :LANES] / acc_scr[0][:, LANES:]
    o1 = acc_scr[1][:, :LANES] / acc_scr[1][:, LANES:]
    if diff_mode:
        lp = lam_ref[...]
        lam = (jnp.exp(jnp.sum(lp[0:1] * lp[1:2], axis=-1, keepdims=True))
               - jnp.exp(jnp.sum(lp[2:3] * lp[3:4], axis=-1, keepdims=True)) + lambda_init)
        o = o0 - lam * o1
        ms = jnp.mean(o * o, axis=-1, keepdims=True)
        o = o * lax.rsqrt(ms + NORM_EPS) * subln_ref[...] * (1.0 - lambda_init)
    else:
        o = jnp.where(lane < LANES // 2, o0, o1)
    o_ref[...] = o.astype(o_ref.dtype)


def _attention(q_arr, q_blk, k_arr, k_blk, v_arr, v_blk, *, batch, n_q, n_k, groups, slice_mode=False, diff=None,
               sub_width=LANES // 2, tq=512, tk=768, rb=16, unroll=16, name="attn"):
    qw = 2 * LANES if slice_mode else LANES
    tq = min(tq, n_q)
    tk = min(tk, n_k)
    assert n_q % tq == 0 and n_k % tk == 0 and tk % LANES == 0 and tq % rb == 0
    nqt = n_q // tq
    in_specs = [pl.BlockSpec((tq, qw), lambda b, g, i: (b * nqt + i, q_blk + g)),
                pl.BlockSpec((n_k, qw), lambda b, g, i: (b, k_blk + g)),
                pl.BlockSpec((n_k, LANES), lambda b, g, i: (b, v_blk + g))]
    args = [q_arr, k_arr, v_arr]
    lambda_init = 0.0
    if diff is not None:
        lam_params, subln, lambda_init = diff
        in_specs.append(pl.BlockSpec((4, DA_HEAD_DIM), lambda b, g, i: (0, 0)))
        in_specs.append(pl.BlockSpec((1, LANES), lambda b, g, i: (0, 0)))
        args += [lam_params, subln.reshape(1, LANES)]
    assert unroll % 2 == 0
    kern = functools.partial(_attn_kernel, n_chunks=n_k // tk, tk=tk, rb=rb, unroll=unroll, slice_mode=slice_mode,
                             sub_width=sub_width, diff_mode=diff is not None, lambda_init=lambda_init)
    scratch = 2 * ([pltpu.VMEM((tq, LANES), BF16)] + 2 * [pltpu.VMEM((tq, tk), F32)] + 2 * [pltpu.VMEM((tq, tk), BF16)]
                   + [pltpu.VMEM((tq, LANES), F32)] + 2 * [pltpu.VMEM((tq, LANES), F32)]
                   + [pltpu.VMEM((tq, 2 * LANES), F32)])
    return pl.pallas_call(
        kern, grid=(batch, groups, nqt), in_specs=in_specs,
        out_specs=pl.BlockSpec((tq, LANES), lambda b, g, i: (b * nqt + i, g)),
        out_shape=jax.ShapeDtypeStruct((batch * n_q, groups * LANES), BF16), scratch_shapes=scratch,
        compiler_params=_cparams(("parallel", "parallel", "arbitrary")), name=name)(*args)


def _na_kernel(q_ref, kc_ref, vc_ref, kl_ref, vl_ref, bias_ref, o_ref, *, n_rows):
    i = pl.program_id(2)
    k0 = jnp.clip(i * NA_Q_ROWS - NA_WIN_ROWS // 2, 0, n_rows - NA_K_ROWS)
    start = pl.multiple_of(k0 * GRID_W, GRID_W * 4)
    nkw = NA_K_ROWS * GRID_W
    kw = kl_ref[pl.ds(start, nkw), :]
    vw = vl_ref[pl.ds(start, nkw), :]
    kc = kc_ref[...]
    vc = vc_ref[...]
    q = q_ref[...]
    lane = lax.broadcasted_iota(jnp.int32, (1, LANES), 1)
    outs = []
    for s in range(2):
        qs = jnp.where((lane < LANES // 2) if s == 0 else (lane >= LANES // 2), q, jnp.zeros_like(q))
        dn = (((1,), (1,)), ((), ()))
        s_win = lax.dot_general(qs, kw, dn, preferred_element_type=F32) + bias_ref[s]
        s_ctx = lax.dot_general(qs, kc, dn, preferred_element_type=F32)
        m = jnp.maximum(jnp.max(s_win, axis=-1, keepdims=True), jnp.max(s_ctx, axis=-1, keepdims=True))
        p_win = jnp.exp2(s_win - m)
        p_ctx = jnp.exp2(s_ctx - m)
        l = jnp.sum(p_win, axis=-1, keepdims=True) + jnp.sum(p_ctx, axis=-1, keepdims=True)
        o = (jnp.dot(p_win.astype(BF16), vw, preferred_element_type=F32)
             + jnp.dot(p_ctx.astype(BF16), vc, preferred_element_type=F32))
        outs.append(o / l)
    o_ref[...] = jnp.where(lane < LANES // 2, outs[0], outs[1]).astype(o_ref.dtype)


def _na_bias_tables(rpb, n_rows):
    nb = n_rows // NA_Q_ROWS
    assert nb >= 3
    dc = np.arange(GRID_W)[None, :] - np.arange(GRID_W)[:, None] + NA_WIN_COLS - 1
    onehot = (dc[None] == np.arange(2 * NA_WIN_COLS - 1)[:, None, None]).astype(np.float32)
    toep = jnp.einsum('hrd,dqk->hrqk', rpb.astype(F32), jnp.asarray(onehot), precision=lax.Precision.HIGHEST)
    cols = np.arange(GRID_W)
    col_start = np.clip(cols - NA_WIN_COLS // 2, 0, GRID_W - NA_WIN_COLS)
    col_ok = (cols[None, :] >= col_start[:, None]) & (cols[None, :] < col_start[:, None] + NA_WIN_COLS)
    ridx = np.zeros((3, NA_Q_ROWS, NA_K_ROWS), np.int32)
    ok = np.zeros((3, NA_Q_ROWS, NA_K_ROWS), bool)
    for var, blk in enumerate((0, 1, nb - 1)):
        r_first = blk * NA_Q_ROWS
        k_first = int(np.clip(r_first - NA_WIN_ROWS // 2, 0, n_rows - NA_K_ROWS))
        for a in range(NA_Q_ROWS):
            qr = r_first + a
            r0 = int(np.clip(qr - NA_WIN_ROWS // 2, 0, n_rows - NA_WIN_ROWS))
            for kb in range(NA_K_ROWS):
                kr = k_first + kb
                if r0 <= kr < r0 + NA_WIN_ROWS:
                    ok[var, a, kb] = True
                    ridx[var, a, kb] = kr - qr + NA_WIN_ROWS - 1
    tab = toep[:, ridx]
    tab = jnp.transpose(tab, (0, 1, 2, 4, 3, 5))
    mask = ok[:, :, None, :, None] & col_ok[None, None, :, None, :]
    tab = jnp.where(jnp.asarray(mask)[None], tab, MASK_VALUE)
    h = rpb.shape[0]
    return tab.reshape(h, 3, NA_Q_ROWS * GRID_W, NA_K_ROWS * GRID_W)


def _na_attention(qkv_lat, qkv_ctx, bias_tab, *, batch, n_lat, n_ctx):
    d = NA_HEADS * NA_HEAD_DIM
    groups = d // LANES
    n_rows = n_lat // GRID_W
    nb = n_rows // NA_Q_ROWS
    tq = NA_Q_ROWS * GRID_W

    def var_of(i):
        return jnp.where(i == 0, 0, jnp.where(i == nb - 1, 2, 1))

    in_specs = [
        pl.BlockSpec((tq, LANES), lambda b, g, i: (b * nb + i, g)),
        pl.BlockSpec((n_ctx, LANES), lambda b, g, i: (b, groups + g)),
        pl.BlockSpec((n_ctx, LANES), lambda b, g, i: (b, 2 * groups + g)),
        pl.BlockSpec((n_lat, LANES), lambda b, g, i: (b, groups + g)),
        pl.BlockSpec((n_lat, LANES), lambda b, g, i: (b, 2 * groups + g)),
        pl.BlockSpec((2, None, tq, NA_K_ROWS * GRID_W), lambda b, g, i: (g, var_of(i), 0, 0)),
    ]
    return pl.pallas_call(
        functools.partial(_na_kernel, n_rows=n_rows), grid=(batch, groups, nb), in_specs=in_specs,
        out_specs=pl.BlockSpec((tq, LANES), lambda b, g, i: (b * nb + i, g)),
        out_shape=jax.ShapeDtypeStruct((batch * n_lat, d), BF16),
        compiler_params=_cparams(("parallel", "parallel", "arbitrary")), name="na_attn",
    )(qkv_lat, qkv_ctx, qkv_ctx, qkv_lat, qkv_lat, bias_tab)


def _dft_kernel(c_ref, ms_ref, ab_ref, o_ref, acc_ref, *, scale):
    n = pl.program_id(2)
    d = o_ref.shape[1]
    part = (jnp.dot(c_ref[...], ab_ref[:, :d], preferred_element_type=F32)
            + jnp.dot(ms_ref[...], ab_ref[:, d:], preferred_element_type=F32))

    @pl.when(n == 0)
    def _():
        acc_ref[...] = part

    @pl.when(n > 0)
    def _():
        acc_ref[...] += part

    @pl.when(n == pl.num_programs(2) - 1)
    def _():
        o_ref[...] = (acc_ref[...] * scale).astype(o_ref.dtype)


def _pos_dft(cos_t, msin_t, ab, *, batch, n_tok, d, scale, name):
    t = min(1024, n_tok)
    nt = n_tok // t
    return pl.pallas_call(
        functools.partial(_dft_kernel, scale=scale), grid=(batch, nt, nt),
        in_specs=[pl.BlockSpec((t, t), lambda b, k, n: (k, n)),
                  pl.BlockSpec((t, t), lambda b, k, n: (k, n)),
                  pl.BlockSpec((t, 2 * d), lambda b, k, n: (b * nt + n, 0))],
        out_specs=pl.BlockSpec((t, d), lambda b, k, n: (b * nt + k, 0)),
        out_shape=jax.ShapeDtypeStruct((batch * n_tok, d), BF16),
        scratch_shapes=[pltpu.VMEM((t, d), F32)],
        compiler_params=_cparams(("parallel", "parallel", "arbitrary")), name=name)(cos_t, msin_t, ab)


def _dft_tables(n):
    split = 64 if n % 64 == 0 and n > 64 else 1
    m = jnp.arange(n, dtype=jnp.int32)[None, :]

    def phase(kvals):
        ang = ((kvals[:, None] * m) % n).astype(F32) * (2.0 * math.pi / n)
        return jnp.cos(ang), jnp.sin(ang)

    c1, s1 = phase(jnp.arange(n // split, dtype=jnp.int32) * split)
    c2, s2 = phase(jnp.arange(split, dtype=jnp.int32))
    c = c1[:, None, :] * c2[None, :, :] - s1[:, None, :] * s2[None, :, :]
    s = s1[:, None, :] * c2[None, :, :] + c1[:, None, :] * s2[None, :, :]
    return c.reshape(n, n).astype(BF16), (-s).reshape(n, n).astype(BF16)


def _channel_dft_weight(d):
    cg = d // FN_GROUPS
    idx = np.arange(cg)
    ang = 2.0 * np.pi * ((idx[:, None] * idx[None, :]) % cg) / cg
    eye = np.eye(FN_GROUPS)
    return jnp.asarray(np.concatenate([np.kron(eye, np.cos(ang)), np.kron(eye, np.sin(ang))], axis=1), F32)


def _router_kernel(x_ref, gain_ref, sc_ref, sh_ref, wr_ref, h_ref, aff_ref):
    x = x_ref[...]
    ms = jnp.mean(x * x, axis=-1, keepdims=True)
    h = x * lax.rsqrt(ms + NORM_EPS) * gain_ref[...]
    h = h * (1.0 + sc_ref[...]) + sh_ref[...]
    h_ref[...] = h.astype(h_ref.dtype)
    w = wr_ref[...]
    h_hi, w_hi = h.astype(BF16), w.astype(BF16)
    h_lo, w_lo = (h - h_hi.astype(F32)).astype(BF16), (w - w_hi.astype(F32)).astype(BF16)
    logits = (jnp.dot(h_hi, w_hi, preferred_element_type=F32) + jnp.dot(h_lo, w_hi, preferred_element_type=F32)
              + jnp.dot(h_hi, w_lo, preferred_element_type=F32))
    e = jnp.exp(logits - jnp.max(logits, axis=-1, keepdims=True))
    aff_ref[...] = e / jnp.sum(e, axis=-1, keepdims=True)


def _norm_router(x, gain, sc, sh, w_router, layer, *, rows_per_batch, tm=512):
    m, d = x.shape
    e = w_router.shape[-1]
    tm = min(tm, rows_per_batch)
    tpb = rows_per_batch // tm
    return pl.pallas_call(
        _router_kernel, grid=(m // tm,),
        in_specs=[pl.BlockSpec((tm, d), lambda i: (i, 0)),
                  pl.BlockSpec((1, d), lambda i: (0, 0)),
                  pl.BlockSpec((None, 1, d), lambda i: (i // tpb, 0, 0)),
                  pl.BlockSpec((None, 1, d), lambda i: (i // tpb, 0, 0)),
                  pl.BlockSpec((None, d, e), lambda i: (layer, 0, 0))],
        out_specs=[pl.BlockSpec((tm, d), lambda i: (i, 0)), pl.BlockSpec((tm, e), lambda i: (i, 0))],
        out_shape=[jax.ShapeDtypeStruct((m, d), BF16), jax.ShapeDtypeStruct((m, e), F32)],
        compiler_params=_cparams(("parallel",)), name="norm_router")(x, gain.reshape(1, d), sc, sh, w_router)


def _ffn_kernel(*refs, n_streams):
    x_refs = refs[:n_streams]
    wg_ref, wu_ref, wd_ref = refs[n_streams:n_streams + 3]
    gate_refs = refs[n_streams + 3:2 * n_streams + 3]
    o_refs = refs[2 * n_streams + 3:3 * n_streams + 3]
    acc_ref = refs[3 * n_streams + 3]
    f = pl.program_id(2)
    x = x_refs[0][...] if n_streams == 1 else jnp.concatenate([r[...] for r in x_refs], axis=0)
    a = jnp.dot(x, wg_ref[...].astype(BF16), preferred_element_type=F32)
    u = jnp.dot(x, wu_ref[...].astype(BF16), preferred_element_type=F32)
    act = (a * (1.0 / (1.0 + jnp.exp(-a)))) * u
    y = jnp.dot(act.astype(BF16), wd_ref[...].astype(BF16), preferred_element_type=F32)

    @pl.when(f == 0)
    def _():
        acc_ref[...] = y

    @pl.when(f > 0)
    def _():
        acc_ref[...] += y

    @pl.when(f == pl.num_programs(2) - 1)
    def _():
        row = 0
        for o_ref, g_ref in zip(o_refs, gate_refs):
            c = o_ref.shape[0]
            o_ref[...] = acc_ref[row:row + c, :] * g_ref[...]
            row += c


def _moe_ffn(xs, gates, w_gate, w_up, w_down, layer, *, tf=512):
    b, e, _, d = xs[0].shape
    f = w_gate.shape[-1]
    tf = min(tf, f)
    slot = lambda c, w: pl.BlockSpec((None, None, c, w), lambda ei, bi, fi: (bi, ei, 0, 0))
    caps = [t.shape[2] for t in xs]
    in_specs = ([slot(c, d) for c in caps]
                + [pl.BlockSpec((None, None, d, tf), lambda ei, bi, fi: (layer, ei, 0, fi)),
                   pl.BlockSpec((None, None, d, tf), lambda ei, bi, fi: (layer, ei, 0, fi)),
                   pl.BlockSpec((None, None, tf, d), lambda ei, bi, fi: (layer, ei, fi, 0))]
                + [slot(c, 1) for c in caps])
    return pl.pallas_call(
        functools.partial(_ffn_kernel, n_streams=len(xs)), grid=(e, b, f // tf), in_specs=in_specs,
        out_specs=[slot(c, d) for c in caps],
        out_shape=[jax.ShapeDtypeStruct((b, e, c, d), F32) for c in caps],
        scratch_shapes=[pltpu.VMEM((sum(caps), d), F32)],
        compiler_params=_cparams(("parallel", "parallel", "arbitrary")), name="moe_ffn",
    )(*xs, w_gate, w_up, w_down, *gates)


def _route(aff, h, batch, n_tok):
    e = aff.shape[-1]
    cap = EC_CAPACITY_FACTOR * n_tok // e
    gate, idx = lax.top_k(jnp.swapaxes(aff.reshape(batch, n_tok, e), 1, 2), cap)
    idx, gate = lax.sort((idx, gate), dimension=2, num_keys=1)
    xs = jax.vmap(lambda hb, ib: hb[ib])(h.reshape(batch, n_tok, -1), idx)
    return gate, idx, xs


COMBINE_TOKENS = 512
COMBINE_WINDOW = 128


def _combine_kernel(lo_ref, nr_ref, x_ref, g_ref, pos_ref, y_hbm, o_ref, stage, sem, acc, *, n_tiles, n_exp, cap):
    b = pl.program_id(0)
    t = pl.program_id(1)
    w = COMBINE_WINDOW
    tile = b * n_tiles + t
    lane = lax.broadcasted_iota(jnp.int32, (1, w), 1)
    acc[...] = jnp.zeros(acc.shape, F32)

    def window(e, r):
        first = lo_ref[tile * n_exp + e] + r * w
        return first, pl.multiple_of(jnp.minimum(first, cap - w), 8)

    def copy(e, start):
        return pltpu.make_async_copy(y_hbm.at[b, e, pl.ds(start, w), :], stage.at[pl.ds(e * w, w), :], sem.at[e])

    @pl.loop(0, nr_ref[tile])
    def _(r):
        for e in range(n_exp):
            copy(e, window(e, r)[1]).start()
        pos = pos_ref[...]
        blocks = []
        for e in range(n_exp):
            first, start = window(e, r)
            col = pos[:, e:e + 1]
            hit = jnp.logical_and(col - start == lane, col >= first)
            blocks.append(jnp.where(hit, 1.0, 0.0).astype(BF16))
        onehot = jnp.concatenate(blocks, axis=1)
        for e in range(n_exp):
            copy(e, window(e, r)[1]).wait()
        y = stage[...]
        y_hi = y.astype(BF16)
        y_lo = (y - y_hi.astype(F32)).astype(BF16)
        acc[...] += (jnp.dot(onehot, y_hi, preferred_element_type=F32)
                     + jnp.dot(onehot, y_lo, preferred_element_type=F32))

    o_ref[...] = x_ref[...] + g_ref[...] * acc[...]


def _combine_lat(x, g2, y, idx, batch, n_tok):
    d = x.shape[-1]
    _, n_exp, cap = idx.shape
    tt, w = COMBINE_TOKENS, COMBINE_WINDOW
    assert n_tok % tt == 0 and cap % 8 == 0 and cap >= w
    nt = n_tok // tt
    bounds = jnp.arange(nt + 1, dtype=jnp.int32) * tt
    below = jnp.sum((idx[..., None] < bounds).astype(jnp.int32), axis=2)
    lo = (below[..., :-1] // 8) * 8
    rounds = jnp.max((below[..., 1:] - lo + w - 1) // w, axis=1)
    rounds = jnp.where(jnp.max(below[..., 1:] - below[..., :-1], axis=1) > 0, rounds, 0)
    lo_flat = jnp.transpose(lo, (0, 2, 1)).reshape(-1)
    bi = jnp.arange(batch)[:, None, None]
    ei = jnp.arange(n_exp)[None, :, None]
    pos = jnp.full((batch, n_tok, n_exp), -1, jnp.int32).at[bi, idx, ei].set(
        jnp.broadcast_to(jnp.arange(cap, dtype=jnp.int32), idx.shape))
    grid_spec = pltpu.PrefetchScalarGridSpec(
        num_scalar_prefetch=2, grid=(batch, nt),
        in_specs=[pl.BlockSpec((tt, d), lambda b, t, lo_r, nr_r: (b * nt + t, 0)),
                  pl.BlockSpec((None, 1, d), lambda b, t, lo_r, nr_r: (b, 0, 0)),
                  pl.BlockSpec((tt, n_exp), lambda b, t, lo_r, nr_r: (b * nt + t, 0)),
                  pl.BlockSpec(memory_space=pl.ANY)],
        out_specs=pl.BlockSpec((tt, d), lambda b, t, lo_r, nr_r: (b * nt + t, 0)),
        scratch_shapes=[pltpu.VMEM((n_exp * w, d), F32), pltpu.SemaphoreType.DMA((n_exp,)), pltpu.VMEM((tt, d), F32)])
    return pl.pallas_call(
        functools.partial(_combine_kernel, n_tiles=nt, n_exp=n_exp, cap=cap), grid_spec=grid_spec,
        out_shape=jax.ShapeDtypeStruct(x.shape, F32),
        compiler_params=_cparams(("parallel", "arbitrary")), name="moe_combine",
    )(lo_flat, rounds.reshape(-1), x, g2, pos.reshape(batch * n_tok, n_exp), y)


def _combine(x, g2, y, idx, batch, n_tok):
    d = x.shape[-1]
    upd = jax.vmap(lambda ib, yb: jnp.zeros((n_tok, d), F32).at[ib.reshape(-1)].add(yb.reshape(-1, d)))(idx, y)
    return (x.reshape(batch, n_tok, d) + g2 * upd).reshape(batch * n_tok, d)


def _final_norm_kernel(x_ref, g_ref, o_ref):
    x = x_ref[...]
    ms = jnp.mean(x * x, axis=-1, keepdims=True)
    o_ref[...] = x * lax.rsqrt(ms + NORM_EPS) * g_ref[...]


def _final_norm(x, g, tm=1024):
    m, d = x.shape
    tm = min(tm, m)
    return pl.pallas_call(
        _final_norm_kernel, grid=(m // tm,),
        in_specs=[pl.BlockSpec((tm, d), lambda i: (i, 0)), pl.BlockSpec((1, d), lambda i: (0, 0))],
        out_specs=pl.BlockSpec((tm, d), lambda i: (i, 0)), out_shape=jax.ShapeDtypeStruct((m, d), F32),
        compiler_params=_cparams(("parallel",)), name="final_norm")(x, g.reshape(1, d))


def _rope_tables(n_tok, rot_dim, pair_of_lane):
    t = jnp.arange(n_tok)
    rows = (t // GRID_W).astype(F32)
    cols = (t % GRID_W).astype(F32)
    n_freq = rot_dim // 4
    inv_freq = ROPE_BASE ** (-jnp.arange(n_freq, dtype=F32) / n_freq)
    ang = jnp.concatenate([rows[:, None] * inv_freq, cols[:, None] * inv_freq], axis=-1)
    cos, sin = jnp.cos(ang), jnp.sin(ang)
    pair = np.concatenate([pair_of_lane, pair_of_lane])
    used = jnp.asarray(pair >= 0)[None]
    sign = jnp.asarray(np.where(np.arange(LANES) < LANES // 2, -1.0, 1.0).astype(np.float32))[None]
    cos_l = jnp.where(used, cos[:, np.maximum(pair, 0)], 1.0)
    sin_l = jnp.where(used, sin[:, np.maximum(pair, 0)] * sign, 0.0)
    return cos_l, sin_l


def kernel(x, c, ctx, c_ctx, ada_w, ada_b, norm_mix, norm_ffn, norm_final, da_w_qkv, da_w_o, da_lambda_q1,
           da_lambda_k1, da_lambda_q2, da_lambda_k2, da_subln, fn_w_o, na_w_qkv, na_w_o, na_rpb, mla_w_dq,
           mla_q_norm, mla_w_uq, mla_w_dkv, mla_kv_norm, mla_w_uk, mla_w_uv, mla_w_o, moe_w_router, moe_w_gate,
           moe_w_up, moe_w_down):
    batch, n_lat, d = x.shape
    n_ctx = ctx.shape[1]
    depth = ada_w.shape[0]
    n_mixers = 4
    xl = x.reshape(batch * n_lat, d)
    xc = ctx.reshape(batch * n_ctx, d)

    cond = jnp.concatenate([c, c_ctx[None], jnp.zeros((8 - batch - 1, d), F32)], axis=0)
    mods = [
        _fused_mm(cond, ada_w, w_index=i, bias=ada_b[i], silu_in=True, precision=lax.Precision.HIGHEST, tn=1024,
                  name="ada_mod")
        for i in range(depth)
    ]

    def chunks(i):
        lat = [mods[i][:batch, k * d:(k + 1) * d].reshape(batch, 1, d) for k in range(ADA_CHUNKS)]
        cx = [mods[i][batch:batch + 1, k * d:(k + 1) * d].reshape(1, 1, d) for k in range(ADA_CHUNKS)]
        return lat, cx

    def all_keys(t_ctx, t_lat):
        w = t_ctx.shape[-1]
        both = jnp.concatenate([t_ctx.reshape(batch, n_ctx, w), t_lat.reshape(batch, n_lat, w)], axis=1)
        return both.reshape(batch * (n_ctx + n_lat), w)

    for i in range(depth):
        kind, j = i % n_mixers, i // n_mixers
        keep_ctx = i < depth - 1
        (sh1, sc1, g1, sh2, sc2, g2), (csh1, csc1, cg1, csh2, csc2, cg2) = chunks(i)
        lat_in = dict(gain=norm_mix[i], mod=(sc1, sh1), rows_per_batch=n_lat)
        ctx_in = dict(gain=norm_mix[i], mod=(csc1, csh1), rows_per_batch=batch * n_ctx, tm=n_ctx)

        if kind == 0:
            lambda_init = 0.8 - 0.6 * math.exp(-0.3 * i)
            scale = jnp.concatenate([jnp.full((d,), LOG2E * DA_HEAD_DIM ** -0.5, F32), jnp.ones((2 * d,), F32)])
            groups = d // LANES
            half = DA_HEAD_DIM // 2
            w = da_w_qkv[j]

            def reorder(cols):
                return cols.reshape(d, groups, 2, 2, half).transpose(0, 1, 3, 2, 4).reshape(d, d)

            w = jnp.concatenate([reorder(w[:, :d]), reorder(w[:, d:2 * d]), w[:, 2 * d:]], axis=1)
            rope = _rope_tables(n_lat, DA_HEAD_DIM, np.arange(LANES // 2) % half)
            qkv_l = _fused_mm(xl, w, colscale=scale, rope=rope, rope_cols=2 * d, out_dtype=BF16, name="da_qkv_lat",
                              **lat_in)
            qkv_c = _fused_mm(xc, w, colscale=scale, out_dtype=BF16, name="da_qkv_ctx", **ctx_in)
            lam_params = jnp.stack([da_lambda_q1[j], da_lambda_k1[j], da_lambda_q2[j], da_lambda_k2[j]])
            diff = (lam_params, da_subln[j], lambda_init)
            kv = all_keys(qkv_c[:, d:], qkv_l[:, d:])
            o_l = _attention(qkv_l, 0, kv, 0, kv, groups, batch=batch, n_q=n_lat, n_k=n_ctx + n_lat, groups=groups,
                             sub_width=half, diff=diff, name="da_attn_lat")
            o_c = _attention(qkv_c, 0, qkv_c, groups, qkv_c, 2 * groups, batch=batch, n_q=n_ctx, n_k=n_ctx,
                             groups=groups, sub_width=half, diff=diff, name="da_attn_ctx") if keep_ctx else None
            w_o = da_w_o
        elif kind == 1:
            w_cd = _channel_dft_weight(d)
            ab_l = _fused_mm(xl, w_cd, out_dtype=BF16, name="fn_chan_lat", **lat_in)
            scale_l = 1.0 / math.sqrt(n_lat * (d // FN_GROUPS))
            o_l = _pos_dft(*_dft_tables(n_lat), ab_l, batch=batch, n_tok=n_lat, d=d, scale=scale_l, name="fn_pos_lat")
            o_c = None
            if keep_ctx:
                ab_c = _fused_mm(xc, w_cd, out_dtype=BF16, name="fn_chan_ctx", **ctx_in)
                scale_c = 1.0 / math.sqrt(n_ctx * (d // FN_GROUPS))
                o_c = _pos_dft(*_dft_tables(n_ctx), ab_c, batch=batch, n_tok=n_ctx, d=d, scale=scale_c,
                               name="fn_pos_ctx")
            w_o = fn_w_o
        elif kind == 2:
            scale = jnp.concatenate([jnp.full((d,), LOG2E * NA_HEAD_DIM ** -0.5, F32), jnp.ones((2 * d,), F32)])
            qkv_l = _fused_mm(xl, na_w_qkv, w_index=j, colscale=scale, out_dtype=BF16, name="na_qkv_lat", **lat_in)
            qkv_c = _fused_mm(xc, na_w_qkv, w_index=j, colscale=scale, out_dtype=BF16, name="na_qkv_ctx", **ctx_in)
            groups = d // LANES
            bias_tab = _na_bias_tables(na_rpb[j] * LOG2E, n_lat // GRID_W)
            o_l = _na_attention(qkv_l, qkv_c, bias_tab, batch=batch, n_lat=n_lat, n_ctx=n_ctx)
            o_c = _attention(qkv_c, 0, qkv_c, groups, qkv_c, 2 * groups, batch=batch, n_q=n_ctx, n_k=n_ctx,
                             groups=groups, name="na_attn_ctx") if keep_ctx else None
            w_o = na_w_o
        else:
            q_rank = mla_w_dq.shape[-1]
            kv_rank = mla_w_uk.shape[-2]
            hq = MLA_NOPE_DIM + MLA_ROPE_DIM
            down_w = 3 * 384
            assert q_rank == 768 and kv_rank == 256
            w_down = jnp.concatenate([mla_w_dq[j], mla_w_dkv[j], jnp.zeros((d, down_w - q_rank - kv_rank - MLA_ROPE_DIM), F32)],
                                     axis=1)
            rh = MLA_ROPE_DIM // 2
            n_a = LANES // 2 - rh

            def head_lanes(nope, rope_part):
                z = jnp.zeros(nope.shape[:-1] + (LANES - hq,), F32)
                return jnp.concatenate([rope_part[..., :rh], nope[..., :n_a], rope_part[..., rh:], nope[..., n_a:], z],
                                       axis=-1)

            wq = mla_w_uq[j].reshape(q_rank, MLA_HEADS, hq)
            wq = head_lanes(wq[..., :MLA_NOPE_DIM], wq[..., MLA_NOPE_DIM:]).reshape(q_rank, MLA_HEADS * LANES)
            wk_nope = head_lanes(mla_w_uk[j].reshape(kv_rank, MLA_HEADS, MLA_NOPE_DIM),
                                 jnp.zeros((kv_rank, MLA_HEADS, MLA_ROPE_DIM), F32))
            eye = jnp.broadcast_to(jnp.eye(MLA_ROPE_DIM, dtype=F32)[:, None, :], (MLA_ROPE_DIM, MLA_HEADS, MLA_ROPE_DIM))
            wk_rope = head_lanes(jnp.zeros((MLA_ROPE_DIM, MLA_HEADS, MLA_NOPE_DIM), F32), eye)
            kw = MLA_HEADS * LANES
            vw = MLA_HEADS * MLA_V_DIM
            pad_rows = 384 - kv_rank - MLA_ROPE_DIM
            w_kv = jnp.concatenate([
                jnp.concatenate([wk_nope.reshape(kv_rank, kw), mla_w_uv[j]], axis=1),
                jnp.concatenate([wk_rope.reshape(MLA_ROPE_DIM, kw), jnp.zeros((MLA_ROPE_DIM, vw), F32)], axis=1),
                jnp.zeros((pad_rows, kw + vw), F32)], axis=0)
            kv_gain = jnp.concatenate([mla_kv_norm[j], jnp.ones((384 - kv_rank,), F32)])
            rope = _rope_tables(n_lat, MLA_ROPE_DIM, np.where(np.arange(LANES // 2) < rh, np.arange(LANES // 2), -1))
            qscale = jnp.full((kw,), LOG2E * hq ** -0.5, F32)

            def project(xs, inp, n_tok, rope_tabs, tag):
                low = _fused_mm(xs, w_down, tn=384, name="mla_down_" + tag, **inp)
                rp = dict(rope=rope_tabs) if rope_tabs is not None else {}
                tmx = inp.get("tm", 1024)
                q = _fused_mm(low, wq, a_cols=(0, q_rank), gain=mla_q_norm[j], colscale=qscale, rope_cols=kw,
                              rows_per_batch=inp["rows_per_batch"], tm=tmx, out_dtype=BF16, name="mla_q_" + tag, **rp)
                kv = _fused_mm(low, w_kv, a_cols=(q_rank, 384), gain=kv_gain, norm_cols=kv_rank, rope_cols=kw,
                               rows_per_batch=inp["rows_per_batch"], tm=tmx, out_dtype=BF16, name="mla_kv_" + tag, **rp)
                return q, kv

            q_l, kv_l = project(xl, lat_in, n_lat, rope, "lat")
            q_c, kv_c = project(xc, ctx_in, n_ctx, None, "ctx")
            groups = MLA_HEADS // 2
            kv = all_keys(kv_c, kv_l)
            o_l = _attention(q_l, 0, kv, 0, kv, 2 * groups, batch=batch, n_q=n_lat, n_k=n_ctx + n_lat, groups=groups,
                             slice_mode=True, name="mla_attn_lat")
            o_c = _attention(q_c, 0, kv_c, 0, kv_c, 2 * groups, batch=batch, n_q=n_ctx, n_k=n_ctx, groups=groups,
                             slice_mode=True, name="mla_attn_ctx") if keep_ctx else None
            w_o = mla_w_o

        xl = _fused_mm(o_l, w_o, w_index=j, res=xl, gate=g1, rows_per_batch=n_lat, name="mix_out_lat")
        if keep_ctx:
            xc = _fused_mm(o_c, w_o, w_index=j, res=xc, gate=jnp.broadcast_to(cg1, (1, 1, d)),
                           rows_per_batch=batch * n_ctx, tm=n_ctx, name="mix_out_ctx")

        h_l, aff_l = _norm_router(xl, norm_ffn[i], sc2, sh2, moe_w_router, i, rows_per_batch=n_lat)
        gate_l, idx_l, xs_l = _route(aff_l, h_l, batch, n_lat)
        xs, gates = [xs_l], [gate_l[..., None]]
        if keep_ctx:
            h_c, aff_c = _norm_router(xc, norm_ffn[i], csc2, csh2, moe_w_router, i, rows_per_batch=batch * n_ctx,
                                      tm=n_ctx)
            gate_c, idx_c, xs_c = _route(aff_c, h_c, batch, n_ctx)
            xs.append(xs_c)
            gates.append(gate_c[..., None])
        ys = _moe_ffn(xs, gates, moe_w_gate, moe_w_up, moe_w_down, i)
        xl = _combine_lat(xl, g2, ys[0], idx_l, batch, n_lat)
        if keep_ctx:
            xc = _combine(xc, cg2, ys[1], idx_c, batch, n_ctx)

    return _final_norm(xl, norm_final).reshape(batch, n_lat, d)
```

```python
import functools
import math

import numpy as np
import jax
import jax.numpy as jnp
from jax import lax
from jax.experimental import pallas as pl
from jax.experimental.pallas import tpu as pltpu

F32 = jnp.float32
BF16 = jnp.bfloat16

LANES = 128
VMEM_LIMIT_BYTES = 56 * 1024 * 1024

GRID_W = 64
ROPE_BASE = 10000.0
NORM_EPS = 1e-6
ADA_CHUNKS = 6
DA_HEAD_DIM = 64
NA_HEADS = 16
NA_HEAD_DIM = 64
NA_WIN_ROWS = 8
NA_WIN_COLS = 16
NA_Q_ROWS = 8
NA_K_ROWS = 16
MLA_HEADS = 16
MLA_NOPE_DIM = 64
MLA_ROPE_DIM = 32
MLA_V_DIM = 64
FN_GROUPS = 4
EC_CAPACITY_FACTOR = 2
MASK_VALUE = -1e30
LOG2E = math.log2(math.e)


def _cparams(sem):
    return pltpu.CompilerParams(dimension_semantics=sem, vmem_limit_bytes=VMEM_LIMIT_BYTES)


def _rope_lanes(acc, cos, sin):
    outs = []
    for g in range(acc.shape[1] // LANES):
        blk = acc[:, g * LANES:(g + 1) * LANES]
        outs.append(blk * cos + pltpu.roll(blk, LANES // 2, 1) * sin)
    return outs[0] if len(outs) == 1 else jnp.concatenate(outs, axis=1)


def _mm_kernel(*refs, norm_cols, has_gain, has_mod, has_colscale, rope_tiles, has_bias, has_res,
               silu_in, precision):
    it = iter(refs)
    a_ref = next(it)
    w_ref = next(it)
    gain_ref = next(it) if has_gain else None
    sc_ref, sh_ref = (next(it), next(it)) if has_mod else (None, None)
    cs_ref = next(it) if has_colscale else None
    cos_ref, sin_ref = (next(it), next(it)) if rope_tiles else (None, None)
    bias_ref = next(it) if has_bias else None
    res_ref, gate_ref = (next(it), next(it)) if has_res else (None, None)
    o_ref = next(it)
    h_scr = next(it, None)
    j = pl.program_id(1)
    hi = precision is not None

    def prologue():
        a = a_ref[...].astype(F32)
        if silu_in:
            a = a * (1.0 / (1.0 + jnp.exp(-a)))
        if has_gain:
            k = a.shape[1]
            if norm_cols < k:
                col = lax.broadcasted_iota(jnp.int32, (1, k), 1)
                sq = jnp.where(col < norm_cols, a * a, 0.0)
            else:
                sq = a * a
            ms = jnp.sum(sq, axis=-1, keepdims=True) * (1.0 / norm_cols)
            y = a * lax.rsqrt(ms + NORM_EPS) * gain_ref[...]
            a = jnp.where(col < norm_cols, y, a) if norm_cols < k else y
        if has_mod:
            a = a * (1.0 + sc_ref[...]) + sh_ref[...]
        return a if hi else a.astype(BF16)

    if h_scr is not None:
        @pl.when(j == 0)
        def _():
            h_scr[...] = prologue()
        h = h_scr[...]
    else:
        h = a_ref[...] if hi else a_ref[...].astype(BF16)
    w = w_ref[...] if hi else w_ref[...].astype(BF16)
    acc = jnp.dot(h, w, preferred_element_type=F32, precision=precision)
    if has_colscale:
        acc = acc * cs_ref[...]

    def finish(v):
        if has_bias:
            v = v + bias_ref[...]
        if has_res:
            v = res_ref[...] + gate_ref[...] * v
        o_ref[...] = v.astype(o_ref.dtype)

    if rope_tiles:
        @pl.when(j < rope_tiles)
        def _():
            finish(_rope_lanes(acc, cos_ref[...], sin_ref[...]))

        @pl.when(j >= rope_tiles)
        def _():
            finish(acc)
    else:
        finish(acc)


def _fused_mm(a, w, *, w_index=None, a_cols=None, gain=None, norm_cols=None, mod=None, colscale=None, rope=None,
              rope_cols=0, bias=None, res=None, gate=None, silu_in=False, rows_per_batch=None,
              out_dtype=F32, tm=1024, tn=1024, precision=None, name="mm"):
    m = a.shape[0]
    k0, k = a_cols if a_cols is not None else (0, a.shape[1])
    if precision is None and w.dtype != BF16:
        w = (w[w_index] if w.ndim == 3 else w).astype(BF16)
    n = w.shape[-1]
    assert w.shape[-2] == k and k0 % k == 0
    tm = min(tm, m)
    tn = min(tn, n)
    assert m % tm == 0 and n % tn == 0
    rows_per_batch = rows_per_batch or m
    assert rows_per_batch % tm == 0
    tpb = rows_per_batch // tm
    has_prologue = gain is not None or mod is not None or silu_in
    in_specs = [pl.BlockSpec((tm, k), lambda i, j: (i, k0 // k))]
    args = [a]
    if w.ndim == 3:
        in_specs.append(pl.BlockSpec((None, k, tn), lambda i, j: (w_index, 0, j)))
    else:
        in_specs.append(pl.BlockSpec((k, tn), lambda i, j: (0, j)))
    args.append(w)
    if gain is not None:
        in_specs.append(pl.BlockSpec((1, k), lambda i, j: (0, 0)))
        args.append(gain.reshape(1, k).astype(F32))
    if mod is not None:
        for t in mod:
            in_specs.append(pl.BlockSpec((None, 1, k), lambda i, j: (i // tpb, 0, 0)))
            args.append(t)
    if colscale is not None:
        in_specs.append(pl.BlockSpec((1, tn), lambda i, j: (0, j)))
        args.append(colscale.reshape(1, n).astype(F32))
    rope_tiles = 0
    if rope is not None:
        assert rope_cols % tn == 0
        rope_tiles = rope_cols // tn
        for t in rope:
            in_specs.append(pl.BlockSpec((tm, LANES), lambda i, j: (i % tpb, 0)))
            args.append(t)
    if bias is not None:
        in_specs.append(pl.BlockSpec((1, tn), lambda i, j: (0, j)))
        args.append(bias.reshape(1, n).astype(F32))
    if res is not None:
        in_specs.append(pl.BlockSpec((tm, tn), lambda i, j: (i, j)))
        args.append(res)
        in_specs.append(pl.BlockSpec((None, 1, tn), lambda i, j: (i // tpb, 0, j)))
        args.append(gate)
    scratch = [pltpu.VMEM((tm, k), F32 if precision is not None else BF16)] if has_prologue else []
    kern = functools.partial(
        _mm_kernel, norm_cols=norm_cols or k, has_gain=gain is not None, has_mod=mod is not None,
        has_colscale=colscale is not None, rope_tiles=rope_tiles, has_bias=bias is not None,
        has_res=res is not None, silu_in=silu_in, precision=precision)
    return pl.pallas_call(
        kern, grid=(m // tm, n // tn), in_specs=in_specs,
        out_specs=pl.BlockSpec((tm, tn), lambda i, j: (i, j)),
        out_shape=jax.ShapeDtypeStruct((m, n), out_dtype), scratch_shapes=scratch,
        compiler_params=_cparams(("parallel", "arbitrary")), name=name)(*args)


def _attn_kernel(*refs, n_chunks, tk, rb, unroll, slice_mode, sub_width, diff_mode, lambda_init):
    q_ref, k_ref, v_ref = refs[:3]
    pos = 3
    if diff_mode:
        lam_ref, subln_ref = refs[pos], refs[pos + 1]
        pos += 2
    o_ref = refs[pos]
    per = [refs[pos + 1 + 9 * s:pos + 10 + 9 * s] for s in range(2)]
    qm_scr = [t[0] for t in per]
    s_scr = [t[1:3] for t in per]
    p_scr = [t[3:5] for t in per]
    m_scr = [t[5] for t in per]
    a_scr = [t[6:8] for t in per]
    acc_scr = [t[8] for t in per]
    tq = q_ref.shape[0]
    q = q_ref[...]
    lane = lax.broadcasted_iota(jnp.int32, (1, LANES), 1)
    for s in range(2):
        if slice_mode:
            qm_scr[s][...] = q[:, s * LANES:(s + 1) * LANES]
        else:
            qm_scr[s][...] = jnp.where((lane // sub_width) % 2 == s, q, jnp.zeros_like(q))
        m_scr[s][...] = jnp.full(m_scr[s].shape, MASK_VALUE, F32)
        acc_scr[s][...] = jnp.zeros(acc_scr[s].shape, F32)

    def chunk_rows(c):
        return pl.ds(c * tk if isinstance(c, int) else pl.multiple_of(c * tk, tk), tk)

    def scores(c, slot):
        k = k_ref[chunk_rows(c), :]
        for s in range(2):
            ks = k[:, s * LANES:(s + 1) * LANES] if slice_mode else k
            s_scr[s][slot][...] = lax.dot_general(qm_scr[s][...], ks, (((1,), (1,)), ((), ())),
                                                  preferred_element_type=F32)

    def softmax(slot):
        for s in range(2):
            for r in range(tq // rb):
                rows = slice(r * rb, (r + 1) * rb)
                sb = s_scr[s][slot][rows, :]
                m_old = m_scr[s][rows, :]
                m_new = jnp.maximum(m_old, jnp.max(sb, axis=-1, keepdims=True))
                m_scr[s][rows, :] = m_new
                a_scr[s][slot][rows, :] = jnp.exp2(m_old - m_new)
                p_scr[s][slot][rows, :] = jnp.exp2(sb - jnp.tile(m_new, (1, tk // LANES))).astype(BF16)

    def weighted_sum(c, slot):
        v = v_ref[chunk_rows(c), :]
        vext = jnp.concatenate([v, jnp.ones((tk, LANES), BF16)], axis=1)
        for s in range(2):
            alpha = a_scr[s][slot][...]
            pv = jnp.dot(p_scr[s][slot][...], vext, preferred_element_type=F32)
            acc_scr[s][...] = jnp.concatenate([alpha, alpha], axis=1) * acc_scr[s][...] + pv

    def step(c, slot):
        scores(c, slot)
        softmax(1 - slot)
        weighted_sum(c - 2, slot)

    scores(0, 0)
    if n_chunks > 1:
        scores(1, 1)
    softmax(0)
    if n_chunks > 2:
        n_loop = (n_chunks - 2) // unroll
        if n_loop:
            @pl.loop(0, n_loop)
            def _(j):
                for u in range(unroll):
                    step(unroll * j + 2 + u, u % 2)
        for c in range(2 + n_loop * unroll, n_chunks):
            step(c, c % 2)
    if n_chunks > 1:
        softmax((n_chunks - 1) % 2)
        weighted_sum(n_chunks - 2, n_chunks % 2)
    weighted_sum(n_chunks - 1, (n_chunks - 1) % 2)
    o0 = acc_scr[0][:, :LANES] / acc_scr[0][:, LANES:]
    o1 = acc_scr[1][:, :LANES] / acc_scr[1][:, LANES:]
    if diff_mode:
        lp = lam_ref[...]
        lam = (jnp.exp(jnp.sum(lp[0:1] * lp[1:2], axis=-1, keepdims=True))
               - jnp.exp(jnp.sum(lp[2:3] * lp[3:4], axis=-1, keepdims=True)) + lambda_init)
        o = o0 - lam * o1
        ms = jnp.mean(o * o, axis=-1, keepdims=True)
        o = o * lax.rsqrt(ms + NORM_EPS) * subln_ref[...] * (1.0 - lambda_init)
    else:
        o = jnp.where(lane < LANES // 2, o0, o1)
    o_ref[...] = o.astype(o_ref.dtype)


def _attention(q_arr, q_blk, k_arr, k_blk, v_arr, v_blk, *, batch, n_q, n_k, groups, slice_mode=False, diff=None,
               sub_width=LANES // 2, tq=512, tk=768, rb=16, unroll=16, name="attn"):
    qw = 2 * LANES if slice_mode else LANES
    tq = min(tq, n_q)
    tk = min(tk, n_k)
    assert n_q % tq == 0 and n_k % tk == 0 and tk % LANES == 0 and tq % rb == 0
    nqt = n_q // tq
    in_specs = [pl.BlockSpec((tq, qw), lambda b, g, i: (b * nqt + i, q_blk + g)),
                pl.BlockSpec((n_k, qw), lambda b, g, i: (b, k_blk + g)),
                pl.BlockSpec((n_k, LANES), lambda b, g, i: (b, v_blk + g))]
    args = [q_arr, k_arr, v_arr]
    lambda_init = 0.0
    if diff is not None:
        lam_params, subln, lambda_init = diff
        in_specs.append(pl.BlockSpec((4, DA_HEAD_DIM), lambda b, g, i: (0, 0)))
        in_specs.append(pl.BlockSpec((1, LANES), lambda b, g, i: (0, 0)))
        args += [lam_params, subln.reshape(1, LANES)]
    assert unroll % 2 == 0
    kern = functools.partial(_attn_kernel, n_chunks=n_k // tk, tk=tk, rb=rb, unroll=unroll, slice_mode=slice_mode,
                             sub_width=sub_width, diff_mode=diff is not None, lambda_init=lambda_init)
    scratch = 2 * ([pltpu.VMEM((tq, LANES), BF16)] + 2 * [pltpu.VMEM((tq, tk), F32)] + 2 * [pltpu.VMEM((tq, tk), BF16)]
                   + [pltpu.VMEM((tq, LANES), F32)] + 2 * [pltpu.VMEM((tq, LANES), F32)]
                   + [pltpu.VMEM((tq, 2 * LANES), F32)])
    return pl.pallas_call(
        kern, grid=(batch, groups, nqt), in_specs=in_specs,
        out_specs=pl.BlockSpec((tq, LANES), lambda b, g, i: (b * nqt + i, g)),
        out_shape=jax.ShapeDtypeStruct((batch * n_q, groups * LANES), BF16), scratch_shapes=scratch,
        compiler_params=_cparams(("parallel", "parallel", "arbitrary")), name=name)(*args)


def _na_kernel(q_ref, kc_ref, vc_ref, kl_ref, vl_ref, bias_ref, o_ref, *, n_rows):
    i = pl.program_id(2)
    k0 = jnp.clip(i * NA_Q_ROWS - NA_WIN_ROWS // 2, 0, n_rows - NA_K_ROWS)
    start = pl.multiple_of(k0 * GRID_W, GRID_W * 4)
    nkw = NA_K_ROWS * GRID_W
    kw = kl_ref[pl.ds(start, nkw), :]
    vw = vl_ref[pl.ds(start, nkw), :]
    kc = kc_ref[...]
    vc = vc_ref[...]
    q = q_ref[...]
    lane = lax.broadcasted_iota(jnp.int32, (1, LANES), 1)
    outs = []
    for s in range(2):
        qs = jnp.where((lane < LANES // 2) if s == 0 else (lane >= LANES // 2), q, jnp.zeros_like(q))
        dn = (((1,), (1,)), ((), ()))
        s_win = lax.dot_general(qs, kw, dn, preferred_element_type=F32) + bias_ref[s]
        s_ctx = lax.dot_general(qs, kc, dn, preferred_element_type=F32)
        m = jnp.maximum(jnp.max(s_win, axis=-1, keepdims=True), jnp.max(s_ctx, axis=-1, keepdims=True))
        p_win = jnp.exp2(s_win - m)
        p_ctx = jnp.exp2(s_ctx - m)
        l = jnp.sum(p_win, axis=-1, keepdims=True) + jnp.sum(p_ctx, axis=-1, keepdims=True)
        o = (jnp.dot(p_win.astype(BF16), vw, preferred_element_type=F32)
             + jnp.dot(p_ctx.astype(BF16), vc, preferred_element_type=F32))
        outs.append(o / l)
    o_ref[...] = jnp.where(lane < LANES // 2, outs[0], outs[1]).astype(o_ref.dtype)


def _na_bias_tables(rpb, n_rows):
    nb = n_rows // NA_Q_ROWS
    assert nb >= 3
    dc = np.arange(GRID_W)[None, :] - np.arange(GRID_W)[:, None] + NA_WIN_COLS - 1
    onehot = (dc[None] == np.arange(2 * NA_WIN_COLS - 1)[:, None, None]).astype(np.float32)
    toep = jnp.einsum('hrd,dqk->hrqk', rpb.astype(F32), jnp.asarray(onehot), precision=lax.Precision.HIGHEST)
    cols = np.arange(GRID_W)
    col_start = np.clip(cols - NA_WIN_COLS // 2, 0, GRID_W - NA_WIN_COLS)
    col_ok = (cols[None, :] >= col_start[:, None]) & (cols[None, :] < col_start[:, None] + NA_WIN_COLS)
    ridx = np.zeros((3, NA_Q_ROWS, NA_K_ROWS), np.int32)
    ok = np.zeros((3, NA_Q_ROWS, NA_K_ROWS), bool)
    for var, blk in enumerate((0, 1, nb - 1)):
        r_first = blk * NA_Q_ROWS
        k_first = int(np.clip(r_first - NA_WIN_ROWS // 2, 0, n_rows - NA_K_ROWS))
        for a in range(NA_Q_ROWS):
            qr = r_first + a
            r0 = int(np.clip(qr - NA_WIN_ROWS // 2, 0, n_rows - NA_WIN_ROWS))
            for kb in range(NA_K_ROWS):
                kr = k_first + kb
                if r0 <= kr < r0 + NA_WIN_ROWS:
                    ok[var, a, kb] = True
                    ridx[var, a, kb] = kr - qr + NA_WIN_ROWS - 1
    tab = toep[:, ridx]
    tab = jnp.transpose(tab, (0, 1, 2, 4, 3, 5))
    mask = ok[:, :, None, :, None] & col_ok[None, None, :, None, :]
    tab = jnp.where(jnp.asarray(mask)[None], tab, MASK_VALUE)
    h = rpb.shape[0]
    return tab.reshape(h, 3, NA_Q_ROWS * GRID_W, NA_K_ROWS * GRID_W)


def _na_attention(qkv_lat, qkv_ctx, bias_tab, *, batch, n_lat, n_ctx):
    d = NA_HEADS * NA_HEAD_DIM
    groups = d // LANES
    n_rows = n_lat // GRID_W
    nb = n_rows // NA_Q_ROWS
    tq = NA_Q_ROWS * GRID_W

    def var_of(i):
        return jnp.where(i == 0, 0, jnp.where(i == nb - 1, 2, 1))

    in_specs = [
        pl.BlockSpec((tq, LANES), lambda b, g, i: (b * nb + i, g)),
        pl.BlockSpec((n_ctx, LANES), lambda b, g, i: (b, groups + g)),
        pl.BlockSpec((n_ctx, LANES), lambda b, g, i: (b, 2 * groups + g)),
        pl.BlockSpec((n_lat, LANES), lambda b, g, i: (b, groups + g)),
        pl.BlockSpec((n_lat, LANES), lambda b, g, i: (b, 2 * groups + g)),
        pl.BlockSpec((2, None, tq, NA_K_ROWS * GRID_W), lambda b, g, i: (g, var_of(i), 0, 0)),
    ]
    return pl.pallas_call(
        functools.partial(_na_kernel, n_rows=n_rows), grid=(batch, groups, nb), in_specs=in_specs,
        out_specs=pl.BlockSpec((tq, LANES), lambda b, g, i: (b * nb + i, g)),
        out_shape=jax.ShapeDtypeStruct((batch * n_lat, d), BF16),
        compiler_params=_cparams(("parallel", "parallel", "arbitrary")), name="na_attn",
    )(qkv_lat, qkv_ctx, qkv_ctx, qkv_lat, qkv_lat, bias_tab)


def _dft_kernel(c_ref, ms_ref, ab_ref, o_ref, acc_ref, *, scale):
    n = pl.program_id(2)
    d = o_ref.shape[1]
    part = (jnp.dot(c_ref[...], ab_ref[:, :d], preferred_element_type=F32)
            + jnp.dot(ms_ref[...], ab_ref[:, d:], preferred_element_type=F32))

    @pl.when(n == 0)
    def _():
        acc_ref[...] = part

    @pl.when(n > 0)
    def _():
        acc_ref[...] += part

    @pl.when(n == pl.num_programs(2) - 1)
    def _():
        o_ref[...] = (acc_ref[...] * scale).astype(o_ref.dtype)


def _pos_dft(cos_t, msin_t, ab, *, batch, n_tok, d, scale, name):
    t = min(1024, n_tok)
    nt = n_tok // t
    return pl.pallas_call(
        functools.partial(_dft_kernel, scale=scale), grid=(batch, nt, nt),
        in_specs=[pl.BlockSpec((t, t), lambda b, k, n: (k, n)),
                  pl.BlockSpec((t, t), lambda b, k, n: (k, n)),
                  pl.BlockSpec((t, 2 * d), lambda b, k, n: (b * nt + n, 0))],
        out_specs=pl.BlockSpec((t, d), lambda b, k, n: (b * nt + k, 0)),
        out_shape=jax.ShapeDtypeStruct((batch * n_tok, d), BF16),
        scratch_shapes=[pltpu.VMEM((t, d), F32)],
        compiler_params=_cparams(("parallel", "parallel", "arbitrary")), name=name)(cos_t, msin_t, ab)


def _dft_tables(n):
    split = 64 if n % 64 == 0 and n > 64 else 1
    m = jnp.arange(n, dtype=jnp.int32)[None, :]

    def phase(kvals):
        ang = ((kvals[:, None] * m) % n).astype(F32) * (2.0 * math.pi / n)
        return jnp.cos(ang), jnp.sin(ang)

    c1, s1 = phase(jnp.arange(n // split, dtype=jnp.int32) * split)
    c2, s2 = phase(jnp.arange(split, dtype=jnp.int32))
    c = c1[:, None, :] * c2[None, :, :] - s1[:, None, :] * s2[None, :, :]
    s = s1[:, None, :] * c2[None, :, :] + c1[:, None, :] * s2[None, :, :]
    return c.reshape(n, n).astype(BF16), (-s).reshape(n, n).astype(BF16)


def _channel_dft_weight(d):
    cg = d // FN_GROUPS
    idx = np.arange(cg)
    ang = 2.0 * np.pi * ((idx[:, None] * idx[None, :]) % cg) / cg
    eye = np.eye(FN_GROUPS)
    return jnp.asarray(np.concatenate([np.kron(eye, np.cos(ang)), np.kron(eye, np.sin(ang))], axis=1), F32)


def _router_kernel(x_ref, gain_ref, sc_ref, sh_ref, wr_ref, h_ref, aff_ref):
    x = x_ref[...]
    ms = jnp.mean(x * x, axis=-1, keepdims=True)
    h = x * lax.rsqrt(ms + NORM_EPS) * gain_ref[...]
    h = h * (1.0 + sc_ref[...]) + sh_ref[...]
    h_ref[...] = h.astype(h_ref.dtype)
    w = wr_ref[...]
    h_hi, w_hi = h.astype(BF16), w.astype(BF16)
    h_lo, w_lo = (h - h_hi.astype(F32)).astype(BF16), (w - w_hi.astype(F32)).astype(BF16)
    logits = (jnp.dot(h_hi, w_hi, preferred_element_type=F32) + jnp.dot(h_lo, w_hi, preferred_element_type=F32)
              + jnp.dot(h_hi, w_lo, preferred_element_type=F32))
    e = jnp.exp(logits - jnp.max(logits, axis=-1, keepdims=True))
    aff_ref[...] = e / jnp.sum(e, axis=-1, keepdims=True)


def _norm_router(x, gain, sc, sh, w_router, layer, *, rows_per_batch, tm=512):
    m, d = x.shape
    e = w_router.shape[-1]
    tm = min(tm, rows_per_batch)
    tpb = rows_per_batch // tm
    return pl.pallas_call(
        _router_kernel, grid=(m // tm,),
        in_specs=[pl.BlockSpec((tm, d), lambda i: (i, 0)),
                  pl.BlockSpec((1, d), lambda i: (0, 0)),
                  pl.BlockSpec((None, 1, d), lambda i: (i // tpb, 0, 0)),
                  pl.BlockSpec((None, 1, d), lambda i: (i // tpb, 0, 0)),
                  pl.BlockSpec((None, d, e), lambda i: (layer, 0, 0))],
        out_specs=[pl.BlockSpec((tm, d), lambda i: (i, 0)), pl.BlockSpec((tm, e), lambda i: (i, 0))],
        out_shape=[jax.ShapeDtypeStruct((m, d), BF16), jax.ShapeDtypeStruct((m, e), F32)],
        compiler_params=_cparams(("parallel",)), name="norm_router")(x, gain.reshape(1, d), sc, sh, w_router)


FFN_PIECE = 256


def _ffn_kernel(*refs, n_streams):
    x_refs = refs[:n_streams]
    wg_ref, wu_ref, wd_ref = refs[n_streams:n_streams + 3]
    gate_refs = refs[n_streams + 3:2 * n_streams + 3]
    o_refs = refs[2 * n_streams + 3:3 * n_streams + 3]
    acc_ref = refs[3 * n_streams + 3]
    f = pl.program_id(2)
    x = x_refs[0][...] if n_streams == 1 else jnp.concatenate([r[...] for r in x_refs], axis=0)
    @pl.when(f == 0)
    def _():
        acc_ref[...] = jnp.zeros(acc_ref.shape, F32)

    tf = wg_ref.shape[1]
    acts = []
    for c0 in range(0, tf, FFN_PIECE):
        a = jnp.dot(x, wg_ref[:, c0:c0 + FFN_PIECE].astype(BF16), preferred_element_type=F32)
        u = jnp.dot(x, wu_ref[:, c0:c0 + FFN_PIECE].astype(BF16), preferred_element_type=F32)
        acts.append(((a * (1.0 / (1.0 + jnp.exp(-a)))) * u).astype(BF16))
    d = wd_ref.shape[1]
    for n0 in range(0, d, FFN_PIECE):
        y = None
        for i, act in enumerate(acts):
            part = jnp.dot(act, wd_ref[i * FFN_PIECE:(i + 1) * FFN_PIECE, n0:n0 + FFN_PIECE].astype(BF16),
                           preferred_element_type=F32)
            y = part if y is None else y + part
        acc_ref[:, n0:n0 + FFN_PIECE] += y

    @pl.when(f == pl.num_programs(2) - 1)
    def _():
        row = 0
        for o_ref, g_ref in zip(o_refs, gate_refs):
            c = o_ref.shape[0]
            o_ref[...] = acc_ref[row:row + c, :] * g_ref[...]
            row += c


def _moe_ffn(xs, gates, w_gate, w_up, w_down, layer, *, tf=512):
    b, e, _, d = xs[0].shape
    f = w_gate.shape[-1]
    tf = min(tf, f)
    slot = lambda c, w: pl.BlockSpec((None, None, c, w), lambda ei, bi, fi: (bi, ei, 0, 0))
    caps = [t.shape[2] for t in xs]
    in_specs = ([slot(c, d) for c in caps]
                + [pl.BlockSpec((None, None, d, tf), lambda ei, bi, fi: (layer, ei, 0, fi)),
                   pl.BlockSpec((None, None, d, tf), lambda ei, bi, fi: (layer, ei, 0, fi)),
                   pl.BlockSpec((None, None, tf, d), lambda ei, bi, fi: (layer, ei, fi, 0))]
                + [slot(c, 1) for c in caps])
    return pl.pallas_call(
        functools.partial(_ffn_kernel, n_streams=len(xs)), grid=(e, b, f // tf), in_specs=in_specs,
        out_specs=[slot(c, d) for c in caps],
        out_shape=[jax.ShapeDtypeStruct((b, e, c, d), F32) for c in caps],
        scratch_shapes=[pltpu.VMEM((sum(caps), d), F32)],
        compiler_params=_cparams(("parallel", "parallel", "arbitrary")), name="moe_ffn",
    )(*xs, w_gate, w_up, w_down, *gates)


def _route(aff, h, batch, n_tok):
    e = aff.shape[-1]
    cap = EC_CAPACITY_FACTOR * n_tok // e
    aff_t = jnp.swapaxes(aff.reshape(batch, n_tok, e), 1, 2)
    gate, idx = lax.top_k(aff_t, cap)
    thr = gate[..., -1:]
    above, tied = aff_t > thr, aff_t == thr
    n_tied = cap - jnp.sum(above, axis=-1, keepdims=True, dtype=jnp.int32)
    chosen = above | (tied & (jnp.cumsum(tied, axis=-1, dtype=jnp.int32) <= n_tied))
    pos = jnp.where(chosen, jnp.cumsum(chosen, axis=-1, dtype=jnp.int32) - 1, -1)
    pos = jnp.swapaxes(pos, 1, 2).reshape(batch * n_tok, e)
    idx, gate = lax.sort((idx, gate), dimension=2, num_keys=1)
    xs = jax.vmap(lambda hb, ib: hb[ib])(h.reshape(batch, n_tok, -1), idx)
    return gate, idx, xs, pos


COMBINE_TOKENS = 512
COMBINE_WINDOW = 128


def _combine_kernel(lo_ref, nr_ref, x_ref, g_ref, pos_ref, y_hbm, o_ref, stage, sem, acc, *, n_tiles, n_total, n_exp,
                    cap):
    w = COMBINE_WINDOW
    tile = pl.program_id(0) * n_tiles + pl.program_id(1)
    slot = tile % 2
    lane = lax.broadcasted_iota(jnp.int32, (1, w), 1)

    def window(tl, e, r):
        first = lo_ref[tl * n_exp + e] + r * w
        return first, pl.multiple_of(jnp.minimum(first, cap - w), 8)

    def copy(tl, e, start, sl):
        return pltpu.make_async_copy(y_hbm.at[tl // n_tiles, e, pl.ds(start, w), :],
                                     stage.at[sl, pl.ds(e * w, w), :], sem.at[sl, e])

    def fetch(tl, r, sl):
        for e in range(n_exp):
            copy(tl, e, window(tl, e, r)[1], sl).start()

    def add_round(r):
        pos = pos_ref[...]
        blocks = []
        for e in range(n_exp):
            first, start = window(tile, e, r)
            col = pos[:, e:e + 1]
            hit = jnp.logical_and(col - start == lane, col >= first)
            blocks.append(jnp.where(hit, 1.0, 0.0).astype(BF16))
        onehot = jnp.concatenate(blocks, axis=1)
        for e in range(n_exp):
            copy(tile, e, 0, slot).wait()
        y = stage[slot]
        y_hi = y.astype(BF16)
        y_lo = (y - y_hi.astype(F32)).astype(BF16)
        acc[...] += (jnp.dot(onehot, y_hi, preferred_element_type=F32)
                     + jnp.dot(onehot, y_lo, preferred_element_type=F32))

    @pl.when(tile == 0)
    def _():
        fetch(tile, 0, slot)

    @pl.when(tile + 1 < n_total)
    def _():
        fetch(tile + 1, 0, 1 - slot)

    acc[...] = jnp.zeros(acc.shape, F32)
    add_round(0)

    @pl.loop(1, nr_ref[tile])
    def _(r):
        fetch(tile, r, slot)
        add_round(r)

    o_ref[...] = x_ref[...] + g_ref[...] * acc[...]


def _combine_lat(x, g2, y, idx, pos, batch, n_tok):
    d = x.shape[-1]
    _, n_exp, cap = idx.shape
    tt, w = COMBINE_TOKENS, COMBINE_WINDOW
    assert n_tok % tt == 0 and cap % 8 == 0 and cap >= w
    nt = n_tok // tt
    bounds = jnp.arange(nt + 1, dtype=jnp.int32) * tt
    below = jnp.sum((idx[..., None] < bounds).astype(jnp.int32), axis=2)
    lo = (below[..., :-1] // 8) * 8
    rounds = jnp.maximum(jnp.max((below[..., 1:] - lo + w - 1) // w, axis=1), 1)
    lo_flat = jnp.transpose(lo, (0, 2, 1)).reshape(-1)
    grid_spec = pltpu.PrefetchScalarGridSpec(
        num_scalar_prefetch=2, grid=(batch, nt),
        in_specs=[pl.BlockSpec((tt, d), lambda b, t, lo_r, nr_r: (b * nt + t, 0)),
                  pl.BlockSpec((None, 1, d), lambda b, t, lo_r, nr_r: (b, 0, 0)),
                  pl.BlockSpec((tt, n_exp), lambda b, t, lo_r, nr_r: (b * nt + t, 0)),
                  pl.BlockSpec(memory_space=pl.ANY)],
        out_specs=pl.BlockSpec((tt, d), lambda b, t, lo_r, nr_r: (b * nt + t, 0)),
        scratch_shapes=[pltpu.VMEM((2, n_exp * w, d), F32), pltpu.SemaphoreType.DMA((2, n_exp)),
                        pltpu.VMEM((tt, d), F32)])
    return pl.pallas_call(
        functools.partial(_combine_kernel, n_tiles=nt, n_total=batch * nt, n_exp=n_exp, cap=cap),
        grid_spec=grid_spec, out_shape=jax.ShapeDtypeStruct(x.shape, F32),
        compiler_params=_cparams(("arbitrary", "arbitrary")), name="moe_combine",
    )(lo_flat, rounds.reshape(-1), x, g2, pos, y)


def _combine(x, g2, y, idx, batch, n_tok):
    d = x.shape[-1]
    upd = jax.vmap(lambda ib, yb: jnp.zeros((n_tok, d), F32).at[ib.reshape(-1)].add(yb.reshape(-1, d)))(idx, y)
    return (x.reshape(batch, n_tok, d) + g2 * upd).reshape(batch * n_tok, d)


def _final_norm_kernel(x_ref, g_ref, o_ref):
    x = x_ref[...]
    ms = jnp.mean(x * x, axis=-1, keepdims=True)
    o_ref[...] = x * lax.rsqrt(ms + NORM_EPS) * g_ref[...]


def _final_norm(x, g, tm=1024):
    m, d = x.shape
    tm = min(tm, m)
    return pl.pallas_call(
        _final_norm_kernel, grid=(m // tm,),
        in_specs=[pl.BlockSpec((tm, d), lambda i: (i, 0)), pl.BlockSpec((1, d), lambda i: (0, 0))],
        out_specs=pl.BlockSpec((tm, d), lambda i: (i, 0)), out_shape=jax.ShapeDtypeStruct((m, d), F32),
        compiler_params=_cparams(("parallel",)), name="final_norm")(x, g.reshape(1, d))


def _rope_tables(n_tok, rot_dim, pair_of_lane):
    t = jnp.arange(n_tok)
    rows = (t // GRID_W).astype(F32)
    cols = (t % GRID_W).astype(F32)
    n_freq = rot_dim // 4
    inv_freq = ROPE_BASE ** (-jnp.arange(n_freq, dtype=F32) / n_freq)
    ang = jnp.concatenate([rows[:, None] * inv_freq, cols[:, None] * inv_freq], axis=-1)
    cos, sin = jnp.cos(ang), jnp.sin(ang)
    pair = np.concatenate([pair_of_lane, pair_of_lane])
    used = jnp.asarray(pair >= 0)[None]
    sign = jnp.asarray(np.where(np.arange(LANES) < LANES // 2, -1.0, 1.0).astype(np.float32))[None]
    cos_l = jnp.where(used, cos[:, np.maximum(pair, 0)], 1.0)
    sin_l = jnp.where(used, sin[:, np.maximum(pair, 0)] * sign, 0.0)
    return cos_l, sin_l


def kernel(x, c, ctx, c_ctx, ada_w, ada_b, norm_mix, norm_ffn, norm_final, da_w_qkv, da_w_o, da_lambda_q1,
           da_lambda_k1, da_lambda_q2, da_lambda_k2, da_subln, fn_w_o, na_w_qkv, na_w_o, na_rpb, mla_w_dq,
           mla_q_norm, mla_w_uq, mla_w_dkv, mla_kv_norm, mla_w_uk, mla_w_uv, mla_w_o, moe_w_router, moe_w_gate,
           moe_w_up, moe_w_down):
    batch, n_lat, d = x.shape
    n_ctx = ctx.shape[1]
    depth = ada_w.shape[0]
    n_mixers = 4
    xl = x.reshape(batch * n_lat, d)
    xc = ctx.reshape(batch * n_ctx, d)

    cond = jnp.concatenate([c, c_ctx[None], jnp.zeros((8 - batch - 1, d), F32)], axis=0)
    mods = [
        _fused_mm(cond, ada_w, w_index=i, bias=ada_b[i], silu_in=True, precision=lax.Precision.HIGHEST, tn=1024,
                  name="ada_mod")
        for i in range(depth)
    ]

    def chunks(i):
        lat = [mods[i][:batch, k * d:(k + 1) * d].reshape(batch, 1, d) for k in range(ADA_CHUNKS)]
        cx = [mods[i][batch:batch + 1, k * d:(k + 1) * d].reshape(1, 1, d) for k in range(ADA_CHUNKS)]
        return lat, cx

    def all_keys(t_ctx, t_lat):
        w = t_ctx.shape[-1]
        both = jnp.concatenate([t_ctx.reshape(batch, n_ctx, w), t_lat.reshape(batch, n_lat, w)], axis=1)
        return both.reshape(batch * (n_ctx + n_lat), w)

    for i in range(depth):
        kind, j = i % n_mixers, i // n_mixers
        keep_ctx = i < depth - 1
        (sh1, sc1, g1, sh2, sc2, g2), (csh1, csc1, cg1, csh2, csc2, cg2) = chunks(i)
        lat_in = dict(gain=norm_mix[i], mod=(sc1, sh1), rows_per_batch=n_lat)
        ctx_in = dict(gain=norm_mix[i], mod=(csc1, csh1), rows_per_batch=batch * n_ctx, tm=n_ctx)

        if kind == 0:
            lambda_init = 0.8 - 0.6 * math.exp(-0.3 * i)
            scale = jnp.concatenate([jnp.full((d,), LOG2E * DA_HEAD_DIM ** -0.5, F32), jnp.ones((2 * d,), F32)])
            groups = d // LANES
            half = DA_HEAD_DIM // 2
            w = da_w_qkv[j]

            def reorder(cols):
                return cols.reshape(d, groups, 2, 2, half).transpose(0, 1, 3, 2, 4).reshape(d, d)

            w = jnp.concatenate([reorder(w[:, :d]), reorder(w[:, d:2 * d]), w[:, 2 * d:]], axis=1)
            rope = _rope_tables(n_lat, DA_HEAD_DIM, np.arange(LANES // 2) % half)
            qkv_l = _fused_mm(xl, w, colscale=scale, rope=rope, rope_cols=2 * d, out_dtype=BF16, name="da_qkv_lat",
                              **lat_in)
            qkv_c = _fused_mm(xc, w, colscale=scale, out_dtype=BF16, name="da_qkv_ctx", **ctx_in)
            lam_params = jnp.stack([da_lambda_q1[j], da_lambda_k1[j], da_lambda_q2[j], da_lambda_k2[j]])
            diff = (lam_params, da_subln[j], lambda_init)
            kv = all_keys(qkv_c[:, d:], qkv_l[:, d:])
            o_l = _attention(qkv_l, 0, kv, 0, kv, groups, batch=batch, n_q=n_lat, n_k=n_ctx + n_lat, groups=groups,
                             sub_width=half, diff=diff, name="da_attn_lat")
            o_c = _attention(qkv_c, 0, qkv_c, groups, qkv_c, 2 * groups, batch=batch, n_q=n_ctx, n_k=n_ctx,
                             groups=groups, sub_width=half, diff=diff, name="da_attn_ctx") if keep_ctx else None
            w_o = da_w_o
        elif kind == 1:
            w_cd = _channel_dft_weight(d)
            ab_l = _fused_mm(xl, w_cd, out_dtype=BF16, name="fn_chan_lat", **lat_in)
            scale_l = 1.0 / math.sqrt(n_lat * (d // FN_GROUPS))
            o_l = _pos_dft(*_dft_tables(n_lat), ab_l, batch=batch, n_tok=n_lat, d=d, scale=scale_l, name="fn_pos_lat")
            o_c = None
            if keep_ctx:
                ab_c = _fused_mm(xc, w_cd, out_dtype=BF16, name="fn_chan_ctx", **ctx_in)
                scale_c = 1.0 / math.sqrt(n_ctx * (d // FN_GROUPS))
                o_c = _pos_dft(*_dft_tables(n_ctx), ab_c, batch=batch, n_tok=n_ctx, d=d, scale=scale_c,
                               name="fn_pos_ctx")
            w_o = fn_w_o
        elif kind == 2:
            scale = jnp.concatenate([jnp.full((d,), LOG2E * NA_HEAD_DIM ** -0.5, F32), jnp.ones((2 * d,), F32)])
            qkv_l = _fused_mm(xl, na_w_qkv, w_index=j, colscale=scale, out_dtype=BF16, name="na_qkv_lat", **lat_in)
            qkv_c = _fused_mm(xc, na_w_qkv, w_index=j, colscale=scale, out_dtype=BF16, name="na_qkv_ctx", **ctx_in)
            groups = d // LANES
            bias_tab = _na_bias_tables(na_rpb[j] * LOG2E, n_lat // GRID_W)
            o_l = _na_attention(qkv_l, qkv_c, bias_tab, batch=batch, n_lat=n_lat, n_ctx=n_ctx)
            o_c = _attention(qkv_c, 0, qkv_c, groups, qkv_c, 2 * groups, batch=batch, n_q=n_ctx, n_k=n_ctx,
                             groups=groups, name="na_attn_ctx") if keep_ctx else None
            w_o = na_w_o
        else:
            q_rank = mla_w_dq.shape[-1]
            kv_rank = mla_w_uk.shape[-2]
            hq = MLA_NOPE_DIM + MLA_ROPE_DIM
            down_w = 3 * 384
            assert q_rank == 768 and kv_rank == 256
            w_down = jnp.concatenate([mla_w_dq[j], mla_w_dkv[j], jnp.zeros((d, down_w - q_rank - kv_rank - MLA_ROPE_DIM), F32)],
                                     axis=1)
            rh = MLA_ROPE_DIM // 2
            n_a = LANES // 2 - rh

            def head_lanes(nope, rope_part):
                z = jnp.zeros(nope.shape[:-1] + (LANES - hq,), F32)
                return jnp.concatenate([rope_part[..., :rh], nope[..., :n_a], rope_part[..., rh:], nope[..., n_a:], z],
                                       axis=-1)

            wq = mla_w_uq[j].reshape(q_rank, MLA_HEADS, hq)
            wq = head_lanes(wq[..., :MLA_NOPE_DIM], wq[..., MLA_NOPE_DIM:]).reshape(q_rank, MLA_HEADS * LANES)
            wk_nope = head_lanes(mla_w_uk[j].reshape(kv_rank, MLA_HEADS, MLA_NOPE_DIM),
                                 jnp.zeros((kv_rank, MLA_HEADS, MLA_ROPE_DIM), F32))
            eye = jnp.broadcast_to(jnp.eye(MLA_ROPE_DIM, dtype=F32)[:, None, :], (MLA_ROPE_DIM, MLA_HEADS, MLA_ROPE_DIM))
            wk_rope = head_lanes(jnp.zeros((MLA_ROPE_DIM, MLA_HEADS, MLA_NOPE_DIM), F32), eye)
            kw = MLA_HEADS * LANES
            vw = MLA_HEADS * MLA_V_DIM
            pad_rows = 384 - kv_rank - MLA_ROPE_DIM
            w_kv = jnp.concatenate([
                jnp.concatenate([wk_nope.reshape(kv_rank, kw), mla_w_uv[j]], axis=1),
                jnp.concatenate([wk_rope.reshape(MLA_ROPE_DIM, kw), jnp.zeros((MLA_ROPE_DIM, vw), F32)], axis=1),
                jnp.zeros((pad_rows, kw + vw), F32)], axis=0)
            kv_gain = jnp.concatenate([mla_kv_norm[j], jnp.ones((384 - kv_rank,), F32)])
            rope = _rope_tables(n_lat, MLA_ROPE_DIM, np.where(np.arange(LANES // 2) < rh, np.arange(LANES // 2), -1))
            qscale = jnp.full((kw,), LOG2E * hq ** -0.5, F32)

            def project(xs, inp, n_tok, rope_tabs, tag):
                low = _fused_mm(xs, w_down, tn=384, name="mla_down_" + tag, **inp)
                rp = dict(rope=rope_tabs) if rope_tabs is not None else {}
                tmx = inp.get("tm", 1024)
                q = _fused_mm(low, wq, a_cols=(0, q_rank), gain=mla_q_norm[j], colscale=qscale, rope_cols=kw,
                              rows_per_batch=inp["rows_per_batch"], tm=tmx, out_dtype=BF16, name="mla_q_" + tag, **rp)
                kv = _fused_mm(low, w_kv, a_cols=(q_rank, 384), gain=kv_gain, norm_cols=kv_rank, rope_cols=kw,
                               rows_per_batch=inp["rows_per_batch"], tm=tmx, out_dtype=BF16, name="mla_kv_" + tag, **rp)
                return q, kv

            q_l, kv_l = project(xl, lat_in, n_lat, rope, "lat")
            q_c, kv_c = project(xc, ctx_in, n_ctx, None, "ctx")
            groups = MLA_HEADS // 2
            kv = all_keys(kv_c, kv_l)
            o_l = _attention(q_l, 0, kv, 0, kv, 2 * groups, batch=batch, n_q=n_lat, n_k=n_ctx + n_lat, groups=groups,
                             slice_mode=True, name="mla_attn_lat")
            o_c = _attention(q_c, 0, kv_c, 0, kv_c, 2 * groups, batch=batch, n_q=n_ctx, n_k=n_ctx, groups=groups,
                             slice_mode=True, name="mla_attn_ctx") if keep_ctx else None
            w_o = mla_w_o

        xl = _fused_mm(o_l, w_o, w_index=j, res=xl, gate=g1, rows_per_batch=n_lat, name="mix_out_lat")
        if keep_ctx:
            xc = _fused_mm(o_c, w_o, w_index=j, res=xc, gate=jnp.broadcast_to(cg1, (1, 1, d)),
                           rows_per_batch=batch * n_ctx, tm=n_ctx, name="mix_out_ctx")

        h_l, aff_l = _norm_router(xl, norm_ffn[i], sc2, sh2, moe_w_router, i, rows_per_batch=n_lat)
        gate_l, idx_l, xs_l, pos_l = _route(aff_l, h_l, batch, n_lat)
        xs, gates = [xs_l], [gate_l[..., None]]
        if keep_ctx:
            h_c, aff_c = _norm_router(xc, norm_ffn[i], csc2, csh2, moe_w_router, i, rows_per_batch=batch * n_ctx,
                                      tm=n_ctx)
            gate_c, idx_c, xs_c, _ = _route(aff_c, h_c, batch, n_ctx)
            xs.append(xs_c)
            gates.append(gate_c[..., None])
        ys = _moe_ffn(xs, gates, moe_w_gate, moe_w_up, moe_w_down, i)
        xl = _combine_lat(xl, g2, ys[0], idx_l, pos_l, batch, n_lat)
        if keep_ctx:
            xc = _combine(xc, cg2, ys[1], idx_c, batch, n_ctx)

    return _final_norm(xl, norm_final).reshape(batch, n_lat, d)
```

```python
import functools
import math

import numpy as np
import jax
import jax.numpy as jnp
from jax import lax
from jax.experimental import pallas as pl
from jax.experimental.pallas import tpu as pltpu

F32 = jnp.float32
BF16 = jnp.bfloat16

LANES = 128
VMEM_LIMIT_BYTES = 56 * 1024 * 1024

GRID_W = 64
ROPE_BASE = 10000.0
NORM_EPS = 1e-6
ADA_CHUNKS = 6
DA_HEAD_DIM = 64
NA_HEADS = 16
NA_HEAD_DIM = 64
NA_WIN_ROWS = 8
NA_WIN_COLS = 16
NA_Q_ROWS = 8
NA_K_ROWS = 16
MLA_HEADS = 16
MLA_NOPE_DIM = 64
MLA_ROPE_DIM = 32
MLA_V_DIM = 64
FN_GROUPS = 4
EC_CAPACITY_FACTOR = 2
MASK_VALUE = -1e30
LOG2E = math.log2(math.e)


def _cparams(sem):
    return pltpu.CompilerParams(dimension_semantics=sem, vmem_limit_bytes=VMEM_LIMIT_BYTES)


def _rope_lanes(acc, cos, sin):
    outs = []
    for g in range(acc.shape[1] // LANES):
        blk = acc[:, g * LANES:(g + 1) * LANES]
        outs.append(blk * cos + pltpu.roll(blk, LANES // 2, 1) * sin)
    return outs[0] if len(outs) == 1 else jnp.concatenate(outs, axis=1)


def _mm_kernel(*refs, norm_cols, has_gain, has_mod, has_colscale, rope_tiles, has_bias, has_res,
               silu_in, precision):
    it = iter(refs)
    a_ref = next(it)
    w_ref = next(it)
    gain_ref = next(it) if has_gain else None
    sc_ref, sh_ref = (next(it), next(it)) if has_mod else (None, None)
    cs_ref = next(it) if has_colscale else None
    cos_ref, sin_ref = (next(it), next(it)) if rope_tiles else (None, None)
    bias_ref = next(it) if has_bias else None
    res_ref, gate_ref = (next(it), next(it)) if has_res else (None, None)
    o_ref = next(it)
    h_scr = next(it, None)
    j = pl.program_id(1)
    hi = precision is not None

    def prologue():
        a = a_ref[...].astype(F32)
        if silu_in:
            a = a * (1.0 / (1.0 + jnp.exp(-a)))
        if has_gain:
            k = a.shape[1]
            if norm_cols < k:
                col = lax.broadcasted_iota(jnp.int32, (1, k), 1)
                sq = jnp.where(col < norm_cols, a * a, 0.0)
            else:
                sq = a * a
            ms = jnp.sum(sq, axis=-1, keepdims=True) * (1.0 / norm_cols)
            y = a * lax.rsqrt(ms + NORM_EPS) * gain_ref[...]
            a = jnp.where(col < norm_cols, y, a) if norm_cols < k else y
        if has_mod:
            a = a * (1.0 + sc_ref[...]) + sh_ref[...]
        return a if hi else a.astype(BF16)

    if h_scr is not None:
        @pl.when(j == 0)
        def _():
            h_scr[...] = prologue()
        h = h_scr[...]
    else:
        h = a_ref[...] if hi else a_ref[...].astype(BF16)
    w = w_ref[...] if hi else w_ref[...].astype(BF16)
    acc = jnp.dot(h, w, preferred_element_type=F32, precision=precision)
    if has_colscale:
        acc = acc * cs_ref[...]

    def finish(v):
        if has_bias:
            v = v + bias_ref[...]
        if has_res:
            v = res_ref[...] + gate_ref[...] * v
        o_ref[...] = v.astype(o_ref.dtype)

    if rope_tiles:
        @pl.when(j < rope_tiles)
        def _():
            finish(_rope_lanes(acc, cos_ref[...], sin_ref[...]))

        @pl.when(j >= rope_tiles)
        def _():
            finish(acc)
    else:
        finish(acc)


def _fused_mm(a, w, *, w_index=None, a_cols=None, gain=None, norm_cols=None, mod=None, colscale=None, rope=None,
              rope_cols=0, bias=None, res=None, gate=None, silu_in=False, rows_per_batch=None,
              out_dtype=F32, tm=1024, tn=1024, precision=None, name="mm"):
    m = a.shape[0]
    k0, k = a_cols if a_cols is not None else (0, a.shape[1])
    if precision is None and w.dtype != BF16:
        w = (w[w_index] if w.ndim == 3 else w).astype(BF16)
    n = w.shape[-1]
    assert w.shape[-2] == k and k0 % k == 0
    tm = min(tm, m)
    tn = min(tn, n)
    assert m % tm == 0 and n % tn == 0
    rows_per_batch = rows_per_batch or m
    assert rows_per_batch % tm == 0
    tpb = rows_per_batch // tm
    has_prologue = gain is not None or mod is not None or silu_in
    in_specs = [pl.BlockSpec((tm, k), lambda i, j: (i, k0 // k))]
    args = [a]
    if w.ndim == 3:
        in_specs.append(pl.BlockSpec((None, k, tn), lambda i, j: (w_index, 0, j)))
    else:
        in_specs.append(pl.BlockSpec((k, tn), lambda i, j: (0, j)))
    args.append(w)
    if gain is not None:
        in_specs.append(pl.BlockSpec((1, k), lambda i, j: (0, 0)))
        args.append(gain.reshape(1, k).astype(F32))
    if mod is not None:
        for t in mod:
            in_specs.append(pl.BlockSpec((None, 1, k), lambda i, j: (i // tpb, 0, 0)))
            args.append(t)
    if colscale is not None:
        in_specs.append(pl.BlockSpec((1, tn), lambda i, j: (0, j)))
        args.append(colscale.reshape(1, n).astype(F32))
    rope_tiles = 0
    if rope is not None:
        assert rope_cols % tn == 0
        rope_tiles = rope_cols // tn
        for t in rope:
            in_specs.append(pl.BlockSpec((tm, LANES), lambda i, j: (i % tpb, 0)))
            args.append(t)
    if bias is not None:
        in_specs.append(pl.BlockSpec((1, tn), lambda i, j: (0, j)))
        args.append(bias.reshape(1, n).astype(F32))
    if res is not None:
        in_specs.append(pl.BlockSpec((tm, tn), lambda i, j: (i, j)))
        args.append(res)
        in_specs.append(pl.BlockSpec((None, 1, tn), lambda i, j: (i // tpb, 0, j)))
        args.append(gate)
    scratch = [pltpu.VMEM((tm, k), F32 if precision is not None else BF16)] if has_prologue else []
    kern = functools.partial(
        _mm_kernel, norm_cols=norm_cols or k, has_gain=gain is not None, has_mod=mod is not None,
        has_colscale=colscale is not None, rope_tiles=rope_tiles, has_bias=bias is not None,
        has_res=res is not None, silu_in=silu_in, precision=precision)
    return pl.pallas_call(
        kern, grid=(m // tm, n // tn), in_specs=in_specs,
        out_specs=pl.BlockSpec((tm, tn), lambda i, j: (i, j)),
        out_shape=jax.ShapeDtypeStruct((m, n), out_dtype), scratch_shapes=scratch,
        compiler_params=_cparams(("parallel", "arbitrary")), name=name)(*args)


def _attn_kernel(*refs, n_chunks, tk, rb, unroll, slice_mode, sub_width, diff_mode, lambda_init):
    q_ref, k_ref, v_ref = refs[:3]
    pos = 3
    if diff_mode:
        lam_ref, subln_ref = refs[pos], refs[pos + 1]
        pos += 2
    o_ref = refs[pos]
    per = [refs[pos + 1 + 9 * s:pos + 10 + 9 * s] for s in range(2)]
    qm_scr = [t[0] for t in per]
    s_scr = [t[1:3] for t in per]
    p_scr = [t[3:5] for t in per]
    m_scr = [t[5] for t in per]
    a_scr = [t[6:8] for t in per]
    acc_scr = [t[8] for t in per]
    tq = q_ref.shape[0]
    q = q_ref[...]
    lane = lax.broadcasted_iota(jnp.int32, (1, LANES), 1)
    for s in range(2):
        if slice_mode:
            qm_scr[s][...] = q[:, s * LANES:(s + 1) * LANES]
        else:
            qm_scr[s][...] = jnp.where((lane // sub_width) % 2 == s, q, jnp.zeros_like(q))
        m_scr[s][...] = jnp.full(m_scr[s].shape, MASK_VALUE, F32)
        acc_scr[s][...] = jnp.zeros(acc_scr[s].shape, F32)

    def chunk_rows(c):
        return pl.ds(c * tk if isinstance(c, int) else pl.multiple_of(c * tk, tk), tk)

    def scores(c, slot):
        k = k_ref[chunk_rows(c), :]
        for s in range(2):
            ks = k[:, s * LANES:(s + 1) * LANES] if slice_mode else k
            s_scr[s][slot][...] = lax.dot_general(qm_scr[s][...], ks, (((1,), (1,)), ((), ())),
                                                  preferred_element_type=F32)

    def softmax(slot):
        for s in range(2):
            for r in range(tq // rb):
                rows = slice(r * rb, (r + 1) * rb)
                sb = s_scr[s][slot][rows, :]
                m_old = m_scr[s][rows, :]
                m_new = jnp.maximum(m_old, jnp.max(sb, axis=-1, keepdims=True))
                m_scr[s][rows, :] = m_new
                a_scr[s][slot][rows, :] = jnp.exp2(m_old - m_new)
                p_scr[s][slot][rows, :] = jnp.exp2(sb - jnp.tile(m_new, (1, tk // LANES))).astype(BF16)

    def weighted_sum(c, slot):
        v = v_ref[chunk_rows(c), :]
        vext = jnp.concatenate([v, jnp.ones((tk, LANES), BF16)], axis=1)
        for s in range(2):
            alpha = a_scr[s][slot][...]
            pv = jnp.dot(p_scr[s][slot][...], vext, preferred_element_type=F32)
            acc_scr[s][...] = jnp.concatenate([alpha, alpha], axis=1) * acc_scr[s][...] + pv

    def step(c, slot):
        scores(c, slot)
        softmax(1 - slot)
        weighted_sum(c - 2, slot)

    scores(0, 0)
    if n_chunks > 1:
        scores(1, 1)
    softmax(0)
    if n_chunks > 2:
        n_loop = (n_chunks - 2) // unroll
        if n_loop:
            @pl.loop(0, n_loop)
            def _(j):
                for u in range(unroll):
                    step(unroll * j + 2 + u, u % 2)
        for c in range(2 + n_loop * unroll, n_chunks):
            step(c, c % 2)
    if n_chunks > 1:
        softmax((n_chunks - 1) % 2)
        weighted_sum(n_chunks - 2, n_chunks % 2)
    weighted_sum(n_chunks - 1, (n_chunks - 1) % 2)
    o0 = acc_scr[0][:, :LANES] / acc_scr[0][:, LANES:]
    o1 = acc_scr[1][:, :LANES] / acc_scr[1][:, LANES:]
    if diff_mode:
        lp = lam_ref[...]
        lam = (jnp.exp(jnp.sum(lp[0:1] * lp[1:2], axis=-1, keepdims=True))
               - jnp.exp(jnp.sum(lp[2:3] * lp[3:4], axis=-1, keepdims=True)) + lambda_init)
        o = o0 - lam * o1
        ms = jnp.mean(o * o, axis=-1, keepdims=True)
        o = o * lax.rsqrt(ms + NORM_EPS) * subln_ref[...] * (1.0 - lambda_init)
    else:
        o = jnp.where(lane < LANES // 2, o0, o1)
    o_ref[...] = o.astype(o_ref.dtype)


def _attention(q_arr, q_blk, k_arr, k_blk, v_arr, v_blk, *, batch, n_q, n_k, groups, slice_mode=False, diff=None,
               sub_width=LANES // 2, tq=512, tk=768, rb=16, unroll=16, name="attn"):
    qw = 2 * LANES if slice_mode else LANES
    tq = min(tq, n_q)
    tk = min(tk, n_k)
    assert n_q % tq == 0 and n_k % tk == 0 and tk % LANES == 0 and tq % rb == 0
    nqt = n_q // tq
    in_specs = [pl.BlockSpec((tq, qw), lambda b, g, i: (b * nqt + i, q_blk + g)),
                pl.BlockSpec((n_k, qw), lambda b, g, i: (b, k_blk + g)),
                pl.BlockSpec((n_k, LANES), lambda b, g, i: (b, v_blk + g))]
    args = [q_arr, k_arr, v_arr]
    lambda_init = 0.0
    if diff is not None:
        lam_params, subln, lambda_init = diff
        in_specs.append(pl.BlockSpec((4, DA_HEAD_DIM), lambda b, g, i: (0, 0)))
        in_specs.append(pl.BlockSpec((1, LANES), lambda b, g, i: (0, 0)))
        args += [lam_params, subln.reshape(1, LANES)]
    assert unroll % 2 == 0
    kern = functools.partial(_attn_kernel, n_chunks=n_k // tk, tk=tk, rb=rb, unroll=unroll, slice_mode=slice_mode,
                             sub_width=sub_width, diff_mode=diff is not None, lambda_init=lambda_init)
    scratch = 2 * ([pltpu.VMEM((tq, LANES), BF16)] + 2 * [pltpu.VMEM((tq, tk), F32)] + 2 * [pltpu.VMEM((tq, tk), BF16)]
                   + [pltpu.VMEM((tq, LANES), F32)] + 2 * [pltpu.VMEM((tq, LANES), F32)]
                   + [pltpu.VMEM((tq, 2 * LANES), F32)])
    return pl.pallas_call(
        kern, grid=(batch, groups, nqt), in_specs=in_specs,
        out_specs=pl.BlockSpec((tq, LANES), lambda b, g, i: (b * nqt + i, g)),
        out_shape=jax.ShapeDtypeStruct((batch * n_q, groups * LANES), BF16), scratch_shapes=scratch,
        compiler_params=_cparams(("parallel", "parallel", "arbitrary")), name=name)(*args)


def _na_kernel(q_ref, kc_ref, vc_ref, kl_ref, vl_ref, bias_ref, o_ref, s0_scr, s1_scr, p0_scr, p1_scr, *, n_rows, rb):
    i = pl.program_id(2)
    k0 = jnp.clip(i * NA_Q_ROWS - NA_WIN_ROWS // 2, 0, n_rows - NA_K_ROWS)
    start = pl.multiple_of(k0 * GRID_W, GRID_W * 4)
    nkw = NA_K_ROWS * GRID_W
    kw = kl_ref[pl.ds(start, nkw), :]
    kc = kc_ref[...]
    q = q_ref[...]
    tq = q.shape[0]
    n_all = nkw + kc.shape[0]
    lane = lax.broadcasted_iota(jnp.int32, (1, LANES), 1)
    s_scr, p_scr = (s0_scr, s1_scr), (p0_scr, p1_scr)
    dn = (((1,), (1,)), ((), ()))
    for s in range(2):
        qs = jnp.where((lane < LANES // 2) if s == 0 else (lane >= LANES // 2), q, jnp.zeros_like(q))
        s_scr[s][:, :nkw] = lax.dot_general(qs, kw, dn, preferred_element_type=F32) + bias_ref[s]
        s_scr[s][:, nkw:] = lax.dot_general(qs, kc, dn, preferred_element_type=F32)
    for s in range(2):
        for r in range(tq // rb):
            rows = slice(r * rb, (r + 1) * rb)
            sb = s_scr[s][rows, :]
            p_scr[s][rows, :] = jnp.exp2(sb - jnp.max(sb, axis=-1, keepdims=True)).astype(BF16)
    v_all = jnp.concatenate([vl_ref[pl.ds(start, nkw), :], vc_ref[...]], axis=0)
    vext = jnp.concatenate([v_all, jnp.ones((n_all, LANES), BF16)], axis=1)
    outs = []
    for s in range(2):
        pv = jnp.dot(p_scr[s][...], vext, preferred_element_type=F32)
        outs.append(pv[:, :LANES] / pv[:, LANES:])
    o_ref[...] = jnp.where(lane < LANES // 2, outs[0], outs[1]).astype(o_ref.dtype)


def _na_bias_tables(rpb, n_rows):
    nb = n_rows // NA_Q_ROWS
    assert nb >= 3
    dc = np.arange(GRID_W)[None, :] - np.arange(GRID_W)[:, None] + NA_WIN_COLS - 1
    onehot = (dc[None] == np.arange(2 * NA_WIN_COLS - 1)[:, None, None]).astype(np.float32)
    toep = jnp.einsum('hrd,dqk->hrqk', rpb.astype(F32), jnp.asarray(onehot), precision=lax.Precision.HIGHEST)
    cols = np.arange(GRID_W)
    col_start = np.clip(cols - NA_WIN_COLS // 2, 0, GRID_W - NA_WIN_COLS)
    col_ok = (cols[None, :] >= col_start[:, None]) & (cols[None, :] < col_start[:, None] + NA_WIN_COLS)
    ridx = np.zeros((3, NA_Q_ROWS, NA_K_ROWS), np.int32)
    ok = np.zeros((3, NA_Q_ROWS, NA_K_ROWS), bool)
    for var, blk in enumerate((0, 1, nb - 1)):
        r_first = blk * NA_Q_ROWS
        k_first = int(np.clip(r_first - NA_WIN_ROWS // 2, 0, n_rows - NA_K_ROWS))
        for a in range(NA_Q_ROWS):
            qr = r_first + a
            r0 = int(np.clip(qr - NA_WIN_ROWS // 2, 0, n_rows - NA_WIN_ROWS))
            for kb in range(NA_K_ROWS):
                kr = k_first + kb
                if r0 <= kr < r0 + NA_WIN_ROWS:
                    ok[var, a, kb] = True
                    ridx[var, a, kb] = kr - qr + NA_WIN_ROWS - 1
    tab = toep[:, ridx]
    tab = jnp.transpose(tab, (0, 1, 2, 4, 3, 5))
    mask = ok[:, :, None, :, None] & col_ok[None, None, :, None, :]
    tab = jnp.where(jnp.asarray(mask)[None], tab, MASK_VALUE)
    h = rpb.shape[0]
    return tab.reshape(h, 3, NA_Q_ROWS * GRID_W, NA_K_ROWS * GRID_W)


def _na_attention(qkv_lat, qkv_ctx, bias_tab, *, batch, n_lat, n_ctx):
    d = NA_HEADS * NA_HEAD_DIM
    groups = d // LANES
    n_rows = n_lat // GRID_W
    nb = n_rows // NA_Q_ROWS
    tq = NA_Q_ROWS * GRID_W

    def var_of(i):
        return jnp.where(i == 0, 0, jnp.where(i == nb - 1, 2, 1))

    in_specs = [
        pl.BlockSpec((tq, LANES), lambda b, g, i: (b * nb + i, g)),
        pl.BlockSpec((n_ctx, LANES), lambda b, g, i: (b, groups + g)),
        pl.BlockSpec((n_ctx, LANES), lambda b, g, i: (b, 2 * groups + g)),
        pl.BlockSpec((n_lat, LANES), lambda b, g, i: (b, groups + g)),
        pl.BlockSpec((n_lat, LANES), lambda b, g, i: (b, 2 * groups + g)),
        pl.BlockSpec((2, None, tq, NA_K_ROWS * GRID_W), lambda b, g, i: (g, var_of(i), 0, 0)),
    ]
    n_all = NA_K_ROWS * GRID_W + n_ctx
    return pl.pallas_call(
        functools.partial(_na_kernel, n_rows=n_rows, rb=16), grid=(batch, groups, nb), in_specs=in_specs,
        out_specs=pl.BlockSpec((tq, LANES), lambda b, g, i: (b * nb + i, g)),
        out_shape=jax.ShapeDtypeStruct((batch * n_lat, d), BF16),
        scratch_shapes=2 * [pltpu.VMEM((tq, n_all), F32)] + 2 * [pltpu.VMEM((tq, n_all), BF16)],
        compiler_params=_cparams(("parallel", "parallel", "arbitrary")), name="na_attn",
    )(qkv_lat, qkv_ctx, qkv_ctx, qkv_lat, qkv_lat, bias_tab)


def _dft_kernel(c_ref, ms_ref, ab_ref, mid_ref, o_ref, acc_ref, *, scale):
    n = pl.program_id(2)
    d = o_ref.shape[1]
    part = (jnp.dot(c_ref[...], ab_ref[:, :d], preferred_element_type=F32)
            + jnp.dot(ms_ref[...], ab_ref[:, d:], preferred_element_type=F32))

    @pl.when(n == 0)
    def _():
        acc_ref[...] = part

    @pl.when(n > 0)
    def _():
        acc_ref[...] += part

    @pl.when(n == pl.num_programs(2) - 1)
    def _():
        tk = o_ref.shape[0]
        k = pl.program_id(1) * tk + lax.broadcasted_iota(jnp.int32, (tk, 1), 0)
        sign = jnp.where(k % 2 == 0, 1.0, -1.0)
        o_ref[...] = ((acc_ref[...] + sign * mid_ref[...].astype(F32)) * scale).astype(o_ref.dtype)


def _pos_dft(cos_t, msin_t, ab, *, batch, n_tok, d, scale, name):
    half = n_tok // 2
    ab3 = ab.reshape(batch, n_tok, 2 * d)
    mirror = jnp.roll(jnp.flip(ab3, axis=1), 1, axis=1)[:, :half]
    keep = (jnp.arange(half) > 0)[None, :, None]
    sign = jnp.concatenate([jnp.ones((d,), BF16), -jnp.ones((d,), BF16)])
    folded = (ab3[:, :half] + jnp.where(keep, mirror * sign, jnp.zeros_like(mirror))).reshape(batch * half, 2 * d)
    mid = ab3[:, half:half + 1, :d]
    tk = min(1024, n_tok)
    tn = min(1024, half)
    nk, nn = n_tok // tk, half // tn
    return pl.pallas_call(
        functools.partial(_dft_kernel, scale=scale), grid=(batch, nk, nn),
        in_specs=[pl.BlockSpec((tk, tn), lambda b, k, n: (k, n)),
                  pl.BlockSpec((tk, tn), lambda b, k, n: (k, n)),
                  pl.BlockSpec((tn, 2 * d), lambda b, k, n: (b * nn + n, 0)),
                  pl.BlockSpec((None, 1, d), lambda b, k, n: (b, 0, 0))],
        out_specs=pl.BlockSpec((tk, d), lambda b, k, n: (b * nk + k, 0)),
        out_shape=jax.ShapeDtypeStruct((batch * n_tok, d), BF16),
        scratch_shapes=[pltpu.VMEM((tk, d), F32)],
        compiler_params=_cparams(("parallel", "parallel", "arbitrary")), name=name)(cos_t, msin_t, folded, mid)


def _dft_tables(n):
    split = 64 if n % 64 == 0 and n > 64 else 1
    m = jnp.arange(n // 2, dtype=jnp.int32)[None, :]

    def phase(kvals):
        ang = ((kvals[:, None] * m) % n).astype(F32) * (2.0 * math.pi / n)
        return jnp.cos(ang), jnp.sin(ang)

    c1, s1 = phase(jnp.arange(n // split, dtype=jnp.int32) * split)
    c2, s2 = phase(jnp.arange(split, dtype=jnp.int32))
    c = c1[:, None, :] * c2[None, :, :] - s1[:, None, :] * s2[None, :, :]
    s = s1[:, None, :] * c2[None, :, :] + c1[:, None, :] * s2[None, :, :]
    return c.reshape(n, n // 2).astype(BF16), (-s).reshape(n, n // 2).astype(BF16)


def _channel_dft_weight(d):
    cg = d // FN_GROUPS
    idx = np.arange(cg)
    ang = 2.0 * np.pi * ((idx[:, None] * idx[None, :]) % cg) / cg
    eye = np.eye(FN_GROUPS)
    return jnp.asarray(np.concatenate([np.kron(eye, np.cos(ang)), np.kron(eye, np.sin(ang))], axis=1), F32)


def _router_kernel(x_ref, gain_ref, sc_ref, sh_ref, wr_ref, h_ref, aff_ref):
    x = x_ref[...]
    ms = jnp.mean(x * x, axis=-1, keepdims=True)
    h = x * lax.rsqrt(ms + NORM_EPS) * gain_ref[...]
    h = h * (1.0 + sc_ref[...]) + sh_ref[...]
    h_ref[...] = h.astype(h_ref.dtype)
    w = wr_ref[...]
    h_hi, w_hi = h.astype(BF16), w.astype(BF16)
    h_lo, w_lo = (h - h_hi.astype(F32)).astype(BF16), (w - w_hi.astype(F32)).astype(BF16)
    logits = (jnp.dot(h_hi, w_hi, preferred_element_type=F32) + jnp.dot(h_lo, w_hi, preferred_element_type=F32)
              + jnp.dot(h_hi, w_lo, preferred_element_type=F32))
    e = jnp.exp(logits - jnp.max(logits, axis=-1, keepdims=True))
    aff_ref[...] = e / jnp.sum(e, axis=-1, keepdims=True)


def _norm_router(x, gain, sc, sh, w_router, layer, *, rows_per_batch, tm=512):
    m, d = x.shape
    e = w_router.shape[-1]
    tm = min(tm, rows_per_batch)
    tpb = rows_per_batch // tm
    return pl.pallas_call(
        _router_kernel, grid=(m // tm,),
        in_specs=[pl.BlockSpec((tm, d), lambda i: (i, 0)),
                  pl.BlockSpec((1, d), lambda i: (0, 0)),
                  pl.BlockSpec((None, 1, d), lambda i: (i // tpb, 0, 0)),
                  pl.BlockSpec((None, 1, d), lambda i: (i // tpb, 0, 0)),
                  pl.BlockSpec((None, d, e), lambda i: (layer, 0, 0))],
        out_specs=[pl.BlockSpec((tm, d), lambda i: (i, 0)), pl.BlockSpec((tm, e), lambda i: (i, 0))],
        out_shape=[jax.ShapeDtypeStruct((m, d), BF16), jax.ShapeDtypeStruct((m, e), F32)],
        compiler_params=_cparams(("parallel",)), name="norm_router")(x, gain.reshape(1, d), sc, sh, w_router)


FFN_PIECE = 256


def _ffn_kernel(*refs, n_streams):
    x_refs = refs[:n_streams]
    wg_ref, wu_ref, wd_ref = refs[n_streams:n_streams + 3]
    gate_refs = refs[n_streams + 3:2 * n_streams + 3]
    o_refs = refs[2 * n_streams + 3:3 * n_streams + 3]
    acc_ref = refs[3 * n_streams + 3]
    f = pl.program_id(2)
    x = x_refs[0][...] if n_streams == 1 else jnp.concatenate([r[...] for r in x_refs], axis=0)
    @pl.when(f == 0)
    def _():
        acc_ref[...] = jnp.zeros(acc_ref.shape, F32)

    tf = wg_ref.shape[1]
    acts = []
    for c0 in range(0, tf, FFN_PIECE):
        a = jnp.dot(x, wg_ref[:, c0:c0 + FFN_PIECE].astype(BF16), preferred_element_type=F32)
        u = jnp.dot(x, wu_ref[:, c0:c0 + FFN_PIECE].astype(BF16), preferred_element_type=F32)
        acts.append(((a * (1.0 / (1.0 + jnp.exp(-a)))) * u).astype(BF16))
    d = wd_ref.shape[1]
    for n0 in range(0, d, FFN_PIECE):
        y = None
        for i, act in enumerate(acts):
            part = jnp.dot(act, wd_ref[i * FFN_PIECE:(i + 1) * FFN_PIECE, n0:n0 + FFN_PIECE].astype(BF16),
                           preferred_element_type=F32)
            y = part if y is None else y + part
        acc_ref[:, n0:n0 + FFN_PIECE] += y

    @pl.when(f == pl.num_programs(2) - 1)
    def _():
        row = 0
        for o_ref, g_ref in zip(o_refs, gate_refs):
            c = o_ref.shape[0]
            o_ref[...] = acc_ref[row:row + c, :] * g_ref[...]
            row += c


def _moe_ffn(xs, gates, w_gate, w_up, w_down, layer, *, tf=512):
    b, e, _, d = xs[0].shape
    f = w_gate.shape[-1]
    tf = min(tf, f)
    slot = lambda c, w: pl.BlockSpec((None, None, c, w), lambda ei, bi, fi: (bi, ei, 0, 0))
    caps = [t.shape[2] for t in xs]
    in_specs = ([slot(c, d) for c in caps]
                + [pl.BlockSpec((None, None, d, tf), lambda ei, bi, fi: (layer, ei, 0, fi)),
                   pl.BlockSpec((None, None, d, tf), lambda ei, bi, fi: (layer, ei, 0, fi)),
                   pl.BlockSpec((None, None, tf, d), lambda ei, bi, fi: (layer, ei, fi, 0))]
                + [slot(c, 1) for c in caps])
    return pl.pallas_call(
        functools.partial(_ffn_kernel, n_streams=len(xs)), grid=(e, b, f // tf), in_specs=in_specs,
        out_specs=[slot(c, d) for c in caps],
        out_shape=[jax.ShapeDtypeStruct((b, e, c, d), F32) for c in caps],
        scratch_shapes=[pltpu.VMEM((sum(caps), d), F32)],
        compiler_params=_cparams(("parallel", "parallel", "arbitrary")), name="moe_ffn",
    )(*xs, w_gate, w_up, w_down, *gates)


def _count_upto(flags):
    lead, n = flags.shape[:-1], flags.shape[-1]
    blocks = flags.reshape(*lead, n // LANES, LANES).astype(BF16)
    tri = jnp.asarray(np.triu(np.ones((LANES, LANES), np.float32))).astype(BF16)
    inside = jnp.einsum('...i,ij->...j', blocks, tri, preferred_element_type=F32)
    totals = inside[..., -1]
    before = jnp.cumsum(totals, axis=-1) - totals
    return (inside + before[..., None]).reshape(*lead, n).astype(jnp.int32)


def _route(aff, h, batch, n_tok):
    e = aff.shape[-1]
    cap = EC_CAPACITY_FACTOR * n_tok // e
    aff_t = jnp.swapaxes(aff.reshape(batch, n_tok, e), 1, 2)
    gate, idx = lax.top_k(aff_t, cap)
    thr = gate[..., -1:]
    above, tied = aff_t > thr, aff_t == thr
    n_tied = cap - jnp.sum(above, axis=-1, keepdims=True, dtype=jnp.int32)
    chosen = above | (tied & (_count_upto(tied) <= n_tied))
    pos = jnp.where(chosen, _count_upto(chosen) - 1, -1)
    pos = jnp.swapaxes(pos, 1, 2).reshape(batch * n_tok, e)
    idx, gate = lax.sort((idx, gate), dimension=2, num_keys=1)
    xs = jax.vmap(lambda hb, ib: hb[ib])(h.reshape(batch, n_tok, -1), idx)
    return gate, idx, xs, pos


COMBINE_TOKENS = 256
COMBINE_WINDOW = 64


def _combine_kernel(lo_ref, nr_ref, x_ref, g_ref, pos_ref, y_hbm, o_ref, stage, sem, acc, *, n_tiles, n_total, n_exp,
                    cap):
    w = COMBINE_WINDOW
    per = LANES // w
    tile = pl.program_id(0) * n_tiles + pl.program_id(1)
    slot = tile % 2
    lane = lax.broadcasted_iota(jnp.int32, (1, LANES), 1)

    def window(tl, e, r):
        first = lo_ref[tl * n_exp + e] + r * w
        return first, pl.multiple_of(jnp.minimum(first, cap - w), 8)

    def copy(tl, e, start, sl):
        return pltpu.make_async_copy(y_hbm.at[tl // n_tiles, e, pl.ds(start, w), :],
                                     stage.at[sl, pl.ds(e * w, w), :], sem.at[sl, e])

    def fetch(tl, r, sl):
        for e in range(n_exp):
            copy(tl, e, window(tl, e, r)[1], sl).start()

    def add_round(r):
        pos = pos_ref[...]
        blocks = []
        for e0 in range(0, n_exp, per):
            first, start = window(tile, e0, r)
            col = pos[:, e0:e0 + 1]
            for k in range(1, per):
                f_k, s_k = window(tile, e0 + k, r)
                here = lane >= k * w
                first, start = jnp.where(here, f_k, first), jnp.where(here, s_k, start)
                col = jnp.where(here, pos[:, e0 + k:e0 + k + 1], col)
            hit = jnp.logical_and(col - start == lane % w, col >= first)
            blocks.append(jnp.where(hit, 1.0, 0.0).astype(BF16))
        onehot = jnp.concatenate(blocks, axis=1)
        for e in range(n_exp):
            copy(tile, e, 0, slot).wait()
        y = stage[slot]
        y_hi = y.astype(BF16)
        y_lo = (y - y_hi.astype(F32)).astype(BF16)
        acc[...] += (jnp.dot(onehot, y_hi, preferred_element_type=F32)
                     + jnp.dot(onehot, y_lo, preferred_element_type=F32))

    @pl.when(tile == 0)
    def _():
        fetch(tile, 0, slot)

    @pl.when(tile + 1 < n_total)
    def _():
        fetch(tile + 1, 0, 1 - slot)

    acc[...] = jnp.zeros(acc.shape, F32)
    add_round(0)

    @pl.loop(1, nr_ref[tile])
    def _(r):
        fetch(tile, r, slot)
        add_round(r)

    o_ref[...] = x_ref[...] + g_ref[...] * acc[...]


def _combine_lat(x, g2, y, idx, pos, batch, n_tok):
    d = x.shape[-1]
    _, n_exp, cap = idx.shape
    tt, w = COMBINE_TOKENS, COMBINE_WINDOW
    assert n_tok % tt == 0 and cap % 8 == 0 and cap >= w
    nt = n_tok // tt
    bounds = jnp.arange(nt + 1, dtype=jnp.int32) * tt
    below = jnp.sum((idx[..., None] < bounds).astype(jnp.int32), axis=2)
    lo = (below[..., :-1] // 8) * 8
    rounds = jnp.maximum(jnp.max((below[..., 1:] - lo + w - 1) // w, axis=1), 1)
    lo_flat = jnp.transpose(lo, (0, 2, 1)).reshape(-1)
    grid_spec = pltpu.PrefetchScalarGridSpec(
        num_scalar_prefetch=2, grid=(batch, nt),
        in_specs=[pl.BlockSpec((tt, d), lambda b, t, lo_r, nr_r: (b * nt + t, 0)),
                  pl.BlockSpec((None, 1, d), lambda b, t, lo_r, nr_r: (b, 0, 0)),
                  pl.BlockSpec((tt, n_exp), lambda b, t, lo_r, nr_r: (b * nt + t, 0)),
                  pl.BlockSpec(memory_space=pl.ANY)],
        out_specs=pl.BlockSpec((tt, d), lambda b, t, lo_r, nr_r: (b * nt + t, 0)),
        scratch_shapes=[pltpu.VMEM((2, n_exp * w, d), F32), pltpu.SemaphoreType.DMA((2, n_exp)),
                        pltpu.VMEM((tt, d), F32)])
    return pl.pallas_call(
        functools.partial(_combine_kernel, n_tiles=nt, n_total=batch * nt, n_exp=n_exp, cap=cap),
        grid_spec=grid_spec, out_shape=jax.ShapeDtypeStruct(x.shape, F32),
        compiler_params=_cparams(("arbitrary", "arbitrary")), name="moe_combine",
    )(lo_flat, rounds.reshape(-1), x, g2, pos, y)


def _combine(x, g2, y, idx, batch, n_tok):
    d = x.shape[-1]
    upd = jax.vmap(lambda ib, yb: jnp.zeros((n_tok, d), F32).at[ib.reshape(-1)].add(yb.reshape(-1, d)))(idx, y)
    return (x.reshape(batch, n_tok, d) + g2 * upd).reshape(batch * n_tok, d)


def _final_norm_kernel(x_ref, g_ref, o_ref):
    x = x_ref[...]
    ms = jnp.mean(x * x, axis=-1, keepdims=True)
    o_ref[...] = x * lax.rsqrt(ms + NORM_EPS) * g_ref[...]


def _final_norm(x, g, tm=1024):
    m, d = x.shape
    tm = min(tm, m)
    return pl.pallas_call(
        _final_norm_kernel, grid=(m // tm,),
        in_specs=[pl.BlockSpec((tm, d), lambda i: (i, 0)), pl.BlockSpec((1, d), lambda i: (0, 0))],
        out_specs=pl.BlockSpec((tm, d), lambda i: (i, 0)), out_shape=jax.ShapeDtypeStruct((m, d), F32),
        compiler_params=_cparams(("parallel",)), name="final_norm")(x, g.reshape(1, d))


def _rope_tables(n_tok, rot_dim, pair_of_lane):
    t = jnp.arange(n_tok)
    rows = (t // GRID_W).astype(F32)
    cols = (t % GRID_W).astype(F32)
    n_freq = rot_dim // 4
    inv_freq = ROPE_BASE ** (-jnp.arange(n_freq, dtype=F32) / n_freq)
    ang = jnp.concatenate([rows[:, None] * inv_freq, cols[:, None] * inv_freq], axis=-1)
    cos, sin = jnp.cos(ang), jnp.sin(ang)
    pair = np.concatenate([pair_of_lane, pair_of_lane])
    used = jnp.asarray(pair >= 0)[None]
    sign = jnp.asarray(np.where(np.arange(LANES) < LANES // 2, -1.0, 1.0).astype(np.float32))[None]
    cos_l = jnp.where(used, cos[:, np.maximum(pair, 0)], 1.0)
    sin_l = jnp.where(used, sin[:, np.maximum(pair, 0)] * sign, 0.0)
    return cos_l, sin_l


def kernel(x, c, ctx, c_ctx, ada_w, ada_b, norm_mix, norm_ffn, norm_final, da_w_qkv, da_w_o, da_lambda_q1,
           da_lambda_k1, da_lambda_q2, da_lambda_k2, da_subln, fn_w_o, na_w_qkv, na_w_o, na_rpb, mla_w_dq,
           mla_q_norm, mla_w_uq, mla_w_dkv, mla_kv_norm, mla_w_uk, mla_w_uv, mla_w_o, moe_w_router, moe_w_gate,
           moe_w_up, moe_w_down):
    batch, n_lat, d = x.shape
    n_ctx = ctx.shape[1]
    depth = ada_w.shape[0]
    n_mixers = 4
    xl = x.reshape(batch * n_lat, d)
    xc = ctx.reshape(batch * n_ctx, d)

    cond = jnp.concatenate([c, c_ctx[None], jnp.zeros((8 - batch - 1, d), F32)], axis=0)
    mods = [
        _fused_mm(cond, ada_w, w_index=i, bias=ada_b[i], silu_in=True, precision=lax.Precision.HIGHEST, tn=1024,
                  name="ada_mod")
        for i in range(depth)
    ]

    def chunks(i):
        lat = [mods[i][:batch, k * d:(k + 1) * d].reshape(batch, 1, d) for k in range(ADA_CHUNKS)]
        cx = [mods[i][batch:batch + 1, k * d:(k + 1) * d].reshape(1, 1, d) for k in range(ADA_CHUNKS)]
        return lat, cx

    def all_keys(t_ctx, t_lat):
        w = t_ctx.shape[-1]
        both = jnp.concatenate([t_ctx.reshape(batch, n_ctx, w), t_lat.reshape(batch, n_lat, w)], axis=1)
        return both.reshape(batch * (n_ctx + n_lat), w)

    for i in range(depth):
        kind, j = i % n_mixers, i // n_mixers
        keep_ctx = i < depth - 1
        (sh1, sc1, g1, sh2, sc2, g2), (csh1, csc1, cg1, csh2, csc2, cg2) = chunks(i)
        lat_in = dict(gain=norm_mix[i], mod=(sc1, sh1), rows_per_batch=n_lat)
        ctx_in = dict(gain=norm_mix[i], mod=(csc1, csh1), rows_per_batch=batch * n_ctx, tm=n_ctx)

        if kind == 0:
            lambda_init = 0.8 - 0.6 * math.exp(-0.3 * i)
            scale = jnp.concatenate([jnp.full((d,), LOG2E * DA_HEAD_DIM ** -0.5, F32), jnp.ones((2 * d,), F32)])
            groups = d // LANES
            half = DA_HEAD_DIM // 2
            w = da_w_qkv[j]

            def reorder(cols):
                return cols.reshape(d, groups, 2, 2, half).transpose(0, 1, 3, 2, 4).reshape(d, d)

            w = jnp.concatenate([reorder(w[:, :d]), reorder(w[:, d:2 * d]), w[:, 2 * d:]], axis=1)
            rope = _rope_tables(n_lat, DA_HEAD_DIM, np.arange(LANES // 2) % half)
            qkv_l = _fused_mm(xl, w, colscale=scale, rope=rope, rope_cols=2 * d, out_dtype=BF16, name="da_qkv_lat",
                              **lat_in)
            qkv_c = _fused_mm(xc, w, colscale=scale, out_dtype=BF16, name="da_qkv_ctx", **ctx_in)
            lam_params = jnp.stack([da_lambda_q1[j], da_lambda_k1[j], da_lambda_q2[j], da_lambda_k2[j]])
            diff = (lam_params, da_subln[j], lambda_init)
            kv = all_keys(qkv_c[:, d:], qkv_l[:, d:])
            o_l = _attention(qkv_l, 0, kv, 0, kv, groups, batch=batch, n_q=n_lat, n_k=n_ctx + n_lat, groups=groups,
                             sub_width=half, diff=diff, name="da_attn_lat")
            o_c = _attention(qkv_c, 0, qkv_c, groups, qkv_c, 2 * groups, batch=batch, n_q=n_ctx, n_k=n_ctx,
                             groups=groups, sub_width=half, diff=diff, name="da_attn_ctx") if keep_ctx else None
            w_o = da_w_o
        elif kind == 1:
            w_cd = _channel_dft_weight(d)
            ab_l = _fused_mm(xl, w_cd, out_dtype=BF16, name="fn_chan_lat", **lat_in)
            scale_l = 1.0 / math.sqrt(n_lat * (d // FN_GROUPS))
            o_l = _pos_dft(*_dft_tables(n_lat), ab_l, batch=batch, n_tok=n_lat, d=d, scale=scale_l, name="fn_pos_lat")
            o_c = None
            if keep_ctx:
                ab_c = _fused_mm(xc, w_cd, out_dtype=BF16, name="fn_chan_ctx", **ctx_in)
                scale_c = 1.0 / math.sqrt(n_ctx * (d // FN_GROUPS))
                o_c = _pos_dft(*_dft_tables(n_ctx), ab_c, batch=batch, n_tok=n_ctx, d=d, scale=scale_c,
                               name="fn_pos_ctx")
            w_o = fn_w_o
        elif kind == 2:
            scale = jnp.concatenate([jnp.full((d,), LOG2E * NA_HEAD_DIM ** -0.5, F32), jnp.ones((2 * d,), F32)])
            qkv_l = _fused_mm(xl, na_w_qkv, w_index=j, colscale=scale, out_dtype=BF16, name="na_qkv_lat", **lat_in)
            qkv_c = _fused_mm(xc, na_w_qkv, w_index=j, colscale=scale, out_dtype=BF16, name="na_qkv_ctx", **ctx_in)
            groups = d // LANES
            bias_tab = _na_bias_tables(na_rpb[j] * LOG2E, n_lat // GRID_W)
            o_l = _na_attention(qkv_l, qkv_c, bias_tab, batch=batch, n_lat=n_lat, n_ctx=n_ctx)
            o_c = _attention(qkv_c, 0, qkv_c, groups, qkv_c, 2 * groups, batch=batch, n_q=n_ctx, n_k=n_ctx,
                             groups=groups, name="na_attn_ctx") if keep_ctx else None
            w_o = na_w_o
        else:
            q_rank = mla_w_dq.shape[-1]
            kv_rank = mla_w_uk.shape[-2]
            hq = MLA_NOPE_DIM + MLA_ROPE_DIM
            down_w = 3 * 384
            assert q_rank == 768 and kv_rank == 256
            w_down = jnp.concatenate([mla_w_dq[j], mla_w_dkv[j], jnp.zeros((d, down_w - q_rank - kv_rank - MLA_ROPE_DIM), F32)],
                                     axis=1)
            rh = MLA_ROPE_DIM // 2
            n_a = LANES // 2 - rh

            def head_lanes(nope, rope_part):
                z = jnp.zeros(nope.shape[:-1] + (LANES - hq,), F32)
                return jnp.concatenate([rope_part[..., :rh], nope[..., :n_a], rope_part[..., rh:], nope[..., n_a:], z],
                                       axis=-1)

            wq = mla_w_uq[j].reshape(q_rank, MLA_HEADS, hq)
            wq = head_lanes(wq[..., :MLA_NOPE_DIM], wq[..., MLA_NOPE_DIM:]).reshape(q_rank, MLA_HEADS * LANES)
            wk_nope = head_lanes(mla_w_uk[j].reshape(kv_rank, MLA_HEADS, MLA_NOPE_DIM),
                                 jnp.zeros((kv_rank, MLA_HEADS, MLA_ROPE_DIM), F32))
            eye = jnp.broadcast_to(jnp.eye(MLA_ROPE_DIM, dtype=F32)[:, None, :], (MLA_ROPE_DIM, MLA_HEADS, MLA_ROPE_DIM))
            wk_rope = head_lanes(jnp.zeros((MLA_ROPE_DIM, MLA_HEADS, MLA_NOPE_DIM), F32), eye)
            kw = MLA_HEADS * LANES
            vw = MLA_HEADS * MLA_V_DIM
            pad_rows = 384 - kv_rank - MLA_ROPE_DIM
            w_kv = jnp.concatenate([
                jnp.concatenate([wk_nope.reshape(kv_rank, kw), mla_w_uv[j]], axis=1),
                jnp.concatenate([wk_rope.reshape(MLA_ROPE_DIM, kw), jnp.zeros((MLA_ROPE_DIM, vw), F32)], axis=1),
                jnp.zeros((pad_rows, kw + vw), F32)], axis=0)
            kv_gain = jnp.concatenate([mla_kv_norm[j], jnp.ones((384 - kv_rank,), F32)])
            rope = _rope_tables(n_lat, MLA_ROPE_DIM, np.where(np.arange(LANES // 2) < rh, np.arange(LANES // 2), -1))
            qscale = jnp.full((kw,), LOG2E * hq ** -0.5, F32)

            def project(xs, inp, n_tok, rope_tabs, tag):
                low = _fused_mm(xs, w_down, tn=384, name="mla_down_" + tag, **inp)
                rp = dict(rope=rope_tabs) if rope_tabs is not None else {}
                tmx = inp.get("tm", 1024)
                q = _fused_mm(low, wq, a_cols=(0, q_rank), gain=mla_q_norm[j], colscale=qscale, rope_cols=kw,
                              rows_per_batch=inp["rows_per_batch"], tm=tmx, out_dtype=BF16, name="mla_q_" + tag, **rp)
                kv = _fused_mm(low, w_kv, a_cols=(q_rank, 384), gain=kv_gain, norm_cols=kv_rank, rope_cols=kw,
                               rows_per_batch=inp["rows_per_batch"], tm=tmx, out_dtype=BF16, name="mla_kv_" + tag, **rp)
                return q, kv

            q_l, kv_l = project(xl, lat_in, n_lat, rope, "lat")
            q_c, kv_c = project(xc, ctx_in, n_ctx, None, "ctx")
            groups = MLA_HEADS // 2
            kv = all_keys(kv_c, kv_l)
            o_l = _attention(q_l, 0, kv, 0, kv, 2 * groups, batch=batch, n_q=n_lat, n_k=n_ctx + n_lat, groups=groups,
                             slice_mode=True, name="mla_attn_lat")
            o_c = _attention(q_c, 0, kv_c, 0, kv_c, 2 * groups, batch=batch, n_q=n_ctx, n_k=n_ctx, groups=groups,
                             slice_mode=True, name="mla_attn_ctx") if keep_ctx else None
            w_o = mla_w_o

        xl = _fused_mm(o_l, w_o, w_index=j, res=xl, gate=g1, rows_per_batch=n_lat, name="mix_out_lat")
        if keep_ctx:
            xc = _fused_mm(o_c, w_o, w_index=j, res=xc, gate=jnp.broadcast_to(cg1, (1, 1, d)),
                           rows_per_batch=batch * n_ctx, tm=n_ctx, name="mix_out_ctx")

        h_l, aff_l = _norm_router(xl, norm_ffn[i], sc2, sh2, moe_w_router, i, rows_per_batch=n_lat)
        gate_l, idx_l, xs_l, pos_l = _route(aff_l, h_l, batch, n_lat)
        xs, gates = [xs_l], [gate_l[..., None]]
        if keep_ctx:
            h_c, aff_c = _norm_router(xc, norm_ffn[i], csc2, csh2, moe_w_router, i, rows_per_batch=batch * n_ctx,
                                      tm=n_ctx)
            gate_c, idx_c, xs_c, _ = _route(aff_c, h_c, batch, n_ctx)
            xs.append(xs_c)
            gates.append(gate_c[..., None])
        ys = _moe_ffn(xs, gates, moe_w_gate, moe_w_up, moe_w_down, i)
        xl = _combine_lat(xl, g2, ys[0], idx_l, pos_l, batch, n_lat)
        if keep_ctx:
            xc = _combine(xc, cg2, ys[1], idx_c, batch, n_ctx)

    return _final_norm(xl, norm_final).reshape(batch, n_lat, d)
```

```python
import functools
import math

import numpy as np
import jax
import jax.numpy as jnp
from jax import lax
from jax.experimental import pallas as pl
from jax.experimental.pallas import tpu as pltpu

F32 = jnp.float32
BF16 = jnp.bfloat16

LANES = 128
VMEM_LIMIT_BYTES = 56 * 1024 * 1024

GRID_W = 64
ROPE_BASE = 10000.0
NORM_EPS = 1e-6
ADA_CHUNKS = 6
DA_HEAD_DIM = 64
NA_HEADS = 16
NA_HEAD_DIM = 64
NA_WIN_ROWS = 8
NA_WIN_COLS = 16
NA_Q_ROWS = 8
NA_K_ROWS = 16
MLA_HEADS = 16
MLA_NOPE_DIM = 64
MLA_ROPE_DIM = 32
MLA_V_DIM = 64
FN_GROUPS = 4
EC_CAPACITY_FACTOR = 2
MASK_VALUE = -1e30
LOG2E = math.log2(math.e)


def _cparams(sem):
    return pltpu.CompilerParams(dimension_semantics=sem, vmem_limit_bytes=VMEM_LIMIT_BYTES)


def _rope_lanes(acc, cos, sin):
    outs = []
    for g in range(acc.shape[1] // LANES):
        blk = acc[:, g * LANES:(g + 1) * LANES]
        outs.append(blk * cos + pltpu.roll(blk, LANES // 2, 1) * sin)
    return outs[0] if len(outs) == 1 else jnp.concatenate(outs, axis=1)


def _mm_kernel(*refs, norm_cols, has_gain, has_mod, has_colscale, rope_tiles, has_bias, has_res,
               silu_in, precision):
    it = iter(refs)
    a_ref = next(it)
    w_ref = next(it)
    gain_ref = next(it) if has_gain else None
    sc_ref, sh_ref = (next(it), next(it)) if has_mod else (None, None)
    cs_ref = next(it) if has_colscale else None
    cos_ref, sin_ref = (next(it), next(it)) if rope_tiles else (None, None)
    bias_ref = next(it) if has_bias else None
    res_ref, gate_ref = (next(it), next(it)) if has_res else (None, None)
    o_ref = next(it)
    h_scr = next(it, None)
    j = pl.program_id(1)
    hi = precision is not None

    def prologue():
        a = a_ref[...].astype(F32)
        if silu_in:
            a = a * (1.0 / (1.0 + jnp.exp(-a)))
        if has_gain:
            k = a.shape[1]
            if norm_cols < k:
                col = lax.broadcasted_iota(jnp.int32, (1, k), 1)
                sq = jnp.where(col < norm_cols, a * a, 0.0)
            else:
                sq = a * a
            ms = jnp.sum(sq, axis=-1, keepdims=True) * (1.0 / norm_cols)
            y = a * lax.rsqrt(ms + NORM_EPS) * gain_ref[...]
            a = jnp.where(col < norm_cols, y, a) if norm_cols < k else y
        if has_mod:
            a = a * (1.0 + sc_ref[...]) + sh_ref[...]
        return a if hi else a.astype(BF16)

    if h_scr is not None:
        @pl.when(j == 0)
        def _():
            h_scr[...] = prologue()
        h = h_scr[...]
    else:
        h = a_ref[...] if hi else a_ref[...].astype(BF16)
    w = w_ref[...] if hi else w_ref[...].astype(BF16)
    acc = jnp.dot(h, w, preferred_element_type=F32, precision=precision)
    if has_colscale:
        acc = acc * cs_ref[...]

    def finish(v):
        if has_bias:
            v = v + bias_ref[...]
        if has_res:
            v = res_ref[...] + gate_ref[...] * v
        o_ref[...] = v.astype(o_ref.dtype)

    if rope_tiles:
        @pl.when(j < rope_tiles)
        def _():
            finish(_rope_lanes(acc, cos_ref[...], sin_ref[...]))

        @pl.when(j >= rope_tiles)
        def _():
            finish(acc)
    else:
        finish(acc)


def _fused_mm(a, w, *, w_index=None, a_cols=None, gain=None, norm_cols=None, mod=None, colscale=None, rope=None,
              rope_cols=0, bias=None, res=None, gate=None, silu_in=False, rows_per_batch=None,
              out_dtype=F32, tm=1024, tn=1024, precision=None, name="mm"):
    m = a.shape[0]
    k0, k = a_cols if a_cols is not None else (0, a.shape[1])
    if precision is None and w.dtype != BF16:
        w = (w[w_index] if w.ndim == 3 else w).astype(BF16)
    n = w.shape[-1]
    assert w.shape[-2] == k and k0 % k == 0
    tm = min(tm, m)
    tn = min(tn, n)
    assert m % tm == 0 and n % tn == 0
    rows_per_batch = rows_per_batch or m
    assert rows_per_batch % tm == 0
    tpb = rows_per_batch // tm
    has_prologue = gain is not None or mod is not None or silu_in
    in_specs = [pl.BlockSpec((tm, k), lambda i, j: (i, k0 // k))]
    args = [a]
    if w.ndim == 3:
        in_specs.append(pl.BlockSpec((None, k, tn), lambda i, j: (w_index, 0, j)))
    else:
        in_specs.append(pl.BlockSpec((k, tn), lambda i, j: (0, j)))
    args.append(w)
    if gain is not None:
        in_specs.append(pl.BlockSpec((1, k), lambda i, j: (0, 0)))
        args.append(gain.reshape(1, k).astype(F32))
    if mod is not None:
        for t in mod:
            in_specs.append(pl.BlockSpec((None, 1, k), lambda i, j: (i // tpb, 0, 0)))
            args.append(t)
    if colscale is not None:
        in_specs.append(pl.BlockSpec((1, tn), lambda i, j: (0, j)))
        args.append(colscale.reshape(1, n).astype(F32))
    rope_tiles = 0
    if rope is not None:
        assert rope_cols % tn == 0
        rope_tiles = rope_cols // tn
        for t in rope:
            in_specs.append(pl.BlockSpec((tm, LANES), lambda i, j: (i % tpb, 0)))
            args.append(t)
    if bias is not None:
        in_specs.append(pl.BlockSpec((1, tn), lambda i, j: (0, j)))
        args.append(bias.reshape(1, n).astype(F32))
    if res is not None:
        in_specs.append(pl.BlockSpec((tm, tn), lambda i, j: (i, j)))
        args.append(res)
        in_specs.append(pl.BlockSpec((None, 1, tn), lambda i, j: (i // tpb, 0, j)))
        args.append(gate)
    scratch = [pltpu.VMEM((tm, k), F32 if precision is not None else BF16)] if has_prologue else []
    kern = functools.partial(
        _mm_kernel, norm_cols=norm_cols or k, has_gain=gain is not None, has_mod=mod is not None,
        has_colscale=colscale is not None, rope_tiles=rope_tiles, has_bias=bias is not None,
        has_res=res is not None, silu_in=silu_in, precision=precision)
    return pl.pallas_call(
        kern, grid=(m // tm, n // tn), in_specs=in_specs,
        out_specs=pl.BlockSpec((tm, tn), lambda i, j: (i, j)),
        out_shape=jax.ShapeDtypeStruct((m, n), out_dtype), scratch_shapes=scratch,
        compiler_params=_cparams(("parallel", "arbitrary")), name=name)(*args)


def _attn_kernel(*refs, n_src, tk, rb, slice_mode, sub_width, diff_mode, lambda_init):
    q_ref = refs[0]
    k_refs, v_refs = refs[1:1 + 2 * n_src:2], refs[2:2 + 2 * n_src:2]
    pos = 1 + 2 * n_src
    if diff_mode:
        lam_ref, subln_ref = refs[pos], refs[pos + 1]
        pos += 2
    o_ref = refs[pos]
    per = [refs[pos + 1 + 9 * s:pos + 10 + 9 * s] for s in range(2)]
    qm_scr = [t[0] for t in per]
    s_scr = [t[1:3] for t in per]
    p_scr = [t[3:5] for t in per]
    m_scr = [t[5] for t in per]
    a_scr = [t[6:8] for t in per]
    acc_scr = [t[8] for t in per]
    tq = q_ref.shape[0]
    q = q_ref[...]
    lane = lax.broadcasted_iota(jnp.int32, (1, LANES), 1)
    for s in range(2):
        if slice_mode:
            qm_scr[s][...] = q[:, s * LANES:(s + 1) * LANES]
        else:
            qm_scr[s][...] = jnp.where((lane // sub_width) % 2 == s, q, jnp.zeros_like(q))
        m_scr[s][...] = jnp.full(m_scr[s].shape, MASK_VALUE, F32)
        acc_scr[s][...] = jnp.zeros(acc_scr[s].shape, F32)

    lengths = [r.shape[0] for r in k_refs]
    n_chunks = sum(lengths) // tk

    def chunk_of(src_refs, c):
        pieces, first, begin = [], c * tk, 0
        for ref, n in zip(src_refs, lengths):
            lo, hi = max(first, begin), min(first + tk, begin + n)
            if lo < hi:
                pieces.append(ref[lo - begin:hi - begin, :])
            begin += n
        return pieces[0] if len(pieces) == 1 else jnp.concatenate(pieces, axis=0)

    def scores(c, slot):
        k = chunk_of(k_refs, c)
        for s in range(2):
            ks = k[:, s * LANES:(s + 1) * LANES] if slice_mode else k
            s_scr[s][slot][...] = lax.dot_general(qm_scr[s][...], ks, (((1,), (1,)), ((), ())),
                                                  preferred_element_type=F32)

    def softmax(slot):
        for s in range(2):
            for r in range(tq // rb):
                rows = slice(r * rb, (r + 1) * rb)
                sb = s_scr[s][slot][rows, :]
                m_old = m_scr[s][rows, :]
                m_new = jnp.maximum(m_old, jnp.max(sb, axis=-1, keepdims=True))
                m_scr[s][rows, :] = m_new
                a_scr[s][slot][rows, :] = jnp.exp2(m_old - m_new)
                p_scr[s][slot][rows, :] = jnp.exp2(sb - jnp.tile(m_new, (1, tk // LANES))).astype(BF16)

    def weighted_sum(c, slot):
        v = chunk_of(v_refs, c)
        vext = jnp.concatenate([v, jnp.ones((tk, LANES), BF16)], axis=1)
        for s in range(2):
            alpha = a_scr[s][slot][...]
            pv = jnp.dot(p_scr[s][slot][...], vext, preferred_element_type=F32)
            acc_scr[s][...] = jnp.concatenate([alpha, alpha], axis=1) * acc_scr[s][...] + pv

    def step(c, slot):
        scores(c, slot)
        softmax(1 - slot)
        weighted_sum(c - 2, slot)

    scores(0, 0)
    if n_chunks > 1:
        scores(1, 1)
    softmax(0)
    for c in range(2, n_chunks):
        step(c, c % 2)
    if n_chunks > 1:
        softmax((n_chunks - 1) % 2)
        weighted_sum(n_chunks - 2, n_chunks % 2)
    weighted_sum(n_chunks - 1, (n_chunks - 1) % 2)
    o0 = acc_scr[0][:, :LANES] / acc_scr[0][:, LANES:]
    o1 = acc_scr[1][:, :LANES] / acc_scr[1][:, LANES:]
    if diff_mode:
        lp = lam_ref[...]
        lam = (jnp.exp(jnp.sum(lp[0:1] * lp[1:2], axis=-1, keepdims=True))
               - jnp.exp(jnp.sum(lp[2:3] * lp[3:4], axis=-1, keepdims=True)) + lambda_init)
        o = o0 - lam * o1
        ms = jnp.mean(o * o, axis=-1, keepdims=True)
        o = o * lax.rsqrt(ms + NORM_EPS) * subln_ref[...] * (1.0 - lambda_init)
    else:
        o = jnp.where(lane < LANES // 2, o0, o1)
    o_ref[...] = o.astype(o_ref.dtype)


def _attention(q_arr, q_blk, srcs, *, batch, n_q, groups, slice_mode=False, diff=None, sub_width=LANES // 2,
               tq=512, tk=768, rb=16, name="attn"):
    qw = 2 * LANES if slice_mode else LANES
    n_k = sum(s[4] for s in srcs)
    tq = min(tq, n_q)
    tk = min(tk, n_k)
    assert n_q % tq == 0 and n_k % tk == 0 and tk % LANES == 0 and tq % rb == 0
    nqt = n_q // tq
    in_specs = [pl.BlockSpec((tq, qw), lambda b, g, i: (b * nqt + i, q_blk + g))]
    args = [q_arr]
    for k_arr, k_blk, v_arr, v_blk, nk in srcs:
        in_specs.append(pl.BlockSpec((nk, qw), lambda b, g, i, k_blk=k_blk: (b, k_blk + g)))
        in_specs.append(pl.BlockSpec((nk, LANES), lambda b, g, i, v_blk=v_blk: (b, v_blk + g)))
        args += [k_arr, v_arr]
    lambda_init = 0.0
    if diff is not None:
        lam_params, subln, lambda_init = diff
        in_specs.append(pl.BlockSpec((4, DA_HEAD_DIM), lambda b, g, i: (0, 0)))
        in_specs.append(pl.BlockSpec((1, LANES), lambda b, g, i: (0, 0)))
        args += [lam_params, subln.reshape(1, LANES)]
    kern = functools.partial(_attn_kernel, n_src=len(srcs), tk=tk, rb=rb, slice_mode=slice_mode,
                             sub_width=sub_width, diff_mode=diff is not None, lambda_init=lambda_init)
    scratch = 2 * ([pltpu.VMEM((tq, LANES), BF16)] + 2 * [pltpu.VMEM((tq, tk), F32)] + 2 * [pltpu.VMEM((tq, tk), BF16)]
                   + [pltpu.VMEM((tq, LANES), F32)] + 2 * [pltpu.VMEM((tq, LANES), F32)]
                   + [pltpu.VMEM((tq, 2 * LANES), F32)])
    return pl.pallas_call(
        kern, grid=(batch, groups, nqt), in_specs=in_specs,
        out_specs=pl.BlockSpec((tq, LANES), lambda b, g, i: (b * nqt + i, g)),
        out_shape=jax.ShapeDtypeStruct((batch * n_q, groups * LANES), BF16), scratch_shapes=scratch,
        compiler_params=_cparams(("parallel", "parallel", "arbitrary")), name=name)(*args)


def _na_kernel(q_ref, kc_ref, vc_ref, kl_ref, vl_ref, bias_ref, o_ref, s0_scr, s1_scr, p0_scr, p1_scr, *, n_rows, rb):
    i = pl.program_id(2)
    k0 = jnp.clip(i * NA_Q_ROWS - NA_WIN_ROWS // 2, 0, n_rows - NA_K_ROWS)
    start = pl.multiple_of(k0 * GRID_W, GRID_W * 4)
    nkw = NA_K_ROWS * GRID_W
    kw = kl_ref[pl.ds(start, nkw), :]
    kc = kc_ref[...]
    q = q_ref[...]
    tq = q.shape[0]
    n_all = nkw + kc.shape[0]
    lane = lax.broadcasted_iota(jnp.int32, (1, LANES), 1)
    s_scr, p_scr = (s0_scr, s1_scr), (p0_scr, p1_scr)
    dn = (((1,), (1,)), ((), ()))
    for s in range(2):
        qs = jnp.where((lane < LANES // 2) if s == 0 else (lane >= LANES // 2), q, jnp.zeros_like(q))
        s_scr[s][:, :nkw] = lax.dot_general(qs, kw, dn, preferred_element_type=F32) + bias_ref[s]
        s_scr[s][:, nkw:] = lax.dot_general(qs, kc, dn, preferred_element_type=F32)
    for s in range(2):
        for r in range(tq // rb):
            rows = slice(r * rb, (r + 1) * rb)
            sb = s_scr[s][rows, :]
            p_scr[s][rows, :] = jnp.exp2(sb - jnp.max(sb, axis=-1, keepdims=True)).astype(BF16)
    v_all = jnp.concatenate([vl_ref[pl.ds(start, nkw), :], vc_ref[...]], axis=0)
    vext = jnp.concatenate([v_all, jnp.ones((n_all, LANES), BF16)], axis=1)
    outs = []
    for s in range(2):
        pv = jnp.dot(p_scr[s][...], vext, preferred_element_type=F32)
        outs.append(pv[:, :LANES] / pv[:, LANES:])
    o_ref[...] = jnp.where(lane < LANES // 2, outs[0], outs[1]).astype(o_ref.dtype)


def _na_bias_tables(rpb, n_rows):
    nb = n_rows // NA_Q_ROWS
    assert nb >= 3
    dc = np.arange(GRID_W)[None, :] - np.arange(GRID_W)[:, None] + NA_WIN_COLS - 1
    onehot = (dc[None] == np.arange(2 * NA_WIN_COLS - 1)[:, None, None]).astype(np.float32)
    toep = jnp.einsum('hrd,dqk->hrqk', rpb.astype(F32), jnp.asarray(onehot), precision=lax.Precision.HIGHEST)
    cols = np.arange(GRID_W)
    col_start = np.clip(cols - NA_WIN_COLS // 2, 0, GRID_W - NA_WIN_COLS)
    col_ok = (cols[None, :] >= col_start[:, None]) & (cols[None, :] < col_start[:, None] + NA_WIN_COLS)
    ridx = np.zeros((3, NA_Q_ROWS, NA_K_ROWS), np.int32)
    ok = np.zeros((3, NA_Q_ROWS, NA_K_ROWS), bool)
    for var, blk in enumerate((0, 1, nb - 1)):
        r_first = blk * NA_Q_ROWS
        k_first = int(np.clip(r_first - NA_WIN_ROWS // 2, 0, n_rows - NA_K_ROWS))
        for a in range(NA_Q_ROWS):
            qr = r_first + a
            r0 = int(np.clip(qr - NA_WIN_ROWS // 2, 0, n_rows - NA_WIN_ROWS))
            for kb in range(NA_K_ROWS):
                kr = k_first + kb
                if r0 <= kr < r0 + NA_WIN_ROWS:
                    ok[var, a, kb] = True
                    ridx[var, a, kb] = kr - qr + NA_WIN_ROWS - 1
    tab = toep[:, ridx]
    tab = jnp.transpose(tab, (0, 1, 2, 4, 3, 5))
    mask = ok[:, :, None, :, None] & col_ok[None, None, :, None, :]
    tab = jnp.where(jnp.asarray(mask)[None], tab, MASK_VALUE)
    h = rpb.shape[0]
    return tab.reshape(h, 3, NA_Q_ROWS * GRID_W, NA_K_ROWS * GRID_W)


def _na_attention(qkv_lat, qkv_ctx, bias_tab, *, batch, n_lat, n_ctx):
    d = NA_HEADS * NA_HEAD_DIM
    groups = d // LANES
    n_rows = n_lat // GRID_W
    nb = n_rows // NA_Q_ROWS
    tq = NA_Q_ROWS * GRID_W

    def var_of(i):
        return jnp.where(i == 0, 0, jnp.where(i == nb - 1, 2, 1))

    in_specs = [
        pl.BlockSpec((tq, LANES), lambda b, g, i: (b * nb + i, g)),
        pl.BlockSpec((n_ctx, LANES), lambda b, g, i: (b, groups + g)),
        pl.BlockSpec((n_ctx, LANES), lambda b, g, i: (b, 2 * groups + g)),
        pl.BlockSpec((n_lat, LANES), lambda b, g, i: (b, groups + g)),
        pl.BlockSpec((n_lat, LANES), lambda b, g, i: (b, 2 * groups + g)),
        pl.BlockSpec((2, None, tq, NA_K_ROWS * GRID_W), lambda b, g, i: (g, var_of(i), 0, 0)),
    ]
    n_all = NA_K_ROWS * GRID_W + n_ctx
    return pl.pallas_call(
        functools.partial(_na_kernel, n_rows=n_rows, rb=16), grid=(batch, groups, nb), in_specs=in_specs,
        out_specs=pl.BlockSpec((tq, LANES), lambda b, g, i: (b * nb + i, g)),
        out_shape=jax.ShapeDtypeStruct((batch * n_lat, d), BF16),
        scratch_shapes=2 * [pltpu.VMEM((tq, n_all), F32)] + 2 * [pltpu.VMEM((tq, n_all), BF16)],
        compiler_params=_cparams(("parallel", "parallel", "arbitrary")), name="na_attn",
    )(qkv_lat, qkv_ctx, qkv_ctx, qkv_lat, qkv_lat, bias_tab)


def _dft_kernel(c_ref, ms_ref, ab_ref, mid_ref, o_ref, acc_ref, *, scale):
    n = pl.program_id(2)
    d = o_ref.shape[1]
    part = (jnp.dot(c_ref[...], ab_ref[:, :d], preferred_element_type=F32)
            + jnp.dot(ms_ref[...], ab_ref[:, d:], preferred_element_type=F32))

    @pl.when(n == 0)
    def _():
        acc_ref[...] = part

    @pl.when(n > 0)
    def _():
        acc_ref[...] += part

    @pl.when(n == pl.num_programs(2) - 1)
    def _():
        tk = o_ref.shape[0]
        k = pl.program_id(1) * tk + lax.broadcasted_iota(jnp.int32, (tk, 1), 0)
        sign = jnp.where(k % 2 == 0, 1.0, -1.0)
        o_ref[...] = ((acc_ref[...] + sign * mid_ref[...].astype(F32)) * scale).astype(o_ref.dtype)


def _fold_kernel(x_ref, src_ref, edge_ref, perm_ref, sign_ref, o_ref):
    i = pl.program_id(1)
    mirror = jnp.dot(perm_ref[...], src_ref[...], preferred_element_type=F32)
    first = lax.broadcasted_iota(jnp.int32, (mirror.shape[0], 1), 0) == 0
    edge = jnp.where(i > 0, edge_ref[0:1, :].astype(F32), 0.0)
    mirror = jnp.where(first, edge, mirror)
    o_ref[...] = (x_ref[...].astype(F32) + sign_ref[...] * mirror).astype(o_ref.dtype)


def _dft_fold(ab, *, batch, n_tok, d, name):
    half = n_tok // 2
    t = min(512, half)
    nt = n_tok // t
    perm = np.zeros((t, t), np.float32)
    perm[np.arange(1, t), t - np.arange(1, t)] = 1.0
    sign = jnp.concatenate([jnp.ones((d,), F32), -jnp.ones((d,), F32)]).reshape(1, 2 * d)
    edge_blocks = t // 8
    return pl.pallas_call(
        _fold_kernel, grid=(batch, nt // 2),
        in_specs=[pl.BlockSpec((t, 2 * d), lambda b, i: (b * nt + i, 0)),
                  pl.BlockSpec((t, 2 * d), lambda b, i: (b * nt + nt - 1 - i, 0)),
                  pl.BlockSpec((8, 2 * d), lambda b, i: ((b * nt + jnp.maximum(nt - i, 1) % nt) * edge_blocks, 0)),
                  pl.BlockSpec((t, t), lambda b, i: (0, 0)),
                  pl.BlockSpec((1, 2 * d), lambda b, i: (0, 0))],
        out_specs=pl.BlockSpec((t, 2 * d), lambda b, i: (b * (nt // 2) + i, 0)),
        out_shape=jax.ShapeDtypeStruct((batch * half, 2 * d), BF16),
        compiler_params=_cparams(("parallel", "parallel")), name=name)(ab, ab, ab, jnp.asarray(perm, BF16), sign)


def _pos_dft(cos_t, msin_t, ab, *, batch, n_tok, d, scale, name):
    half = n_tok // 2
    folded = _dft_fold(ab, batch=batch, n_tok=n_tok, d=d, name=name + "_fold")
    mid = ab.reshape(batch, n_tok, 2 * d)[:, half:half + 1, :d]
    tk = min(1024, n_tok)
    tn = min(1024, half)
    nk, nn = n_tok // tk, half // tn
    return pl.pallas_call(
        functools.partial(_dft_kernel, scale=scale), grid=(batch, nk, nn),
        in_specs=[pl.BlockSpec((tk, tn), lambda b, k, n: (k, n)),
                  pl.BlockSpec((tk, tn), lambda b, k, n: (k, n)),
                  pl.BlockSpec((tn, 2 * d), lambda b, k, n: (b * nn + n, 0)),
                  pl.BlockSpec((None, 1, d), lambda b, k, n: (b, 0, 0))],
        out_specs=pl.BlockSpec((tk, d), lambda b, k, n: (b * nk + k, 0)),
        out_shape=jax.ShapeDtypeStruct((batch * n_tok, d), BF16),
        scratch_shapes=[pltpu.VMEM((tk, d), F32)],
        compiler_params=_cparams(("parallel", "parallel", "arbitrary")), name=name)(cos_t, msin_t, folded, mid)


def _dft_tables(n):
    split = 64 if n % 64 == 0 and n > 64 else 1
    m = jnp.arange(n // 2, dtype=jnp.int32)[None, :]

    def phase(kvals):
        ang = ((kvals[:, None] * m) % n).astype(F32) * (2.0 * math.pi / n)
        return jnp.cos(ang), jnp.sin(ang)

    c1, s1 = phase(jnp.arange(n // split, dtype=jnp.int32) * split)
    c2, s2 = phase(jnp.arange(split, dtype=jnp.int32))
    c = c1[:, None, :] * c2[None, :, :] - s1[:, None, :] * s2[None, :, :]
    s = s1[:, None, :] * c2[None, :, :] + c1[:, None, :] * s2[None, :, :]
    return c.reshape(n, n // 2).astype(BF16), (-s).reshape(n, n // 2).astype(BF16)


def _channel_dft_weight(d):
    cg = d // FN_GROUPS
    idx = np.arange(cg)
    ang = 2.0 * np.pi * ((idx[:, None] * idx[None, :]) % cg) / cg
    eye = np.eye(FN_GROUPS)
    return jnp.asarray(np.concatenate([np.kron(eye, np.cos(ang)), np.kron(eye, np.sin(ang))], axis=1), F32)


def _router_kernel(x_ref, gain_ref, sc_ref, sh_ref, wr_ref, h_ref, aff_ref):
    x = x_ref[...]
    ms = jnp.mean(x * x, axis=-1, keepdims=True)
    h = x * lax.rsqrt(ms + NORM_EPS) * gain_ref[...]
    h = h * (1.0 + sc_ref[...]) + sh_ref[...]
    h_ref[...] = h.astype(h_ref.dtype)
    w = wr_ref[...]
    h_hi, w_hi = h.astype(BF16), w.astype(BF16)
    h_lo, w_lo = (h - h_hi.astype(F32)).astype(BF16), (w - w_hi.astype(F32)).astype(BF16)
    logits = (jnp.dot(h_hi, w_hi, preferred_element_type=F32) + jnp.dot(h_lo, w_hi, preferred_element_type=F32)
              + jnp.dot(h_hi, w_lo, preferred_element_type=F32))
    e = jnp.exp(logits - jnp.max(logits, axis=-1, keepdims=True))
    aff_ref[...] = e / jnp.sum(e, axis=-1, keepdims=True)


def _norm_router(x, gain, sc, sh, w_router, layer, *, rows_per_batch, tm=512):
    m, d = x.shape
    e = w_router.shape[-1]
    tm = min(tm, rows_per_batch)
    tpb = rows_per_batch // tm
    return pl.pallas_call(
        _router_kernel, grid=(m // tm,),
        in_specs=[pl.BlockSpec((tm, d), lambda i: (i, 0)),
                  pl.BlockSpec((1, d), lambda i: (0, 0)),
                  pl.BlockSpec((None, 1, d), lambda i: (i // tpb, 0, 0)),
                  pl.BlockSpec((None, 1, d), lambda i: (i // tpb, 0, 0)),
                  pl.BlockSpec((None, d, e), lambda i: (layer, 0, 0))],
        out_specs=[pl.BlockSpec((tm, d), lambda i: (i, 0)), pl.BlockSpec((tm, e), lambda i: (i, 0))],
        out_shape=[jax.ShapeDtypeStruct((m, d), BF16), jax.ShapeDtypeStruct((m, e), F32)],
        compiler_params=_cparams(("parallel",)), name="norm_router")(x, gain.reshape(1, d), sc, sh, w_router)


FFN_PIECE = 256


def _ffn_kernel(*refs, n_streams):
    x_refs = refs[:n_streams]
    wg_ref, wu_ref, wd_ref = refs[n_streams:n_streams + 3]
    gate_refs = refs[n_streams + 3:2 * n_streams + 3]
    o_refs = refs[2 * n_streams + 3:3 * n_streams + 3]
    acc_ref = refs[3 * n_streams + 3]
    f = pl.program_id(2)
    x = x_refs[0][...] if n_streams == 1 else jnp.concatenate([r[...] for r in x_refs], axis=0)
    @pl.when(f == 0)
    def _():
        acc_ref[...] = jnp.zeros(acc_ref.shape, F32)

    tf = wg_ref.shape[1]
    acts = []
    for c0 in range(0, tf, FFN_PIECE):
        a = jnp.dot(x, wg_ref[:, c0:c0 + FFN_PIECE].astype(BF16), preferred_element_type=F32)
        u = jnp.dot(x, wu_ref[:, c0:c0 + FFN_PIECE].astype(BF16), preferred_element_type=F32)
        acts.append(((a * (1.0 / (1.0 + jnp.exp(-a)))) * u).astype(BF16))
    d = wd_ref.shape[1]
    for n0 in range(0, d, FFN_PIECE):
        y = None
        for i, act in enumerate(acts):
            part = jnp.dot(act, wd_ref[i * FFN_PIECE:(i + 1) * FFN_PIECE, n0:n0 + FFN_PIECE].astype(BF16),
                           preferred_element_type=F32)
            y = part if y is None else y + part
        acc_ref[:, n0:n0 + FFN_PIECE] += y

    @pl.when(f == pl.num_programs(2) - 1)
    def _():
        row = 0
        for o_ref, g_ref in zip(o_refs, gate_refs):
            c = o_ref.shape[0]
            o_ref[...] = acc_ref[row:row + c, :] * g_ref[...]
            row += c


def _moe_ffn(xs, gates, w_gate, w_up, w_down, layer, *, tf=512):
    b, e, _, d = xs[0].shape
    f = w_gate.shape[-1]
    tf = min(tf, f)
    slot = lambda c, w: pl.BlockSpec((None, None, c, w), lambda ei, bi, fi: (bi, ei, 0, 0))
    caps = [t.shape[2] for t in xs]
    in_specs = ([slot(c, d) for c in caps]
                + [pl.BlockSpec((None, None, d, tf), lambda ei, bi, fi: (layer, ei, 0, fi)),
                   pl.BlockSpec((None, None, d, tf), lambda ei, bi, fi: (layer, ei, 0, fi)),
                   pl.BlockSpec((None, None, tf, d), lambda ei, bi, fi: (layer, ei, fi, 0))]
                + [slot(c, 1) for c in caps])
    return pl.pallas_call(
        functools.partial(_ffn_kernel, n_streams=len(xs)), grid=(e, b, f // tf), in_specs=in_specs,
        out_specs=[slot(c, d) for c in caps],
        out_shape=[jax.ShapeDtypeStruct((b, e, c, d), F32) for c in caps],
        scratch_shapes=[pltpu.VMEM((sum(caps), d), F32)],
        compiler_params=_cparams(("parallel", "parallel", "arbitrary")), name="moe_ffn",
    )(*xs, w_gate, w_up, w_down, *gates)


def _count_upto(flags):
    lead, n = flags.shape[:-1], flags.shape[-1]
    blocks = flags.reshape(*lead, n // LANES, LANES).astype(BF16)
    tri = jnp.asarray(np.triu(np.ones((LANES, LANES), np.float32))).astype(BF16)
    inside = jnp.einsum('...i,ij->...j', blocks, tri, preferred_element_type=F32)
    totals = inside[..., -1]
    before = jnp.cumsum(totals, axis=-1) - totals
    return (inside + before[..., None]).reshape(*lead, n).astype(jnp.int32)


def _route(aff, h, batch, n_tok):
    e = aff.shape[-1]
    cap = EC_CAPACITY_FACTOR * n_tok // e
    aff_t = jnp.swapaxes(aff.reshape(batch, n_tok, e), 1, 2)
    gate, idx = lax.top_k(aff_t, cap)
    thr = gate[..., -1:]
    above, tied = aff_t > thr, aff_t == thr
    n_tied = cap - jnp.sum(above, axis=-1, keepdims=True, dtype=jnp.int32)
    chosen = above | (tied & (_count_upto(tied) <= n_tied))
    pos = jnp.where(chosen, _count_upto(chosen) - 1, -1)
    pos = jnp.swapaxes(pos, 1, 2).reshape(batch * n_tok, e)
    idx, gate = lax.sort((idx, gate), dimension=2, num_keys=1)
    xs = jax.vmap(lambda hb, ib: hb[ib])(h.reshape(batch, n_tok, -1), idx)
    return gate, idx, xs, pos


COMBINE_TOKENS = 256
COMBINE_WINDOW = 64


def _combine_kernel(lo_ref, nr_ref, x_ref, g_ref, pos_ref, y_hbm, o_ref, stage, sem, acc, *, n_tiles, n_total, n_exp,
                    cap):
    w = COMBINE_WINDOW
    per = LANES // w
    tile = pl.program_id(0) * n_tiles + pl.program_id(1)
    slot = tile % 2
    lane = lax.broadcasted_iota(jnp.int32, (1, LANES), 1)

    def window(tl, e, r):
        first = lo_ref[tl * n_exp + e] + r * w
        return first, pl.multiple_of(jnp.minimum(first, cap - w), 8)

    def copy(tl, e, start, sl):
        return pltpu.make_async_copy(y_hbm.at[tl // n_tiles, e, pl.ds(start, w), :],
                                     stage.at[sl, pl.ds(e * w, w), :], sem.at[sl, e])

    def fetch(tl, r, sl):
        for e in range(n_exp):
            copy(tl, e, window(tl, e, r)[1], sl).start()

    def add_round(r):
        pos = pos_ref[...]
        blocks = []
        for e0 in range(0, n_exp, per):
            first, start = window(tile, e0, r)
            col = pos[:, e0:e0 + 1]
            for k in range(1, per):
                f_k, s_k = window(tile, e0 + k, r)
                here = lane >= k * w
                first, start = jnp.where(here, f_k, first), jnp.where(here, s_k, start)
                col = jnp.where(here, pos[:, e0 + k:e0 + k + 1], col)
            hit = jnp.logical_and(col - start == lane % w, col >= first)
            blocks.append(jnp.where(hit, 1.0, 0.0).astype(BF16))
        onehot = jnp.concatenate(blocks, axis=1)
        for e in range(n_exp):
            copy(tile, e, 0, slot).wait()
        y = stage[slot]
        y_hi = y.astype(BF16)
        y_lo = (y - y_hi.astype(F32)).astype(BF16)
        acc[...] += (jnp.dot(onehot, y_hi, preferred_element_type=F32)
                     + jnp.dot(onehot, y_lo, preferred_element_type=F32))

    @pl.when(tile == 0)
    def _():
        fetch(tile, 0, slot)

    @pl.when(tile + 1 < n_total)
    def _():
        fetch(tile + 1, 0, 1 - slot)

    acc[...] = jnp.zeros(acc.shape, F32)
    add_round(0)

    @pl.loop(1, nr_ref[tile])
    def _(r):
        fetch(tile, r, slot)
        add_round(r)

    o_ref[...] = x_ref[...] + g_ref[...] * acc[...]


def _combine_lat(x, g2, y, idx, pos, batch, n_tok):
    d = x.shape[-1]
    _, n_exp, cap = idx.shape
    tt, w = COMBINE_TOKENS, COMBINE_WINDOW
    assert n_tok % tt == 0 and cap % 8 == 0 and cap >= w
    nt = n_tok // tt
    bounds = jnp.arange(nt + 1, dtype=jnp.int32) * tt
    below = jnp.sum((idx[..., None] < bounds).astype(jnp.int32), axis=2)
    lo = (below[..., :-1] // 8) * 8
    rounds = jnp.maximum(jnp.max((below[..., 1:] - lo + w - 1) // w, axis=1), 1)
    lo_flat = jnp.transpose(lo, (0, 2, 1)).reshape(-1)
    grid_spec = pltpu.PrefetchScalarGridSpec(
        num_scalar_prefetch=2, grid=(batch, nt),
        in_specs=[pl.BlockSpec((tt, d), lambda b, t, lo_r, nr_r: (b * nt + t, 0)),
                  pl.BlockSpec((None, 1, d), lambda b, t, lo_r, nr_r: (b, 0, 0)),
                  pl.BlockSpec((tt, n_exp), lambda b, t, lo_r, nr_r: (b * nt + t, 0)),
                  pl.BlockSpec(memory_space=pl.ANY)],
        out_specs=pl.BlockSpec((tt, d), lambda b, t, lo_r, nr_r: (b * nt + t, 0)),
        scratch_shapes=[pltpu.VMEM((2, n_exp * w, d), F32), pltpu.SemaphoreType.DMA((2, n_exp)),
                        pltpu.VMEM((tt, d), F32)])
    return pl.pallas_call(
        functools.partial(_combine_kernel, n_tiles=nt, n_total=batch * nt, n_exp=n_exp, cap=cap),
        grid_spec=grid_spec, out_shape=jax.ShapeDtypeStruct(x.shape, F32),
        compiler_params=_cparams(("arbitrary", "arbitrary")), name="moe_combine",
    )(lo_flat, rounds.reshape(-1), x, g2, pos, y)


def _combine(x, g2, y, idx, batch, n_tok):
    d = x.shape[-1]
    upd = jax.vmap(lambda ib, yb: jnp.zeros((n_tok, d), F32).at[ib.reshape(-1)].add(yb.reshape(-1, d)))(idx, y)
    return (x.reshape(batch, n_tok, d) + g2 * upd).reshape(batch * n_tok, d)


def _final_norm_kernel(x_ref, g_ref, o_ref):
    x = x_ref[...]
    ms = jnp.mean(x * x, axis=-1, keepdims=True)
    o_ref[...] = x * lax.rsqrt(ms + NORM_EPS) * g_ref[...]


def _final_norm(x, g, tm=1024):
    m, d = x.shape
    tm = min(tm, m)
    return pl.pallas_call(
        _final_norm_kernel, grid=(m // tm,),
        in_specs=[pl.BlockSpec((tm, d), lambda i: (i, 0)), pl.BlockSpec((1, d), lambda i: (0, 0))],
        out_specs=pl.BlockSpec((tm, d), lambda i: (i, 0)), out_shape=jax.ShapeDtypeStruct((m, d), F32),
        compiler_params=_cparams(("parallel",)), name="final_norm")(x, g.reshape(1, d))


def _rope_tables(n_tok, rot_dim, pair_of_lane):
    t = jnp.arange(n_tok)
    rows = (t // GRID_W).astype(F32)
    cols = (t % GRID_W).astype(F32)
    n_freq = rot_dim // 4
    inv_freq = ROPE_BASE ** (-jnp.arange(n_freq, dtype=F32) / n_freq)
    ang = jnp.concatenate([rows[:, None] * inv_freq, cols[:, None] * inv_freq], axis=-1)
    cos, sin = jnp.cos(ang), jnp.sin(ang)
    pair = np.concatenate([pair_of_lane, pair_of_lane])
    used = jnp.asarray(pair >= 0)[None]
    sign = jnp.asarray(np.where(np.arange(LANES) < LANES // 2, -1.0, 1.0).astype(np.float32))[None]
    cos_l = jnp.where(used, cos[:, np.maximum(pair, 0)], 1.0)
    sin_l = jnp.where(used, sin[:, np.maximum(pair, 0)] * sign, 0.0)
    return cos_l, sin_l


def kernel(x, c, ctx, c_ctx, ada_w, ada_b, norm_mix, norm_ffn, norm_final, da_w_qkv, da_w_o, da_lambda_q1,
           da_lambda_k1, da_lambda_q2, da_lambda_k2, da_subln, fn_w_o, na_w_qkv, na_w_o, na_rpb, mla_w_dq,
           mla_q_norm, mla_w_uq, mla_w_dkv, mla_kv_norm, mla_w_uk, mla_w_uv, mla_w_o, moe_w_router, moe_w_gate,
           moe_w_up, moe_w_down):
    batch, n_lat, d = x.shape
    n_ctx = ctx.shape[1]
    depth = ada_w.shape[0]
    n_mixers = 4
    xl = x.reshape(batch * n_lat, d)
    xc = ctx.reshape(batch * n_ctx, d)

    cond = jnp.concatenate([c, c_ctx[None], jnp.zeros((8 - batch - 1, d), F32)], axis=0)
    mods = [
        _fused_mm(cond, ada_w, w_index=i, bias=ada_b[i], silu_in=True, precision=lax.Precision.HIGHEST, tn=1024,
                  name="ada_mod")
        for i in range(depth)
    ]

    def chunks(i):
        lat = [mods[i][:batch, k * d:(k + 1) * d].reshape(batch, 1, d) for k in range(ADA_CHUNKS)]
        cx = [mods[i][batch:batch + 1, k * d:(k + 1) * d].reshape(1, 1, d) for k in range(ADA_CHUNKS)]
        return lat, cx

    for i in range(depth):
        kind, j = i % n_mixers, i // n_mixers
        keep_ctx = i < depth - 1
        (sh1, sc1, g1, sh2, sc2, g2), (csh1, csc1, cg1, csh2, csc2, cg2) = chunks(i)
        lat_in = dict(gain=norm_mix[i], mod=(sc1, sh1), rows_per_batch=n_lat)
        ctx_in = dict(gain=norm_mix[i], mod=(csc1, csh1), rows_per_batch=batch * n_ctx, tm=n_ctx)

        if kind == 0:
            lambda_init = 0.8 - 0.6 * math.exp(-0.3 * i)
            scale = jnp.concatenate([jnp.full((d,), LOG2E * DA_HEAD_DIM ** -0.5, F32), jnp.ones((2 * d,), F32)])
            groups = d // LANES
            half = DA_HEAD_DIM // 2
            w = da_w_qkv[j]

            def reorder(cols):
                return cols.reshape(d, groups, 2, 2, half).transpose(0, 1, 3, 2, 4).reshape(d, d)

            w = jnp.concatenate([reorder(w[:, :d]), reorder(w[:, d:2 * d]), w[:, 2 * d:]], axis=1)
            rope = _rope_tables(n_lat, DA_HEAD_DIM, np.arange(LANES // 2) % half)
            qkv_l = _fused_mm(xl, w, colscale=scale, rope=rope, rope_cols=2 * d, out_dtype=BF16, name="da_qkv_lat",
                              **lat_in)
            qkv_c = _fused_mm(xc, w, colscale=scale, out_dtype=BF16, name="da_qkv_ctx", **ctx_in)
            lam_params = jnp.stack([da_lambda_q1[j], da_lambda_k1[j], da_lambda_q2[j], da_lambda_k2[j]])
            diff = (lam_params, da_subln[j], lambda_init)
            ctx_kv = (qkv_c, groups, qkv_c, 2 * groups, n_ctx)
            lat_kv = (qkv_l, groups, qkv_l, 2 * groups, n_lat)
            o_l = _attention(qkv_l, 0, [ctx_kv, lat_kv], batch=batch, n_q=n_lat, groups=groups, sub_width=half,
                             diff=diff, name="da_attn_lat")
            o_c = _attention(qkv_c, 0, [ctx_kv], batch=batch, n_q=n_ctx, groups=groups, sub_width=half, diff=diff,
                             name="da_attn_ctx") if keep_ctx else None
            w_o = da_w_o
        elif kind == 1:
            w_cd = _channel_dft_weight(d)
            ab_l = _fused_mm(xl, w_cd, out_dtype=BF16, name="fn_chan_lat", **lat_in)
            scale_l = 1.0 / math.sqrt(n_lat * (d // FN_GROUPS))
            o_l = _pos_dft(*_dft_tables(n_lat), ab_l, batch=batch, n_tok=n_lat, d=d, scale=scale_l, name="fn_pos_lat")
            o_c = None
            if keep_ctx:
                ab_c = _fused_mm(xc, w_cd, out_dtype=BF16, name="fn_chan_ctx", **ctx_in)
                scale_c = 1.0 / math.sqrt(n_ctx * (d // FN_GROUPS))
                o_c = _pos_dft(*_dft_tables(n_ctx), ab_c, batch=batch, n_tok=n_ctx, d=d, scale=scale_c,
                               name="fn_pos_ctx")
            w_o = fn_w_o
        elif kind == 2:
            scale = jnp.concatenate([jnp.full((d,), LOG2E * NA_HEAD_DIM ** -0.5, F32), jnp.ones((2 * d,), F32)])
            qkv_l = _fused_mm(xl, na_w_qkv, w_index=j, colscale=scale, out_dtype=BF16, name="na_qkv_lat", **lat_in)
            qkv_c = _fused_mm(xc, na_w_qkv, w_index=j, colscale=scale, out_dtype=BF16, name="na_qkv_ctx", **ctx_in)
            groups = d // LANES
            bias_tab = _na_bias_tables(na_rpb[j] * LOG2E, n_lat // GRID_W)
            o_l = _na_attention(qkv_l, qkv_c, bias_tab, batch=batch, n_lat=n_lat, n_ctx=n_ctx)
            o_c = _attention(qkv_c, 0, [(qkv_c, groups, qkv_c, 2 * groups, n_ctx)], batch=batch, n_q=n_ctx,
                             groups=groups, name="na_attn_ctx") if keep_ctx else None
            w_o = na_w_o
        else:
            q_rank = mla_w_dq.shape[-1]
            kv_rank = mla_w_uk.shape[-2]
            hq = MLA_NOPE_DIM + MLA_ROPE_DIM
            down_w = 3 * 384
            assert q_rank == 768 and kv_rank == 256
            w_down = jnp.concatenate([mla_w_dq[j], mla_w_dkv[j], jnp.zeros((d, down_w - q_rank - kv_rank - MLA_ROPE_DIM), F32)],
                                     axis=1)
            rh = MLA_ROPE_DIM // 2
            n_a = LANES // 2 - rh

            def head_lanes(nope, rope_part):
                z = jnp.zeros(nope.shape[:-1] + (LANES - hq,), F32)
                return jnp.concatenate([rope_part[..., :rh], nope[..., :n_a], rope_part[..., rh:], nope[..., n_a:], z],
                                       axis=-1)

            wq = mla_w_uq[j].reshape(q_rank, MLA_HEADS, hq)
            wq = head_lanes(wq[..., :MLA_NOPE_DIM], wq[..., MLA_NOPE_DIM:]).reshape(q_rank, MLA_HEADS * LANES)
            wk_nope = head_lanes(mla_w_uk[j].reshape(kv_rank, MLA_HEADS, MLA_NOPE_DIM),
                                 jnp.zeros((kv_rank, MLA_HEADS, MLA_ROPE_DIM), F32))
            eye = jnp.broadcast_to(jnp.eye(MLA_ROPE_DIM, dtype=F32)[:, None, :], (MLA_ROPE_DIM, MLA_HEADS, MLA_ROPE_DIM))
            wk_rope = head_lanes(jnp.zeros((MLA_ROPE_DIM, MLA_HEADS, MLA_NOPE_DIM), F32), eye)
            kw = MLA_HEADS * LANES
            vw = MLA_HEADS * MLA_V_DIM
            pad_rows = 384 - kv_rank - MLA_ROPE_DIM
            w_kv = jnp.concatenate([
                jnp.concatenate([wk_nope.reshape(kv_rank, kw), mla_w_uv[j]], axis=1),
                jnp.concatenate([wk_rope.reshape(MLA_ROPE_DIM, kw), jnp.zeros((MLA_ROPE_DIM, vw), F32)], axis=1),
                jnp.zeros((pad_rows, kw + vw), F32)], axis=0)
            kv_gain = jnp.concatenate([mla_kv_norm[j], jnp.ones((384 - kv_rank,), F32)])
            rope = _rope_tables(n_lat, MLA_ROPE_DIM, np.where(np.arange(LANES // 2) < rh, np.arange(LANES // 2), -1))
            qscale = jnp.full((kw,), LOG2E * hq ** -0.5, F32)

            def project(xs, inp, n_tok, rope_tabs, tag):
                low = _fused_mm(xs, w_down, tn=384, name="mla_down_" + tag, **inp)
                rp = dict(rope=rope_tabs) if rope_tabs is not None else {}
                tmx = inp.get("tm", 1024)
                q = _fused_mm(low, wq, a_cols=(0, q_rank), gain=mla_q_norm[j], colscale=qscale, rope_cols=kw,
                              rows_per_batch=inp["rows_per_batch"], tm=tmx, out_dtype=BF16, name="mla_q_" + tag, **rp)
                kv = _fused_mm(low, w_kv, a_cols=(q_rank, 384), gain=kv_gain, norm_cols=kv_rank, rope_cols=kw,
                               rows_per_batch=inp["rows_per_batch"], tm=tmx, out_dtype=BF16, name="mla_kv_" + tag, **rp)
                return q, kv

            q_l, kv_l = project(xl, lat_in, n_lat, rope, "lat")
            q_c, kv_c = project(xc, ctx_in, n_ctx, None, "ctx")
            groups = MLA_HEADS // 2
            ctx_kv = (kv_c, 0, kv_c, 2 * groups, n_ctx)
            lat_kv = (kv_l, 0, kv_l, 2 * groups, n_lat)
            o_l = _attention(q_l, 0, [ctx_kv, lat_kv], batch=batch, n_q=n_lat, groups=groups, slice_mode=True,
                             name="mla_attn_lat")
            o_c = _attention(q_c, 0, [ctx_kv], batch=batch, n_q=n_ctx, groups=groups, slice_mode=True,
                             name="mla_attn_ctx") if keep_ctx else None
            w_o = mla_w_o

        xl = _fused_mm(o_l, w_o, w_index=j, res=xl, gate=g1, rows_per_batch=n_lat, name="mix_out_lat")
        if keep_ctx:
            xc = _fused_mm(o_c, w_o, w_index=j, res=xc, gate=jnp.broadcast_to(cg1, (1, 1, d)),
                           rows_per_batch=batch * n_ctx, tm=n_ctx, name="mix_out_ctx")

        h_l, aff_l = _norm_router(xl, norm_ffn[i], sc2, sh2, moe_w_router, i, rows_per_batch=n_lat)
        gate_l, idx_l, xs_l, pos_l = _route(aff_l, h_l, batch, n_lat)
        xs, gates = [xs_l], [gate_l[..., None]]
        if keep_ctx:
            h_c, aff_c = _norm_router(xc, norm_ffn[i], csc2, csh2, moe_w_router, i, rows_per_batch=batch * n_ctx,
                                      tm=n_ctx)
            gate_c, idx_c, xs_c, _ = _route(aff_c, h_c, batch, n_ctx)
            xs.append(xs_c)
            gates.append(gate_c[..., None])
        ys = _moe_ffn(xs, gates, moe_w_gate, moe_w_up, moe_w_down, i)
        xl = _combine_lat(xl, g2, ys[0], idx_l, pos_l, batch, n_lat)
        if keep_ctx:
            xc = _combine(xc, cg2, ys[1], idx_c, batch, n_ctx)

    return _final_norm(xl, norm_final).reshape(batch, n_lat, d)
```

```python
import functools
import math

import numpy as np
import jax
import jax.numpy as jnp
from jax import lax
from jax.experimental import pallas as pl
from jax.experimental.pallas import tpu as pltpu

F32 = jnp.float32
BF16 = jnp.bfloat16

LANES = 128
VMEM_LIMIT_BYTES = 56 * 1024 * 1024

GRID_W = 64
ROPE_BASE = 10000.0
NORM_EPS = 1e-6
ADA_CHUNKS = 6
DA_HEAD_DIM = 64
NA_HEADS = 16
NA_HEAD_DIM = 64
NA_WIN_ROWS = 8
NA_WIN_COLS = 16
NA_Q_ROWS = 8
NA_K_ROWS = 16
MLA_HEADS = 16
MLA_NOPE_DIM = 64
MLA_ROPE_DIM = 32
MLA_V_DIM = 64
FN_GROUPS = 4
EC_CAPACITY_FACTOR = 2
MASK_VALUE = -1e30
LOG2E = math.log2(math.e)


def _cparams(sem):
    return pltpu.CompilerParams(dimension_semantics=sem, vmem_limit_bytes=VMEM_LIMIT_BYTES)


def _rope_lanes(acc, cos, sin):
    outs = []
    for g in range(acc.shape[1] // LANES):
        blk = acc[:, g * LANES:(g + 1) * LANES]
        outs.append(blk * cos + pltpu.roll(blk, LANES // 2, 1) * sin)
    return outs[0] if len(outs) == 1 else jnp.concatenate(outs, axis=1)


def _mm_kernel(*refs, norm_cols, has_gain, has_mod, has_colscale, rope_tiles, has_bias, has_res,
               silu_in, precision):
    it = iter(refs)
    a_ref = next(it)
    w_ref = next(it)
    gain_ref = next(it) if has_gain else None
    sc_ref, sh_ref = (next(it), next(it)) if has_mod else (None, None)
    cs_ref = next(it) if has_colscale else None
    cos_ref, sin_ref = (next(it), next(it)) if rope_tiles else (None, None)
    bias_ref = next(it) if has_bias else None
    res_ref, gate_ref = (next(it), next(it)) if has_res else (None, None)
    o_ref = next(it)
    h_scr = next(it, None)
    j = pl.program_id(1)
    hi = precision is not None

    def prologue():
        a = a_ref[...].astype(F32)
        if silu_in:
            a = a * (1.0 / (1.0 + jnp.exp(-a)))
        if has_gain:
            k = a.shape[1]
            if norm_cols < k:
                col = lax.broadcasted_iota(jnp.int32, (1, k), 1)
                sq = jnp.where(col < norm_cols, a * a, 0.0)
            else:
                sq = a * a
            ms = jnp.sum(sq, axis=-1, keepdims=True) * (1.0 / norm_cols)
            y = a * lax.rsqrt(ms + NORM_EPS) * gain_ref[...]
            a = jnp.where(col < norm_cols, y, a) if norm_cols < k else y
        if has_mod:
            a = a * (1.0 + sc_ref[...]) + sh_ref[...]
        return a if hi else a.astype(BF16)

    if h_scr is not None:
        @pl.when(j == 0)
        def _():
            h_scr[...] = prologue()
        h = h_scr[...]
    else:
        h = a_ref[...] if hi else a_ref[...].astype(BF16)
    w = w_ref[...] if hi else w_ref[...].astype(BF16)
    acc = jnp.dot(h, w, preferred_element_type=F32, precision=precision)
    if has_colscale:
        acc = acc * cs_ref[...]

    def finish(v):
        if has_bias:
            v = v + bias_ref[...]
        if has_res:
            v = res_ref[...] + gate_ref[...] * v
        o_ref[...] = v.astype(o_ref.dtype)

    if rope_tiles:
        @pl.when(j < rope_tiles)
        def _():
            finish(_rope_lanes(acc, cos_ref[...], sin_ref[...]))

        @pl.when(j >= rope_tiles)
        def _():
            finish(acc)
    else:
        finish(acc)


def _fused_mm(a, w, *, w_index=None, a_cols=None, gain=None, norm_cols=None, mod=None, colscale=None, rope=None,
              rope_cols=0, bias=None, res=None, gate=None, silu_in=False, rows_per_batch=None,
              out_dtype=F32, tm=1024, tn=1024, precision=None, name="mm"):
    m = a.shape[0]
    k0, k = a_cols if a_cols is not None else (0, a.shape[1])
    if precision is None and w.dtype != BF16:
        w = (w[w_index] if w.ndim == 3 else w).astype(BF16)
    n = w.shape[-1]
    assert w.shape[-2] == k and k0 % k == 0
    tm = min(tm, m)
    tn = min(tn, n)
    assert m % tm == 0 and n % tn == 0
    rows_per_batch = rows_per_batch or m
    assert rows_per_batch % tm == 0
    tpb = rows_per_batch // tm
    has_prologue = gain is not None or mod is not None or silu_in
    in_specs = [pl.BlockSpec((tm, k), lambda i, j: (i, k0 // k))]
    args = [a]
    if w.ndim == 3:
        in_specs.append(pl.BlockSpec((None, k, tn), lambda i, j: (w_index, 0, j)))
    else:
        in_specs.append(pl.BlockSpec((k, tn), lambda i, j: (0, j)))
    args.append(w)
    if gain is not None:
        in_specs.append(pl.BlockSpec((1, k), lambda i, j: (0, 0)))
        args.append(gain.reshape(1, k).astype(F32))
    if mod is not None:
        for t in mod:
            in_specs.append(pl.BlockSpec((None, 1, k), lambda i, j: (i // tpb, 0, 0)))
            args.append(t)
    if colscale is not None:
        in_specs.append(pl.BlockSpec((1, tn), lambda i, j: (0, j)))
        args.append(colscale.reshape(1, n).astype(F32))
    rope_tiles = 0
    if rope is not None:
        assert rope_cols % tn == 0
        rope_tiles = rope_cols // tn
        for t in rope:
            in_specs.append(pl.BlockSpec((tm, LANES), lambda i, j: (i % tpb, 0)))
            args.append(t)
    if bias is not None:
        in_specs.append(pl.BlockSpec((1, tn), lambda i, j: (0, j)))
        args.append(bias.reshape(1, n).astype(F32))
    if res is not None:
        in_specs.append(pl.BlockSpec((tm, tn), lambda i, j: (i, j)))
        args.append(res)
        in_specs.append(pl.BlockSpec((None, 1, tn), lambda i, j: (i // tpb, 0, j)))
        args.append(gate)
    scratch = [pltpu.VMEM((tm, k), F32 if precision is not None else BF16)] if has_prologue else []
    kern = functools.partial(
        _mm_kernel, norm_cols=norm_cols or k, has_gain=gain is not None, has_mod=mod is not None,
        has_colscale=colscale is not None, rope_tiles=rope_tiles, has_bias=bias is not None,
        has_res=res is not None, silu_in=silu_in, precision=precision)
    return pl.pallas_call(
        kern, grid=(m // tm, n // tn), in_specs=in_specs,
        out_specs=pl.BlockSpec((tm, tn), lambda i, j: (i, j)),
        out_shape=jax.ShapeDtypeStruct((m, n), out_dtype), scratch_shapes=scratch,
        compiler_params=_cparams(("parallel", "arbitrary")), name=name)(*args)


def _attn_kernel(*refs, n_src, tk, rb, slice_mode, sub_width, diff_mode, lambda_init):
    q_ref = refs[0]
    k_refs, v_refs = refs[1:1 + 2 * n_src:2], refs[2:2 + 2 * n_src:2]
    pos = 1 + 2 * n_src
    if diff_mode:
        lam_ref, subln_ref = refs[pos], refs[pos + 1]
        pos += 2
    o_ref = refs[pos]
    per = [refs[pos + 1 + 9 * s:pos + 10 + 9 * s] for s in range(2)]
    qm_scr = [t[0] for t in per]
    s_scr = [t[1:3] for t in per]
    p_scr = [t[3:5] for t in per]
    m_scr = [t[5] for t in per]
    a_scr = [t[6:8] for t in per]
    acc_scr = [t[8] for t in per]
    tq = q_ref.shape[0]
    q = q_ref[...]
    lane = lax.broadcasted_iota(jnp.int32, (1, LANES), 1)
    for s in range(2):
        if slice_mode:
            qm_scr[s][...] = q[:, s * LANES:(s + 1) * LANES]
        else:
            qm_scr[s][...] = jnp.where((lane // sub_width) % 2 == s, q, jnp.zeros_like(q))
        m_scr[s][...] = jnp.full(m_scr[s].shape, MASK_VALUE, F32)
        acc_scr[s][...] = jnp.zeros(acc_scr[s].shape, F32)

    lengths = [r.shape[0] for r in k_refs]
    n_chunks = sum(lengths) // tk

    def chunk_of(src_refs, c):
        pieces, first, begin = [], c * tk, 0
        for ref, n in zip(src_refs, lengths):
            lo, hi = max(first, begin), min(first + tk, begin + n)
            if lo < hi:
                pieces.append(ref[lo - begin:hi - begin, :])
            begin += n
        return pieces[0] if len(pieces) == 1 else jnp.concatenate(pieces, axis=0)

    def scores(c, slot):
        k = chunk_of(k_refs, c)
        for s in range(2):
            ks = k[:, s * LANES:(s + 1) * LANES] if slice_mode else k
            s_scr[s][slot][...] = lax.dot_general(qm_scr[s][...], ks, (((1,), (1,)), ((), ())),
                                                  preferred_element_type=F32)

    def softmax(slot):
        for s in range(2):
            for r in range(tq // rb):
                rows = slice(r * rb, (r + 1) * rb)
                sb = s_scr[s][slot][rows, :]
                m_old = m_scr[s][rows, :]
                m_new = jnp.maximum(m_old, jnp.max(sb, axis=-1, keepdims=True))
                m_scr[s][rows, :] = m_new
                a_scr[s][slot][rows, :] = jnp.exp2(m_old - m_new)
                p_scr[s][slot][rows, :] = jnp.exp2(sb - jnp.tile(m_new, (1, tk // LANES))).astype(BF16)

    def weighted_sum(c, slot):
        v = chunk_of(v_refs, c)
        vext = jnp.concatenate([v, jnp.ones((tk, LANES), BF16)], axis=1)
        for s in range(2):
            alpha = a_scr[s][slot][...]
            pv = jnp.dot(p_scr[s][slot][...], vext, preferred_element_type=F32)
            acc_scr[s][...] = jnp.concatenate([alpha, alpha], axis=1) * acc_scr[s][...] + pv

    def step(c, slot):
        scores(c, slot)
        softmax(1 - slot)
        weighted_sum(c - 2, slot)

    scores(0, 0)
    if n_chunks > 1:
        scores(1, 1)
    softmax(0)
    for c in range(2, n_chunks):
        step(c, c % 2)
    if n_chunks > 1:
        softmax((n_chunks - 1) % 2)
        weighted_sum(n_chunks - 2, n_chunks % 2)
    weighted_sum(n_chunks - 1, (n_chunks - 1) % 2)
    o0 = acc_scr[0][:, :LANES] / acc_scr[0][:, LANES:]
    o1 = acc_scr[1][:, :LANES] / acc_scr[1][:, LANES:]
    if diff_mode:
        lp = lam_ref[...]
        lam = (jnp.exp(jnp.sum(lp[0:1] * lp[1:2], axis=-1, keepdims=True))
               - jnp.exp(jnp.sum(lp[2:3] * lp[3:4], axis=-1, keepdims=True)) + lambda_init)
        o = o0 - lam * o1
        ms = jnp.mean(o * o, axis=-1, keepdims=True)
        o = o * lax.rsqrt(ms + NORM_EPS) * subln_ref[...] * (1.0 - lambda_init)
    else:
        o = jnp.where(lane < LANES // 2, o0, o1)
    o_ref[...] = o.astype(o_ref.dtype)


def _attention(q_arr, q_blk, srcs, *, batch, n_q, groups, slice_mode=False, diff=None, sub_width=LANES // 2,
               tq=512, tk=768, rb=16, name="attn"):
    qw = 2 * LANES if slice_mode else LANES
    n_k = sum(s[4] for s in srcs)
    tq = min(tq, n_q)
    tk = min(tk, n_k)
    assert n_q % tq == 0 and n_k % tk == 0 and tk % LANES == 0 and tq % rb == 0
    nqt = n_q // tq
    in_specs = [pl.BlockSpec((tq, qw), lambda b, g, i: (b * nqt + i, q_blk + g))]
    args = [q_arr]
    for k_arr, k_blk, v_arr, v_blk, nk in srcs:
        in_specs.append(pl.BlockSpec((nk, qw), lambda b, g, i, k_blk=k_blk: (b, k_blk + g)))
        in_specs.append(pl.BlockSpec((nk, LANES), lambda b, g, i, v_blk=v_blk: (b, v_blk + g)))
        args += [k_arr, v_arr]
    lambda_init = 0.0
    if diff is not None:
        lam_params, subln, lambda_init = diff
        in_specs.append(pl.BlockSpec((4, DA_HEAD_DIM), lambda b, g, i: (0, 0)))
        in_specs.append(pl.BlockSpec((1, LANES), lambda b, g, i: (0, 0)))
        args += [lam_params, subln.reshape(1, LANES)]
    kern = functools.partial(_attn_kernel, n_src=len(srcs), tk=tk, rb=rb, slice_mode=slice_mode,
                             sub_width=sub_width, diff_mode=diff is not None, lambda_init=lambda_init)
    scratch = 2 * ([pltpu.VMEM((tq, LANES), BF16)] + 2 * [pltpu.VMEM((tq, tk), F32)] + 2 * [pltpu.VMEM((tq, tk), BF16)]
                   + [pltpu.VMEM((tq, LANES), F32)] + 2 * [pltpu.VMEM((tq, LANES), F32)]
                   + [pltpu.VMEM((tq, 2 * LANES), F32)])
    return pl.pallas_call(
        kern, grid=(batch, groups, nqt), in_specs=in_specs,
        out_specs=pl.BlockSpec((tq, LANES), lambda b, g, i: (b * nqt + i, g)),
        out_shape=jax.ShapeDtypeStruct((batch * n_q, groups * LANES), BF16), scratch_shapes=scratch,
        compiler_params=_cparams(("parallel", "parallel", "arbitrary")), name=name)(*args)


def _na_kernel(q_ref, kc_ref, vc_ref, kl_ref, vl_ref, bias_ref, o_ref, s0_scr, s1_scr, p0_scr, p1_scr, *, n_rows, rb):
    i = pl.program_id(2)
    k0 = jnp.clip(i * NA_Q_ROWS - NA_WIN_ROWS // 2, 0, n_rows - NA_K_ROWS)
    start = pl.multiple_of(k0 * GRID_W, GRID_W * 4)
    nkw = NA_K_ROWS * GRID_W
    kw = kl_ref[pl.ds(start, nkw), :]
    kc = kc_ref[...]
    q = q_ref[...]
    tq = q.shape[0]
    n_all = nkw + kc.shape[0]
    lane = lax.broadcasted_iota(jnp.int32, (1, LANES), 1)
    s_scr, p_scr = (s0_scr, s1_scr), (p0_scr, p1_scr)
    dn = (((1,), (1,)), ((), ()))
    key_row = lax.broadcasted_iota(jnp.int32, (1, nkw), 1) // GRID_W
    for s in range(2):
        qs = jnp.where((lane < LANES // 2) if s == 0 else (lane >= LANES // 2), q, jnp.zeros_like(q))
        s_win = lax.dot_general(qs, kw, dn, preferred_element_type=F32)
        for a in range(NA_Q_ROWS):
            qr = i * NA_Q_ROWS + a
            r0 = jnp.clip(qr - NA_WIN_ROWS // 2, 0, n_rows - NA_WIN_ROWS)
            u = k0 - i * NA_Q_ROWS + (NA_WIN_ROWS - 1 - a + NA_BIAS_PAD)
            off = pl.multiple_of((u // 2) * LANES, LANES)
            parity = (NA_WIN_ROWS - 1 - a + NA_BIAS_PAD) % 2
            bias = bias_ref[s, parity, :, pl.ds(off, nkw)]
            seen = jnp.logical_and(key_row >= r0 - k0, key_row < r0 - k0 + NA_WIN_ROWS)
            rows = slice(a * GRID_W, (a + 1) * GRID_W)
            s_scr[s][rows, :nkw] = jnp.where(seen, s_win[rows, :] + bias, MASK_VALUE)
        s_scr[s][:, nkw:] = lax.dot_general(qs, kc, dn, preferred_element_type=F32)
    for s in range(2):
        for r in range(tq // rb):
            rows = slice(r * rb, (r + 1) * rb)
            sb = s_scr[s][rows, :]
            p_scr[s][rows, :] = jnp.exp2(sb - jnp.max(sb, axis=-1, keepdims=True)).astype(BF16)
    v_all = jnp.concatenate([vl_ref[pl.ds(start, nkw), :], vc_ref[...]], axis=0)
    vext = jnp.concatenate([v_all, jnp.ones((n_all, LANES), BF16)], axis=1)
    outs = []
    for s in range(2):
        pv = jnp.dot(p_scr[s][...], vext, preferred_element_type=F32)
        outs.append(pv[:, :LANES] / pv[:, LANES:])
    o_ref[...] = jnp.where(lane < LANES // 2, outs[0], outs[1]).astype(o_ref.dtype)


NA_BIAS_PAD = 8
NA_BIAS_LANES = 2048


def _na_bias_tables(rpb):
    h = rpb.shape[0]
    n_off = 2 * NA_WIN_ROWS - 1
    dc = np.arange(GRID_W)[None, :] - np.arange(GRID_W)[:, None] + NA_WIN_COLS - 1
    onehot = (dc[None] == np.arange(2 * NA_WIN_COLS - 1)[:, None, None]).astype(np.float32)
    toep = jnp.einsum('hrd,dqk->hqrk', rpb.astype(F32), jnp.asarray(onehot), precision=lax.Precision.HIGHEST)
    cols = np.arange(GRID_W)
    col_start = np.clip(cols - NA_WIN_COLS // 2, 0, GRID_W - NA_WIN_COLS)
    col_ok = (cols[None, :] >= col_start[:, None]) & (cols[None, :] < col_start[:, None] + NA_WIN_COLS)
    toep = jnp.where(jnp.asarray(col_ok)[None, :, None, :], toep, MASK_VALUE).reshape(h, GRID_W, n_off * GRID_W)
    copies = []
    for shift in (0, GRID_W):
        left = NA_BIAS_PAD * GRID_W - shift
        copies.append(jnp.pad(toep, ((0, 0), (0, 0), (left, NA_BIAS_LANES - left - n_off * GRID_W))))
    return jnp.stack(copies, axis=1)


def _na_attention(qkv_lat, qkv_ctx, bias_tab, *, batch, n_lat, n_ctx):
    d = NA_HEADS * NA_HEAD_DIM
    groups = d // LANES
    n_rows = n_lat // GRID_W
    nb = n_rows // NA_Q_ROWS
    tq = NA_Q_ROWS * GRID_W
    assert n_rows >= NA_K_ROWS and n_rows % NA_Q_ROWS == 0
    in_specs = [
        pl.BlockSpec((tq, LANES), lambda b, g, i: (b * nb + i, g)),
        pl.BlockSpec((n_ctx, LANES), lambda b, g, i: (b, groups + g)),
        pl.BlockSpec((n_ctx, LANES), lambda b, g, i: (b, 2 * groups + g)),
        pl.BlockSpec((n_lat, LANES), lambda b, g, i: (b, groups + g)),
        pl.BlockSpec((n_lat, LANES), lambda b, g, i: (b, 2 * groups + g)),
        pl.BlockSpec((2, 2, GRID_W, NA_BIAS_LANES), lambda b, g, i: (g, 0, 0, 0)),
    ]
    n_all = NA_K_ROWS * GRID_W + n_ctx
    return pl.pallas_call(
        functools.partial(_na_kernel, n_rows=n_rows, rb=16), grid=(batch, groups, nb), in_specs=in_specs,
        out_specs=pl.BlockSpec((tq, LANES), lambda b, g, i: (b * nb + i, g)),
        out_shape=jax.ShapeDtypeStruct((batch * n_lat, d), BF16),
        scratch_shapes=2 * [pltpu.VMEM((tq, n_all), F32)] + 2 * [pltpu.VMEM((tq, n_all), BF16)],
        compiler_params=_cparams(("parallel", "parallel", "arbitrary")), name="na_attn",
    )(qkv_lat, qkv_ctx, qkv_ctx, qkv_lat, qkv_lat, bias_tab)


def _dft_kernel(c_ref, ms_ref, ab_ref, mid_ref, o_ref, acc_ref, *, scale):
    n = pl.program_id(2)
    d = o_ref.shape[1]
    part = (jnp.dot(c_ref[...], ab_ref[:, :d], preferred_element_type=F32)
            + jnp.dot(ms_ref[...], ab_ref[:, d:], preferred_element_type=F32))

    @pl.when(n == 0)
    def _():
        acc_ref[...] = part

    @pl.when(n > 0)
    def _():
        acc_ref[...] += part

    @pl.when(n == pl.num_programs(2) - 1)
    def _():
        tk = o_ref.shape[0]
        k = pl.program_id(1) * tk + lax.broadcasted_iota(jnp.int32, (tk, 1), 0)
        sign = jnp.where(k % 2 == 0, 1.0, -1.0)
        o_ref[...] = ((acc_ref[...] + sign * mid_ref[...].astype(F32)) * scale).astype(o_ref.dtype)


def _fold_kernel(x_ref, src_ref, edge_ref, perm_ref, sign_ref, o_ref):
    i = pl.program_id(1)
    mirror = jnp.dot(perm_ref[...], src_ref[...], preferred_element_type=F32)
    first = lax.broadcasted_iota(jnp.int32, (mirror.shape[0], 1), 0) == 0
    edge = jnp.where(i > 0, edge_ref[0:1, :].astype(F32), 0.0)
    mirror = jnp.where(first, edge, mirror)
    o_ref[...] = (x_ref[...].astype(F32) + sign_ref[...] * mirror).astype(o_ref.dtype)


def _dft_fold(ab, *, batch, n_tok, d, name):
    half = n_tok // 2
    t = min(512, half)
    nt = n_tok // t
    perm = np.zeros((t, t), np.float32)
    perm[np.arange(1, t), t - np.arange(1, t)] = 1.0
    sign = jnp.concatenate([jnp.ones((d,), F32), -jnp.ones((d,), F32)]).reshape(1, 2 * d)
    edge_blocks = t // 8
    return pl.pallas_call(
        _fold_kernel, grid=(batch, nt // 2),
        in_specs=[pl.BlockSpec((t, 2 * d), lambda b, i: (b * nt + i, 0)),
                  pl.BlockSpec((t, 2 * d), lambda b, i: (b * nt + nt - 1 - i, 0)),
                  pl.BlockSpec((8, 2 * d), lambda b, i: ((b * nt + jnp.maximum(nt - i, 1) % nt) * edge_blocks, 0)),
                  pl.BlockSpec((t, t), lambda b, i: (0, 0)),
                  pl.BlockSpec((1, 2 * d), lambda b, i: (0, 0))],
        out_specs=pl.BlockSpec((t, 2 * d), lambda b, i: (b * (nt // 2) + i, 0)),
        out_shape=jax.ShapeDtypeStruct((batch * half, 2 * d), BF16),
        compiler_params=_cparams(("parallel", "parallel")), name=name)(ab, ab, ab, jnp.asarray(perm, BF16), sign)


def _pos_dft(cos_t, msin_t, ab, *, batch, n_tok, d, scale, name):
    half = n_tok // 2
    folded = _dft_fold(ab, batch=batch, n_tok=n_tok, d=d, name=name + "_fold")
    mid = ab.reshape(batch, n_tok, 2 * d)[:, half:half + 1, :d]
    tk = min(1024, n_tok)
    tn = min(1024, half)
    nk, nn = n_tok // tk, half // tn
    return pl.pallas_call(
        functools.partial(_dft_kernel, scale=scale), grid=(batch, nk, nn),
        in_specs=[pl.BlockSpec((tk, tn), lambda b, k, n: (k, n)),
                  pl.BlockSpec((tk, tn), lambda b, k, n: (k, n)),
                  pl.BlockSpec((tn, 2 * d), lambda b, k, n: (b * nn + n, 0)),
                  pl.BlockSpec((None, 1, d), lambda b, k, n: (b, 0, 0))],
        out_specs=pl.BlockSpec((tk, d), lambda b, k, n: (b * nk + k, 0)),
        out_shape=jax.ShapeDtypeStruct((batch * n_tok, d), BF16),
        scratch_shapes=[pltpu.VMEM((tk, d), F32)],
        compiler_params=_cparams(("parallel", "parallel", "arbitrary")), name=name)(cos_t, msin_t, folded, mid)


def _dft_tables(n):
    split = 64 if n % 64 == 0 and n > 64 else 1
    m = jnp.arange(n // 2, dtype=jnp.int32)[None, :]

    def phase(kvals):
        ang = ((kvals[:, None] * m) % n).astype(F32) * (2.0 * math.pi / n)
        return jnp.cos(ang), jnp.sin(ang)

    c1, s1 = phase(jnp.arange(n // split, dtype=jnp.int32) * split)
    c2, s2 = phase(jnp.arange(split, dtype=jnp.int32))
    c = c1[:, None, :] * c2[None, :, :] - s1[:, None, :] * s2[None, :, :]
    s = s1[:, None, :] * c2[None, :, :] + c1[:, None, :] * s2[None, :, :]
    return c.reshape(n, n // 2).astype(BF16), (-s).reshape(n, n // 2).astype(BF16)


def _channel_dft_weight(d):
    cg = d // FN_GROUPS
    idx = np.arange(cg)
    ang = 2.0 * np.pi * ((idx[:, None] * idx[None, :]) % cg) / cg
    eye = np.eye(FN_GROUPS)
    return jnp.asarray(np.concatenate([np.kron(eye, np.cos(ang)), np.kron(eye, np.sin(ang))], axis=1), F32)


def _router_kernel(x_ref, gain_ref, sc_ref, sh_ref, wr_ref, h_ref, aff_ref):
    x = x_ref[...]
    ms = jnp.mean(x * x, axis=-1, keepdims=True)
    h = x * lax.rsqrt(ms + NORM_EPS) * gain_ref[...]
    h = h * (1.0 + sc_ref[...]) + sh_ref[...]
    h_ref[...] = h.astype(h_ref.dtype)
    w = wr_ref[...]
    h_hi, w_hi = h.astype(BF16), w.astype(BF16)
    h_lo, w_lo = (h - h_hi.astype(F32)).astype(BF16), (w - w_hi.astype(F32)).astype(BF16)
    logits = (jnp.dot(h_hi, w_hi, preferred_element_type=F32) + jnp.dot(h_lo, w_hi, preferred_element_type=F32)
              + jnp.dot(h_hi, w_lo, preferred_element_type=F32))
    e = jnp.exp(logits - jnp.max(logits, axis=-1, keepdims=True))
    aff_ref[...] = e / jnp.sum(e, axis=-1, keepdims=True)


def _norm_router(x, gain, sc, sh, w_router, layer, *, rows_per_batch, tm=512):
    m, d = x.shape
    e = w_router.shape[-1]
    tm = min(tm, rows_per_batch)
    tpb = rows_per_batch // tm
    return pl.pallas_call(
        _router_kernel, grid=(m // tm,),
        in_specs=[pl.BlockSpec((tm, d), lambda i: (i, 0)),
                  pl.BlockSpec((1, d), lambda i: (0, 0)),
                  pl.BlockSpec((None, 1, d), lambda i: (i // tpb, 0, 0)),
                  pl.BlockSpec((None, 1, d), lambda i: (i // tpb, 0, 0)),
                  pl.BlockSpec((None, d, e), lambda i: (layer, 0, 0))],
        out_specs=[pl.BlockSpec((tm, d), lambda i: (i, 0)), pl.BlockSpec((tm, e), lambda i: (i, 0))],
        out_shape=[jax.ShapeDtypeStruct((m, d), BF16), jax.ShapeDtypeStruct((m, e), F32)],
        compiler_params=_cparams(("parallel",)), name="norm_router")(x, gain.reshape(1, d), sc, sh, w_router)


FFN_PIECE = 256


def _ffn_kernel(*refs, n_streams):
    x_refs = refs[:n_streams]
    wg_ref, wu_ref, wd_ref = refs[n_streams:n_streams + 3]
    gate_refs = refs[n_streams + 3:2 * n_streams + 3]
    o_refs = refs[2 * n_streams + 3:3 * n_streams + 3]
    acc_ref = refs[3 * n_streams + 3]
    f = pl.program_id(2)
    x = x_refs[0][...] if n_streams == 1 else jnp.concatenate([r[...] for r in x_refs], axis=0)
    @pl.when(f == 0)
    def _():
        acc_ref[...] = jnp.zeros(acc_ref.shape, F32)

    tf = wg_ref.shape[1]
    acts = []
    for c0 in range(0, tf, FFN_PIECE):
        a = jnp.dot(x, wg_ref[:, c0:c0 + FFN_PIECE].astype(BF16), preferred_element_type=F32)
        u = jnp.dot(x, wu_ref[:, c0:c0 + FFN_PIECE].astype(BF16), preferred_element_type=F32)
        acts.append(((a * (1.0 / (1.0 + jnp.exp(-a)))) * u).astype(BF16))
    d = wd_ref.shape[1]
    for n0 in range(0, d, FFN_PIECE):
        y = None
        for i, act in enumerate(acts):
            part = jnp.dot(act, wd_ref[i * FFN_PIECE:(i + 1) * FFN_PIECE, n0:n0 + FFN_PIECE].astype(BF16),
                           preferred_element_type=F32)
            y = part if y is None else y + part
        acc_ref[:, n0:n0 + FFN_PIECE] += y

    @pl.when(f == pl.num_programs(2) - 1)
    def _():
        row = 0
        for o_ref, g_ref in zip(o_refs, gate_refs):
            c = o_ref.shape[0]
            y = acc_ref[row:row + c, :] * g_ref[...]
            if o_ref.dtype == BF16:
                hi = y.astype(BF16)
                o_ref[:, :d] = hi
                o_ref[:, d:] = (y - hi.astype(F32)).astype(BF16)
            else:
                o_ref[...] = y
            row += c


def _moe_ffn(xs, gates, w_gate, w_up, w_down, layer, *, tf=512):
    b, e, _, d = xs[0].shape
    f = w_gate.shape[-1]
    tf = min(tf, f)
    slot = lambda c, w: pl.BlockSpec((None, None, c, w), lambda ei, bi, fi: (bi, ei, 0, 0))
    caps = [t.shape[2] for t in xs]
    in_specs = ([slot(c, d) for c in caps]
                + [pl.BlockSpec((None, None, d, tf), lambda ei, bi, fi: (layer, ei, 0, fi)),
                   pl.BlockSpec((None, None, d, tf), lambda ei, bi, fi: (layer, ei, 0, fi)),
                   pl.BlockSpec((None, None, tf, d), lambda ei, bi, fi: (layer, ei, fi, 0))]
                + [slot(c, 1) for c in caps])
    return pl.pallas_call(
        functools.partial(_ffn_kernel, n_streams=len(xs)), grid=(e, b, f // tf), in_specs=in_specs,
        out_specs=[slot(caps[0], 2 * d)] + [slot(c, d) for c in caps[1:]],
        out_shape=([jax.ShapeDtypeStruct((b, e, caps[0], 2 * d), BF16)]
                   + [jax.ShapeDtypeStruct((b, e, c, d), F32) for c in caps[1:]]),
        scratch_shapes=[pltpu.VMEM((sum(caps), d), F32)],
        compiler_params=_cparams(("parallel", "parallel", "arbitrary")), name="moe_ffn",
    )(*xs, w_gate, w_up, w_down, *gates)


def _count_upto(flags):
    lead, n = flags.shape[:-1], flags.shape[-1]
    blocks = flags.reshape(*lead, n // LANES, LANES).astype(BF16)
    tri = jnp.asarray(np.triu(np.ones((LANES, LANES), np.float32))).astype(BF16)
    inside = jnp.einsum('...i,ij->...j', blocks, tri, preferred_element_type=F32)
    totals = inside[..., -1]
    before = jnp.cumsum(totals, axis=-1) - totals
    return (inside + before[..., None]).reshape(*lead, n).astype(jnp.int32)


def _route(aff, h, batch, n_tok):
    e = aff.shape[-1]
    cap = EC_CAPACITY_FACTOR * n_tok // e
    aff_t = jnp.swapaxes(aff.reshape(batch, n_tok, e), 1, 2)
    gate, idx = lax.top_k(aff_t, cap)
    thr = gate[..., -1:]
    above, tied = aff_t > thr, aff_t == thr
    n_tied = cap - jnp.sum(above, axis=-1, keepdims=True, dtype=jnp.int32)
    chosen = above | (tied & (_count_upto(tied) <= n_tied))
    pos = jnp.where(chosen, _count_upto(chosen) - 1, -1)
    pos = jnp.swapaxes(pos, 1, 2).reshape(batch * n_tok, e)
    idx, gate = lax.sort((idx, gate), dimension=2, num_keys=1)
    xs = jax.vmap(lambda hb, ib: hb[ib])(h.reshape(batch, n_tok, -1), idx)
    return gate, idx, xs, pos


COMBINE_TOKENS = 256
COMBINE_WINDOW = 64
SLOT_ALIGN = 16


def _combine_kernel(lo_ref, nr_ref, x_ref, g_ref, pos_ref, y_hbm, o_ref, stage, sem, acc, *, n_tiles, n_total, n_exp,
                    cap):
    w = COMBINE_WINDOW
    per = LANES // w
    tile = pl.program_id(0) * n_tiles + pl.program_id(1)
    slot = tile % 2
    lane = lax.broadcasted_iota(jnp.int32, (1, LANES), 1)

    def window(tl, e, r):
        first = lo_ref[tl * n_exp + e] + r * w
        return first, pl.multiple_of(jnp.minimum(first, cap - w), SLOT_ALIGN)

    def copy(tl, e, start, sl):
        return pltpu.make_async_copy(y_hbm.at[tl // n_tiles, e, pl.ds(start, w), :],
                                     stage.at[sl, pl.ds(e * w, w), :], sem.at[sl, e])

    def fetch(tl, r, sl):
        for e in range(n_exp):
            copy(tl, e, window(tl, e, r)[1], sl).start()

    def add_round(r):
        pos = pos_ref[...]
        blocks = []
        for e0 in range(0, n_exp, per):
            first, start = window(tile, e0, r)
            col = pos[:, e0:e0 + 1]
            for k in range(1, per):
                f_k, s_k = window(tile, e0 + k, r)
                here = lane >= k * w
                first, start = jnp.where(here, f_k, first), jnp.where(here, s_k, start)
                col = jnp.where(here, pos[:, e0 + k:e0 + k + 1], col)
            hit = jnp.logical_and(col - start == lane % w, col >= first)
            blocks.append(jnp.where(hit, 1.0, 0.0).astype(BF16))
        onehot = jnp.concatenate(blocks, axis=1)
        for e in range(n_exp):
            copy(tile, e, 0, slot).wait()
        d = acc.shape[1]
        acc[...] += (jnp.dot(onehot, stage[slot, :, :d], preferred_element_type=F32)
                     + jnp.dot(onehot, stage[slot, :, d:], preferred_element_type=F32))

    @pl.when(tile == 0)
    def _():
        fetch(tile, 0, slot)

    @pl.when(tile + 1 < n_total)
    def _():
        fetch(tile + 1, 0, 1 - slot)

    acc[...] = jnp.zeros(acc.shape, F32)
    add_round(0)

    @pl.loop(1, nr_ref[tile])
    def _(r):
        fetch(tile, r, slot)
        add_round(r)

    o_ref[...] = x_ref[...] + g_ref[...] * acc[...]


def _combine_lat(x, g2, y, idx, pos, batch, n_tok):
    d = x.shape[-1]
    _, n_exp, cap = idx.shape
    tt, w = COMBINE_TOKENS, COMBINE_WINDOW
    assert n_tok % tt == 0 and cap % SLOT_ALIGN == 0 and w % SLOT_ALIGN == 0 and cap >= w
    nt = n_tok // tt
    bounds = jnp.arange(nt + 1, dtype=jnp.int32) * tt
    below = jnp.sum((idx[..., None] < bounds).astype(jnp.int32), axis=2)
    lo = (below[..., :-1] // SLOT_ALIGN) * SLOT_ALIGN
    rounds = jnp.maximum(jnp.max((below[..., 1:] - lo + w - 1) // w, axis=1), 1)
    lo_flat = jnp.transpose(lo, (0, 2, 1)).reshape(-1)
    grid_spec = pltpu.PrefetchScalarGridSpec(
        num_scalar_prefetch=2, grid=(batch, nt),
        in_specs=[pl.BlockSpec((tt, d), lambda b, t, lo_r, nr_r: (b * nt + t, 0)),
                  pl.BlockSpec((None, 1, d), lambda b, t, lo_r, nr_r: (b, 0, 0)),
                  pl.BlockSpec((tt, n_exp), lambda b, t, lo_r, nr_r: (b * nt + t, 0)),
                  pl.BlockSpec(memory_space=pl.ANY)],
        out_specs=pl.BlockSpec((tt, d), lambda b, t, lo_r, nr_r: (b * nt + t, 0)),
        scratch_shapes=[pltpu.VMEM((2, n_exp * w, 2 * d), BF16), pltpu.SemaphoreType.DMA((2, n_exp)),
                        pltpu.VMEM((tt, d), F32)])
    return pl.pallas_call(
        functools.partial(_combine_kernel, n_tiles=nt, n_total=batch * nt, n_exp=n_exp, cap=cap),
        grid_spec=grid_spec, out_shape=jax.ShapeDtypeStruct(x.shape, F32),
        compiler_params=_cparams(("arbitrary", "arbitrary")), name="moe_combine",
    )(lo_flat, rounds.reshape(-1), x, g2, pos, y)


def _combine(x, g2, y, idx, batch, n_tok):
    d = x.shape[-1]
    upd = jax.vmap(lambda ib, yb: jnp.zeros((n_tok, d), F32).at[ib.reshape(-1)].add(yb.reshape(-1, d)))(idx, y)
    return (x.reshape(batch, n_tok, d) + g2 * upd).reshape(batch * n_tok, d)


def _final_norm_kernel(x_ref, g_ref, o_ref):
    x = x_ref[...]
    ms = jnp.mean(x * x, axis=-1, keepdims=True)
    o_ref[...] = x * lax.rsqrt(ms + NORM_EPS) * g_ref[...]


def _final_norm(x, g, tm=1024):
    m, d = x.shape
    tm = min(tm, m)
    return pl.pallas_call(
        _final_norm_kernel, grid=(m // tm,),
        in_specs=[pl.BlockSpec((tm, d), lambda i: (i, 0)), pl.BlockSpec((1, d), lambda i: (0, 0))],
        out_specs=pl.BlockSpec((tm, d), lambda i: (i, 0)), out_shape=jax.ShapeDtypeStruct((m, d), F32),
        compiler_params=_cparams(("parallel",)), name="final_norm")(x, g.reshape(1, d))


def _rope_tables(n_tok, rot_dim, pair_of_lane):
    t = jnp.arange(n_tok)
    rows = (t // GRID_W).astype(F32)
    cols = (t % GRID_W).astype(F32)
    n_freq = rot_dim // 4
    inv_freq = ROPE_BASE ** (-jnp.arange(n_freq, dtype=F32) / n_freq)
    ang = jnp.concatenate([rows[:, None] * inv_freq, cols[:, None] * inv_freq], axis=-1)
    cos, sin = jnp.cos(ang), jnp.sin(ang)
    pair = np.concatenate([pair_of_lane, pair_of_lane])
    used = jnp.asarray(pair >= 0)[None]
    sign = jnp.asarray(np.where(np.arange(LANES) < LANES // 2, -1.0, 1.0).astype(np.float32))[None]
    cos_l = jnp.where(used, cos[:, np.maximum(pair, 0)], 1.0)
    sin_l = jnp.where(used, sin[:, np.maximum(pair, 0)] * sign, 0.0)
    return cos_l, sin_l


def kernel(x, c, ctx, c_ctx, ada_w, ada_b, norm_mix, norm_ffn, norm_final, da_w_qkv, da_w_o, da_lambda_q1,
           da_lambda_k1, da_lambda_q2, da_lambda_k2, da_subln, fn_w_o, na_w_qkv, na_w_o, na_rpb, mla_w_dq,
           mla_q_norm, mla_w_uq, mla_w_dkv, mla_kv_norm, mla_w_uk, mla_w_uv, mla_w_o, moe_w_router, moe_w_gate,
           moe_w_up, moe_w_down):
    batch, n_lat, d = x.shape
    n_ctx = ctx.shape[1]
    depth = ada_w.shape[0]
    n_mixers = 4
    xl = x.reshape(batch * n_lat, d)
    xc = ctx.reshape(batch * n_ctx, d)

    cond = jnp.concatenate([c, c_ctx[None], jnp.zeros((8 - batch - 1, d), F32)], axis=0)
    mods = [
        _fused_mm(cond, ada_w, w_index=i, bias=ada_b[i], silu_in=True, precision=lax.Precision.HIGHEST, tn=1024,
                  name="ada_mod")
        for i in range(depth)
    ]

    def chunks(i):
        lat = [mods[i][:batch, k * d:(k + 1) * d].reshape(batch, 1, d) for k in range(ADA_CHUNKS)]
        cx = [mods[i][batch:batch + 1, k * d:(k + 1) * d].reshape(1, 1, d) for k in range(ADA_CHUNKS)]
        return lat, cx

    for i in range(depth):
        kind, j = i % n_mixers, i // n_mixers
        keep_ctx = i < depth - 1
        (sh1, sc1, g1, sh2, sc2, g2), (csh1, csc1, cg1, csh2, csc2, cg2) = chunks(i)
        lat_in = dict(gain=norm_mix[i], mod=(sc1, sh1), rows_per_batch=n_lat)
        ctx_in = dict(gain=norm_mix[i], mod=(csc1, csh1), rows_per_batch=batch * n_ctx, tm=n_ctx)

        if kind == 0:
            lambda_init = 0.8 - 0.6 * math.exp(-0.3 * i)
            scale = jnp.concatenate([jnp.full((d,), LOG2E * DA_HEAD_DIM ** -0.5, F32), jnp.ones((2 * d,), F32)])
            groups = d // LANES
            half = DA_HEAD_DIM // 2
            w = da_w_qkv[j]

            def reorder(cols):
                return cols.reshape(d, groups, 2, 2, half).transpose(0, 1, 3, 2, 4).reshape(d, d)

            w = jnp.concatenate([reorder(w[:, :d]), reorder(w[:, d:2 * d]), w[:, 2 * d:]], axis=1)
            rope = _rope_tables(n_lat, DA_HEAD_DIM, np.arange(LANES // 2) % half)
            qkv_l = _fused_mm(xl, w, colscale=scale, rope=rope, rope_cols=2 * d, out_dtype=BF16, name="da_qkv_lat",
                              **lat_in)
            qkv_c = _fused_mm(xc, w, colscale=scale, out_dtype=BF16, name="da_qkv_ctx", **ctx_in)
            lam_params = jnp.stack([da_lambda_q1[j], da_lambda_k1[j], da_lambda_q2[j], da_lambda_k2[j]])
            diff = (lam_params, da_subln[j], lambda_init)
            ctx_kv = (qkv_c, groups, qkv_c, 2 * groups, n_ctx)
            lat_kv = (qkv_l, groups, qkv_l, 2 * groups, n_lat)
            o_l = _attention(qkv_l, 0, [ctx_kv, lat_kv], batch=batch, n_q=n_lat, groups=groups, sub_width=half,
                             diff=diff, name="da_attn_lat")
            o_c = _attention(qkv_c, 0, [ctx_kv], batch=batch, n_q=n_ctx, groups=groups, sub_width=half, diff=diff,
                             name="da_attn_ctx") if keep_ctx else None
            w_o = da_w_o
        elif kind == 1:
            w_cd = _channel_dft_weight(d)
            ab_l = _fused_mm(xl, w_cd, out_dtype=BF16, name="fn_chan_lat", **lat_in)
            scale_l = 1.0 / math.sqrt(n_lat * (d // FN_GROUPS))
            o_l = _pos_dft(*_dft_tables(n_lat), ab_l, batch=batch, n_tok=n_lat, d=d, scale=scale_l, name="fn_pos_lat")
            o_c = None
            if keep_ctx:
                ab_c = _fused_mm(xc, w_cd, out_dtype=BF16, name="fn_chan_ctx", **ctx_in)
                scale_c = 1.0 / math.sqrt(n_ctx * (d // FN_GROUPS))
                o_c = _pos_dft(*_dft_tables(n_ctx), ab_c, batch=batch, n_tok=n_ctx, d=d, scale=scale_c,
                               name="fn_pos_ctx")
            w_o = fn_w_o
        elif kind == 2:
            scale = jnp.concatenate([jnp.full((d,), LOG2E * NA_HEAD_DIM ** -0.5, F32), jnp.ones((2 * d,), F32)])
            qkv_l = _fused_mm(xl, na_w_qkv, w_index=j, colscale=scale, out_dtype=BF16, name="na_qkv_lat", **lat_in)
            qkv_c = _fused_mm(xc, na_w_qkv, w_index=j, colscale=scale, out_dtype=BF16, name="na_qkv_ctx", **ctx_in)
            groups = d // LANES
            bias_tab = _na_bias_tables(na_rpb[j] * LOG2E)
            o_l = _na_attention(qkv_l, qkv_c, bias_tab, batch=batch, n_lat=n_lat, n_ctx=n_ctx)
            o_c = _attention(qkv_c, 0, [(qkv_c, groups, qkv_c, 2 * groups, n_ctx)], batch=batch, n_q=n_ctx,
                             groups=groups, name="na_attn_ctx") if keep_ctx else None
            w_o = na_w_o
        else:
            q_rank = mla_w_dq.shape[-1]
            kv_rank = mla_w_uk.shape[-2]
            hq = MLA_NOPE_DIM + MLA_ROPE_DIM
            down_w = 3 * 384
            assert q_rank == 768 and kv_rank == 256
            w_down = jnp.concatenate([mla_w_dq[j], mla_w_dkv[j], jnp.zeros((d, down_w - q_rank - kv_rank - MLA_ROPE_DIM), F32)],
                                     axis=1)
            rh = MLA_ROPE_DIM // 2
            n_a = LANES // 2 - rh

            def head_lanes(nope, rope_part):
                z = jnp.zeros(nope.shape[:-1] + (LANES - hq,), F32)
                return jnp.concatenate([rope_part[..., :rh], nope[..., :n_a], rope_part[..., rh:], nope[..., n_a:], z],
                                       axis=-1)

            wq = mla_w_uq[j].reshape(q_rank, MLA_HEADS, hq)
            wq = head_lanes(wq[..., :MLA_NOPE_DIM], wq[..., MLA_NOPE_DIM:]).reshape(q_rank, MLA_HEADS * LANES)
            wk_nope = head_lanes(mla_w_uk[j].reshape(kv_rank, MLA_HEADS, MLA_NOPE_DIM),
                                 jnp.zeros((kv_rank, MLA_HEADS, MLA_ROPE_DIM), F32))
            eye = jnp.broadcast_to(jnp.eye(MLA_ROPE_DIM, dtype=F32)[:, None, :], (MLA_ROPE_DIM, MLA_HEADS, MLA_ROPE_DIM))
            wk_rope = head_lanes(jnp.zeros((MLA_ROPE_DIM, MLA_HEADS, MLA_NOPE_DIM), F32), eye)
            kw = MLA_HEADS * LANES
            vw = MLA_HEADS * MLA_V_DIM
            pad_rows = 384 - kv_rank - MLA_ROPE_DIM
            w_kv = jnp.concatenate([
                jnp.concatenate([wk_nope.reshape(kv_rank, kw), mla_w_uv[j]], axis=1),
                jnp.concatenate([wk_rope.reshape(MLA_ROPE_DIM, kw), jnp.zeros((MLA_ROPE_DIM, vw), F32)], axis=1),
                jnp.zeros((pad_rows, kw + vw), F32)], axis=0)
            kv_gain = jnp.concatenate([mla_kv_norm[j], jnp.ones((384 - kv_rank,), F32)])
            rope = _rope_tables(n_lat, MLA_ROPE_DIM, np.where(np.arange(LANES // 2) < rh, np.arange(LANES // 2), -1))
            qscale = jnp.full((kw,), LOG2E * hq ** -0.5, F32)

            def project(xs, inp, n_tok, rope_tabs, tag):
                low = _fused_mm(xs, w_down, tn=384, name="mla_down_" + tag, **inp)
                rp = dict(rope=rope_tabs) if rope_tabs is not None else {}
                tmx = inp.get("tm", 1024)
                q = _fused_mm(low, wq, a_cols=(0, q_rank), gain=mla_q_norm[j], colscale=qscale, rope_cols=kw,
                              rows_per_batch=inp["rows_per_batch"], tm=tmx, out_dtype=BF16, name="mla_q_" + tag, **rp)
                kv = _fused_mm(low, w_kv, a_cols=(q_rank, 384), gain=kv_gain, norm_cols=kv_rank, rope_cols=kw,
                               rows_per_batch=inp["rows_per_batch"], tm=tmx, out_dtype=BF16, name="mla_kv_" + tag, **rp)
                return q, kv

            q_l, kv_l = project(xl, lat_in, n_lat, rope, "lat")
            q_c, kv_c = project(xc, ctx_in, n_ctx, None, "ctx")
            groups = MLA_HEADS // 2
            ctx_kv = (kv_c, 0, kv_c, 2 * groups, n_ctx)
            lat_kv = (kv_l, 0, kv_l, 2 * groups, n_lat)
            o_l = _attention(q_l, 0, [ctx_kv, lat_kv], batch=batch, n_q=n_lat, groups=groups, slice_mode=True,
                             name="mla_attn_lat")
            o_c = _attention(q_c, 0, [ctx_kv], batch=batch, n_q=n_ctx, groups=groups, slice_mode=True,
                             name="mla_attn_ctx") if keep_ctx else None
            w_o = mla_w_o

        xl = _fused_mm(o_l, w_o, w_index=j, res=xl, gate=g1, rows_per_batch=n_lat, name="mix_out_lat")
        if keep_ctx:
            xc = _fused_mm(o_c, w_o, w_index=j, res=xc, gate=jnp.broadcast_to(cg1, (1, 1, d)),
                           rows_per_batch=batch * n_ctx, tm=n_ctx, name="mix_out_ctx")

        h_l, aff_l = _norm_router(xl, norm_ffn[i], sc2, sh2, moe_w_router, i, rows_per_batch=n_lat)
        gate_l, idx_l, xs_l, pos_l = _route(aff_l, h_l, batch, n_lat)
        xs, gates = [xs_l], [gate_l[..., None]]
        if keep_ctx:
            h_c, aff_c = _norm_router(xc, norm_ffn[i], csc2, csh2, moe_w_router, i, rows_per_batch=batch * n_ctx,
                                      tm=n_ctx)
            gate_c, idx_c, xs_c, _ = _route(aff_c, h_c, batch, n_ctx)
            xs.append(xs_c)
            gates.append(gate_c[..., None])
        ys = _moe_ffn(xs, gates, moe_w_gate, moe_w_up, moe_w_down, i)
        xl = _combine_lat(xl, g2, ys[0], idx_l, pos_l, batch, n_lat)
        if keep_ctx:
            xc = _combine(xc, cg2, ys[1], idx_c, batch, n_ctx)

    return _final_norm(xl, norm_final).reshape(batch, n_lat, d)
```

```python
import functools
import math

import numpy as np
import jax
import jax.numpy as jnp
from jax import lax
from jax.experimental import pallas as pl
from jax.experimental.pallas import tpu as pltpu

F32 = jnp.float32
BF16 = jnp.bfloat16

LANES = 128
VMEM_LIMIT_BYTES = 56 * 1024 * 1024

GRID_W = 64
ROPE_BASE = 10000.0
NORM_EPS = 1e-6
ADA_CHUNKS = 6
DA_HEAD_DIM = 64
NA_HEADS = 16
NA_HEAD_DIM = 64
NA_WIN_ROWS = 8
NA_WIN_COLS = 16
NA_Q_ROWS = 8
NA_K_ROWS = 16
MLA_HEADS = 16
MLA_NOPE_DIM = 64
MLA_ROPE_DIM = 32
MLA_V_DIM = 64
FN_GROUPS = 4
EC_CAPACITY_FACTOR = 2
MASK_VALUE = -1e30
LOG2E = math.log2(math.e)


def _cparams(sem):
    return pltpu.CompilerParams(dimension_semantics=sem, vmem_limit_bytes=VMEM_LIMIT_BYTES)


def _rope_lanes(acc, cos, sin):
    outs = []
    for g in range(acc.shape[1] // LANES):
        blk = acc[:, g * LANES:(g + 1) * LANES]
        outs.append(blk * cos + pltpu.roll(blk, LANES // 2, 1) * sin)
    return outs[0] if len(outs) == 1 else jnp.concatenate(outs, axis=1)


def _mm_kernel(*refs, norm_cols, has_gain, has_mod, has_colscale, rope_tiles, has_bias, has_res,
               silu_in, precision):
    it = iter(refs)
    a_ref = next(it)
    w_ref = next(it)
    gain_ref = next(it) if has_gain else None
    sc_ref, sh_ref = (next(it), next(it)) if has_mod else (None, None)
    cs_ref = next(it) if has_colscale else None
    cos_ref, sin_ref = (next(it), next(it)) if rope_tiles else (None, None)
    bias_ref = next(it) if has_bias else None
    res_ref, gate_ref = (next(it), next(it)) if has_res else (None, None)
    o_ref = next(it)
    h_scr = next(it, None)
    j = pl.program_id(1)
    hi = precision is not None

    def prologue():
        a = a_ref[...].astype(F32)
        if silu_in:
            a = a * (1.0 / (1.0 + jnp.exp(-a)))
        if has_gain:
            k = a.shape[1]
            if norm_cols < k:
                col = lax.broadcasted_iota(jnp.int32, (1, k), 1)
                sq = jnp.where(col < norm_cols, a * a, 0.0)
            else:
                sq = a * a
            ms = jnp.sum(sq, axis=-1, keepdims=True) * (1.0 / norm_cols)
            y = a * lax.rsqrt(ms + NORM_EPS) * gain_ref[...]
            a = jnp.where(col < norm_cols, y, a) if norm_cols < k else y
        if has_mod:
            a = a * (1.0 + sc_ref[...]) + sh_ref[...]
        return a if hi else a.astype(BF16)

    if h_scr is not None:
        @pl.when(j == 0)
        def _():
            h_scr[...] = prologue()
        h = h_scr[...]
    else:
        h = a_ref[...] if hi else a_ref[...].astype(BF16)
    w = w_ref[...] if hi else w_ref[...].astype(BF16)
    acc = jnp.dot(h, w, preferred_element_type=F32, precision=precision)
    if has_colscale:
        acc = acc * cs_ref[...]

    def finish(v):
        if has_bias:
            v = v + bias_ref[...]
        if has_res:
            v = res_ref[...] + gate_ref[...] * v
        o_ref[...] = v.astype(o_ref.dtype)

    if rope_tiles:
        @pl.when(j < rope_tiles)
        def _():
            finish(_rope_lanes(acc, cos_ref[...], sin_ref[...]))

        @pl.when(j >= rope_tiles)
        def _():
            finish(acc)
    else:
        finish(acc)


def _fused_mm(a, w, *, w_index=None, a_cols=None, gain=None, norm_cols=None, mod=None, colscale=None, rope=None,
              rope_cols=0, bias=None, res=None, gate=None, silu_in=False, rows_per_batch=None,
              out_dtype=F32, tm=1024, tn=1024, precision=None, name="mm"):
    m = a.shape[0]
    k0, k = a_cols if a_cols is not None else (0, a.shape[1])
    if precision is None and w.dtype != BF16:
        w = (w[w_index] if w.ndim == 3 else w).astype(BF16)
    n = w.shape[-1]
    assert w.shape[-2] == k and k0 % k == 0
    tm = min(tm, m)
    tn = min(tn, n)
    assert m % tm == 0 and n % tn == 0
    rows_per_batch = rows_per_batch or m
    assert rows_per_batch % tm == 0
    tpb = rows_per_batch // tm
    has_prologue = gain is not None or mod is not None or silu_in
    in_specs = [pl.BlockSpec((tm, k), lambda i, j: (i, k0 // k))]
    args = [a]
    if w.ndim == 3:
        in_specs.append(pl.BlockSpec((None, k, tn), lambda i, j: (w_index, 0, j)))
    else:
        in_specs.append(pl.BlockSpec((k, tn), lambda i, j: (0, j)))
    args.append(w)
    if gain is not None:
        in_specs.append(pl.BlockSpec((1, k), lambda i, j: (0, 0)))
        args.append(gain.reshape(1, k).astype(F32))
    if mod is not None:
        for t in mod:
            in_specs.append(pl.BlockSpec((None, 1, k), lambda i, j: (i // tpb, 0, 0)))
            args.append(t)
    if colscale is not None:
        in_specs.append(pl.BlockSpec((1, tn), lambda i, j: (0, j)))
        args.append(colscale.reshape(1, n).astype(F32))
    rope_tiles = 0
    if rope is not None:
        assert rope_cols % tn == 0
        rope_tiles = rope_cols // tn
        for t in rope:
            in_specs.append(pl.BlockSpec((tm, LANES), lambda i, j: (i % tpb, 0)))
            args.append(t)
    if bias is not None:
        in_specs.append(pl.BlockSpec((1, tn), lambda i, j: (0, j)))
        args.append(bias.reshape(1, n).astype(F32))
    if res is not None:
        in_specs.append(pl.BlockSpec((tm, tn), lambda i, j: (i, j)))
        args.append(res)
        in_specs.append(pl.BlockSpec((None, 1, tn), lambda i, j: (i // tpb, 0, j)))
        args.append(gate)
    scratch = [pltpu.VMEM((tm, k), F32 if precision is not None else BF16)] if has_prologue else []
    kern = functools.partial(
        _mm_kernel, norm_cols=norm_cols or k, has_gain=gain is not None, has_mod=mod is not None,
        has_colscale=colscale is not None, rope_tiles=rope_tiles, has_bias=bias is not None,
        has_res=res is not None, silu_in=silu_in, precision=precision)
    return pl.pallas_call(
        kern, grid=(m // tm, n // tn), in_specs=in_specs,
        out_specs=pl.BlockSpec((tm, tn), lambda i, j: (i, j)),
        out_shape=jax.ShapeDtypeStruct((m, n), out_dtype), scratch_shapes=scratch,
        compiler_params=_cparams(("parallel", "arbitrary")), name=name)(*args)


def _attn_kernel(*refs, n_src, tk, rb, slice_mode, sub_width, diff_mode, lambda_init):
    q_ref = refs[0]
    k_refs, v_refs = refs[1:1 + 2 * n_src:2], refs[2:2 + 2 * n_src:2]
    pos = 1 + 2 * n_src
    if diff_mode:
        lam_ref, subln_ref = refs[pos], refs[pos + 1]
        pos += 2
    o_ref = refs[pos]
    per = [refs[pos + 1 + 9 * s:pos + 10 + 9 * s] for s in range(2)]
    qm_scr = [t[0] for t in per]
    s_scr = [t[1:3] for t in per]
    p_scr = [t[3:5] for t in per]
    m_scr = [t[5] for t in per]
    a_scr = [t[6:8] for t in per]
    acc_scr = [t[8] for t in per]
    tq = q_ref.shape[0]
    q = q_ref[...]
    lane = lax.broadcasted_iota(jnp.int32, (1, LANES), 1)
    for s in range(2):
        if slice_mode:
            qm_scr[s][...] = q[:, s * LANES:(s + 1) * LANES]
        else:
            qm_scr[s][...] = jnp.where((lane // sub_width) % 2 == s, q, jnp.zeros_like(q))
        m_scr[s][...] = jnp.full(m_scr[s].shape, MASK_VALUE, F32)
        acc_scr[s][...] = jnp.zeros(acc_scr[s].shape, F32)

    lengths = [r.shape[0] for r in k_refs]
    n_chunks = sum(lengths) // tk

    def chunk_of(src_refs, c):
        pieces, first, begin = [], c * tk, 0
        for ref, n in zip(src_refs, lengths):
            lo, hi = max(first, begin), min(first + tk, begin + n)
            if lo < hi:
                pieces.append(ref[lo - begin:hi - begin, :])
            begin += n
        return pieces[0] if len(pieces) == 1 else jnp.concatenate(pieces, axis=0)

    def scores(c, slot):
        k = chunk_of(k_refs, c)
        for s in range(2):
            ks = k[:, s * LANES:(s + 1) * LANES] if slice_mode else k
            s_scr[s][slot][...] = lax.dot_general(qm_scr[s][...], ks, (((1,), (1,)), ((), ())),
                                                  preferred_element_type=F32)

    def softmax(slot):
        for s in range(2):
            for r in range(tq // rb):
                rows = slice(r * rb, (r + 1) * rb)
                sb = s_scr[s][slot][rows, :]
                m_old = m_scr[s][rows, :]
                m_new = jnp.maximum(m_old, jnp.max(sb, axis=-1, keepdims=True))
                m_scr[s][rows, :] = m_new
                a_scr[s][slot][rows, :] = jnp.exp2(m_old - m_new)
                p_scr[s][slot][rows, :] = jnp.exp2(sb - jnp.tile(m_new, (1, tk // LANES))).astype(BF16)

    def weighted_sum(c, slot):
        v = chunk_of(v_refs, c)
        vext = jnp.concatenate([v, jnp.ones((tk, LANES), BF16)], axis=1)
        for s in range(2):
            alpha = a_scr[s][slot][...]
            pv = jnp.dot(p_scr[s][slot][...], vext, preferred_element_type=F32)
            acc_scr[s][...] = jnp.concatenate([alpha, alpha], axis=1) * acc_scr[s][...] + pv

    def step(c, slot):
        scores(c, slot)
        softmax(1 - slot)
        weighted_sum(c - 2, slot)

    scores(0, 0)
    if n_chunks > 1:
        scores(1, 1)
    softmax(0)
    for c in range(2, n_chunks):
        step(c, c % 2)
    if n_chunks > 1:
        softmax((n_chunks - 1) % 2)
        weighted_sum(n_chunks - 2, n_chunks % 2)
    weighted_sum(n_chunks - 1, (n_chunks - 1) % 2)
    o0 = acc_scr[0][:, :LANES] / acc_scr[0][:, LANES:]
    o1 = acc_scr[1][:, :LANES] / acc_scr[1][:, LANES:]
    if diff_mode:
        lp = lam_ref[...]
        lam = (jnp.exp(jnp.sum(lp[0:1] * lp[1:2], axis=-1, keepdims=True))
               - jnp.exp(jnp.sum(lp[2:3] * lp[3:4], axis=-1, keepdims=True)) + lambda_init)
        o = o0 - lam * o1
        ms = jnp.mean(o * o, axis=-1, keepdims=True)
        o = o * lax.rsqrt(ms + NORM_EPS) * subln_ref[...] * (1.0 - lambda_init)
    else:
        o = jnp.where(lane < LANES // 2, o0, o1)
    o_ref[...] = o.astype(o_ref.dtype)


def _attention(q_arr, q_blk, srcs, *, batch, n_q, groups, slice_mode=False, diff=None, sub_width=LANES // 2,
               tq=1024, tk=768, rb=16, name="attn"):
    qw = 2 * LANES if slice_mode else LANES
    n_k = sum(s[4] for s in srcs)
    tq = min(tq, n_q)
    tk = min(tk, n_k)
    assert n_q % tq == 0 and n_k % tk == 0 and tk % LANES == 0 and tq % rb == 0
    nqt = n_q // tq
    in_specs = [pl.BlockSpec((tq, qw), lambda b, g, i: (b * nqt + i, q_blk + g))]
    args = [q_arr]
    for k_arr, k_blk, v_arr, v_blk, nk in srcs:
        in_specs.append(pl.BlockSpec((nk, qw), lambda b, g, i, k_blk=k_blk: (b, k_blk + g)))
        in_specs.append(pl.BlockSpec((nk, LANES), lambda b, g, i, v_blk=v_blk: (b, v_blk + g)))
        args += [k_arr, v_arr]
    lambda_init = 0.0
    if diff is not None:
        lam_params, subln, lambda_init = diff
        in_specs.append(pl.BlockSpec((4, DA_HEAD_DIM), lambda b, g, i: (0, 0)))
        in_specs.append(pl.BlockSpec((1, LANES), lambda b, g, i: (0, 0)))
        args += [lam_params, subln.reshape(1, LANES)]
    kern = functools.partial(_attn_kernel, n_src=len(srcs), tk=tk, rb=rb, slice_mode=slice_mode,
                             sub_width=sub_width, diff_mode=diff is not None, lambda_init=lambda_init)
    scratch = 2 * ([pltpu.VMEM((tq, LANES), BF16)] + 2 * [pltpu.VMEM((tq, tk), F32)] + 2 * [pltpu.VMEM((tq, tk), BF16)]
                   + [pltpu.VMEM((tq, LANES), F32)] + 2 * [pltpu.VMEM((tq, LANES), F32)]
                   + [pltpu.VMEM((tq, 2 * LANES), F32)])
    return pl.pallas_call(
        kern, grid=(batch, groups, nqt), in_specs=in_specs,
        out_specs=pl.BlockSpec((tq, LANES), lambda b, g, i: (b * nqt + i, g)),
        out_shape=jax.ShapeDtypeStruct((batch * n_q, groups * LANES), BF16), scratch_shapes=scratch,
        compiler_params=_cparams(("parallel", "parallel", "arbitrary")), name=name)(*args)


def _na_kernel(q_ref, kc_ref, vc_ref, kl_ref, vl_ref, bias_ref, o_ref, s0_scr, s1_scr, p0_scr, p1_scr, *, n_rows, rb):
    i = pl.program_id(2)
    k0 = jnp.clip(i * NA_Q_ROWS - NA_WIN_ROWS // 2, 0, n_rows - NA_K_ROWS)
    start = pl.multiple_of(k0 * GRID_W, GRID_W * 4)
    nkw = NA_K_ROWS * GRID_W
    kw = kl_ref[pl.ds(start, nkw), :]
    kc = kc_ref[...]
    q = q_ref[...]
    tq = q.shape[0]
    n_all = nkw + kc.shape[0]
    lane = lax.broadcasted_iota(jnp.int32, (1, LANES), 1)
    s_scr, p_scr = (s0_scr, s1_scr), (p0_scr, p1_scr)
    dn = (((1,), (1,)), ((), ()))
    key_row = lax.broadcasted_iota(jnp.int32, (1, nkw), 1) // GRID_W
    for s in range(2):
        qs = jnp.where((lane < LANES // 2) if s == 0 else (lane >= LANES // 2), q, jnp.zeros_like(q))
        s_win = lax.dot_general(qs, kw, dn, preferred_element_type=F32)
        for a in range(NA_Q_ROWS):
            qr = i * NA_Q_ROWS + a
            r0 = jnp.clip(qr - NA_WIN_ROWS // 2, 0, n_rows - NA_WIN_ROWS)
            u = k0 - i * NA_Q_ROWS + (NA_WIN_ROWS - 1 - a + NA_BIAS_PAD)
            off = pl.multiple_of((u // 2) * LANES, LANES)
            parity = (NA_WIN_ROWS - 1 - a + NA_BIAS_PAD) % 2
            bias = bias_ref[s, parity, :, pl.ds(off, nkw)]
            seen = jnp.logical_and(key_row >= r0 - k0, key_row < r0 - k0 + NA_WIN_ROWS)
            rows = slice(a * GRID_W, (a + 1) * GRID_W)
            s_scr[s][rows, :nkw] = jnp.where(seen, s_win[rows, :] + bias, MASK_VALUE)
        s_scr[s][:, nkw:] = lax.dot_general(qs, kc, dn, preferred_element_type=F32)
    for s in range(2):
        for r in range(tq // rb):
            rows = slice(r * rb, (r + 1) * rb)
            sb = s_scr[s][rows, :]
            p_scr[s][rows, :] = jnp.exp2(sb - jnp.max(sb, axis=-1, keepdims=True)).astype(BF16)
    v_all = jnp.concatenate([vl_ref[pl.ds(start, nkw), :], vc_ref[...]], axis=0)
    vext = jnp.concatenate([v_all, jnp.ones((n_all, LANES), BF16)], axis=1)
    outs = []
    for s in range(2):
        pv = jnp.dot(p_scr[s][...], vext, preferred_element_type=F32)
        outs.append(pv[:, :LANES] / pv[:, LANES:])
    o_ref[...] = jnp.where(lane < LANES // 2, outs[0], outs[1]).astype(o_ref.dtype)


NA_BIAS_PAD = 8
NA_BIAS_LANES = 2048


def _na_bias_tables(rpb):
    h = rpb.shape[0]
    n_off = 2 * NA_WIN_ROWS - 1
    dc = np.arange(GRID_W)[None, :] - np.arange(GRID_W)[:, None] + NA_WIN_COLS - 1
    onehot = (dc[None] == np.arange(2 * NA_WIN_COLS - 1)[:, None, None]).astype(np.float32)
    toep = jnp.einsum('hrd,dqk->hqrk', rpb.astype(F32), jnp.asarray(onehot), precision=lax.Precision.HIGHEST)
    cols = np.arange(GRID_W)
    col_start = np.clip(cols - NA_WIN_COLS // 2, 0, GRID_W - NA_WIN_COLS)
    col_ok = (cols[None, :] >= col_start[:, None]) & (cols[None, :] < col_start[:, None] + NA_WIN_COLS)
    toep = jnp.where(jnp.asarray(col_ok)[None, :, None, :], toep, MASK_VALUE).reshape(h, GRID_W, n_off * GRID_W)
    copies = []
    for shift in (0, GRID_W):
        left = NA_BIAS_PAD * GRID_W - shift
        copies.append(jnp.pad(toep, ((0, 0), (0, 0), (left, NA_BIAS_LANES - left - n_off * GRID_W))))
    return jnp.stack(copies, axis=1)


def _na_attention(qkv_lat, qkv_ctx, bias_tab, *, batch, n_lat, n_ctx):
    d = NA_HEADS * NA_HEAD_DIM
    groups = d // LANES
    n_rows = n_lat // GRID_W
    nb = n_rows // NA_Q_ROWS
    tq = NA_Q_ROWS * GRID_W
    assert n_rows >= NA_K_ROWS and n_rows % NA_Q_ROWS == 0
    in_specs = [
        pl.BlockSpec((tq, LANES), lambda b, g, i: (b * nb + i, g)),
        pl.BlockSpec((n_ctx, LANES), lambda b, g, i: (b, groups + g)),
        pl.BlockSpec((n_ctx, LANES), lambda b, g, i: (b, 2 * groups + g)),
        pl.BlockSpec((n_lat, LANES), lambda b, g, i: (b, groups + g)),
        pl.BlockSpec((n_lat, LANES), lambda b, g, i: (b, 2 * groups + g)),
        pl.BlockSpec((2, 2, GRID_W, NA_BIAS_LANES), lambda b, g, i: (g, 0, 0, 0)),
    ]
    n_all = NA_K_ROWS * GRID_W + n_ctx
    return pl.pallas_call(
        functools.partial(_na_kernel, n_rows=n_rows, rb=16), grid=(batch, groups, nb), in_specs=in_specs,
        out_specs=pl.BlockSpec((tq, LANES), lambda b, g, i: (b * nb + i, g)),
        out_shape=jax.ShapeDtypeStruct((batch * n_lat, d), BF16),
        scratch_shapes=2 * [pltpu.VMEM((tq, n_all), F32)] + 2 * [pltpu.VMEM((tq, n_all), BF16)],
        compiler_params=_cparams(("parallel", "parallel", "arbitrary")), name="na_attn",
    )(qkv_lat, qkv_ctx, qkv_ctx, qkv_lat, qkv_lat, bias_tab)


def _dft_kernel(c_ref, ms_ref, ab_ref, mid_ref, o_ref, acc_ref, *, scale):
    n = pl.program_id(2)
    d = o_ref.shape[1]
    part = (jnp.dot(c_ref[...], ab_ref[:, :d], preferred_element_type=F32)
            + jnp.dot(ms_ref[...], ab_ref[:, d:], preferred_element_type=F32))

    @pl.when(n == 0)
    def _():
        acc_ref[...] = part

    @pl.when(n > 0)
    def _():
        acc_ref[...] += part

    @pl.when(n == pl.num_programs(2) - 1)
    def _():
        tk = o_ref.shape[0]
        k = pl.program_id(1) * tk + lax.broadcasted_iota(jnp.int32, (tk, 1), 0)
        sign = jnp.where(k % 2 == 0, 1.0, -1.0)
        o_ref[...] = ((acc_ref[...] + sign * mid_ref[...].astype(F32)) * scale).astype(o_ref.dtype)


def _fold_kernel(x_ref, src_ref, edge_ref, perm_ref, sign_ref, o_ref):
    i = pl.program_id(1)
    mirror = jnp.dot(perm_ref[...], src_ref[...], preferred_element_type=F32)
    first = lax.broadcasted_iota(jnp.int32, (mirror.shape[0], 1), 0) == 0
    edge = jnp.where(i > 0, edge_ref[0:1, :].astype(F32), 0.0)
    mirror = jnp.where(first, edge, mirror)
    o_ref[...] = (x_ref[...].astype(F32) + sign_ref[...] * mirror).astype(o_ref.dtype)


def _dft_fold(ab, *, batch, n_tok, d, name):
    half = n_tok // 2
    t = min(512, half)
    nt = n_tok // t
    perm = np.zeros((t, t), np.float32)
    perm[np.arange(1, t), t - np.arange(1, t)] = 1.0
    sign = jnp.concatenate([jnp.ones((d,), F32), -jnp.ones((d,), F32)]).reshape(1, 2 * d)
    edge_blocks = t // 8
    return pl.pallas_call(
        _fold_kernel, grid=(batch, nt // 2),
        in_specs=[pl.BlockSpec((t, 2 * d), lambda b, i: (b * nt + i, 0)),
                  pl.BlockSpec((t, 2 * d), lambda b, i: (b * nt + nt - 1 - i, 0)),
                  pl.BlockSpec((8, 2 * d), lambda b, i: ((b * nt + jnp.maximum(nt - i, 1) % nt) * edge_blocks, 0)),
                  pl.BlockSpec((t, t), lambda b, i: (0, 0)),
                  pl.BlockSpec((1, 2 * d), lambda b, i: (0, 0))],
        out_specs=pl.BlockSpec((t, 2 * d), lambda b, i: (b * (nt // 2) + i, 0)),
        out_shape=jax.ShapeDtypeStruct((batch * half, 2 * d), BF16),
        compiler_params=_cparams(("parallel", "parallel")), name=name)(ab, ab, ab, jnp.asarray(perm, BF16), sign)


def _pos_dft(cos_t, msin_t, ab, *, batch, n_tok, d, scale, name):
    half = n_tok // 2
    folded = _dft_fold(ab, batch=batch, n_tok=n_tok, d=d, name=name + "_fold")
    mid = ab.reshape(batch, n_tok, 2 * d)[:, half:half + 1, :d]
    tk = min(1024, n_tok)
    tn = min(1024, half)
    nk, nn = n_tok // tk, half // tn
    return pl.pallas_call(
        functools.partial(_dft_kernel, scale=scale), grid=(batch, nk, nn),
        in_specs=[pl.BlockSpec((tk, tn), lambda b, k, n: (k, n)),
                  pl.BlockSpec((tk, tn), lambda b, k, n: (k, n)),
                  pl.BlockSpec((tn, 2 * d), lambda b, k, n: (b * nn + n, 0)),
                  pl.BlockSpec((None, 1, d), lambda b, k, n: (b, 0, 0))],
        out_specs=pl.BlockSpec((tk, d), lambda b, k, n: (b * nk + k, 0)),
        out_shape=jax.ShapeDtypeStruct((batch * n_tok, d), BF16),
        scratch_shapes=[pltpu.VMEM((tk, d), F32)],
        compiler_params=_cparams(("parallel", "parallel", "arbitrary")), name=name)(cos_t, msin_t, folded, mid)


def _dft_tables(n):
    split = 64 if n % 64 == 0 and n > 64 else 1
    m = jnp.arange(n // 2, dtype=jnp.int32)[None, :]

    def phase(kvals):
        ang = ((kvals[:, None] * m) % n).astype(F32) * (2.0 * math.pi / n)
        return jnp.cos(ang), jnp.sin(ang)

    c1, s1 = phase(jnp.arange(n // split, dtype=jnp.int32) * split)
    c2, s2 = phase(jnp.arange(split, dtype=jnp.int32))
    c = c1[:, None, :] * c2[None, :, :] - s1[:, None, :] * s2[None, :, :]
    s = s1[:, None, :] * c2[None, :, :] + c1[:, None, :] * s2[None, :, :]
    return c.reshape(n, n // 2).astype(BF16), (-s).reshape(n, n // 2).astype(BF16)


def _channel_dft_weight(d):
    cg = d // FN_GROUPS
    idx = np.arange(cg)
    ang = 2.0 * np.pi * ((idx[:, None] * idx[None, :]) % cg) / cg
    eye = np.eye(FN_GROUPS)
    return jnp.asarray(np.concatenate([np.kron(eye, np.cos(ang)), np.kron(eye, np.sin(ang))], axis=1), F32)


def _router_kernel(x_ref, gain_ref, sc_ref, sh_ref, wr_ref, h_ref, aff_ref):
    x = x_ref[...]
    ms = jnp.mean(x * x, axis=-1, keepdims=True)
    h = x * lax.rsqrt(ms + NORM_EPS) * gain_ref[...]
    h = h * (1.0 + sc_ref[...]) + sh_ref[...]
    h_ref[...] = h.astype(h_ref.dtype)
    w = wr_ref[...]
    h_hi, w_hi = h.astype(BF16), w.astype(BF16)
    h_lo, w_lo = (h - h_hi.astype(F32)).astype(BF16), (w - w_hi.astype(F32)).astype(BF16)
    logits = (jnp.dot(h_hi, w_hi, preferred_element_type=F32) + jnp.dot(h_lo, w_hi, preferred_element_type=F32)
              + jnp.dot(h_hi, w_lo, preferred_element_type=F32))
    e = jnp.exp(logits - jnp.max(logits, axis=-1, keepdims=True))
    aff_ref[...] = e / jnp.sum(e, axis=-1, keepdims=True)


def _norm_router(x, gain, sc, sh, w_router, layer, *, rows_per_batch, tm=512):
    m, d = x.shape
    e = w_router.shape[-1]
    tm = min(tm, rows_per_batch)
    tpb = rows_per_batch // tm
    return pl.pallas_call(
        _router_kernel, grid=(m // tm,),
        in_specs=[pl.BlockSpec((tm, d), lambda i: (i, 0)),
                  pl.BlockSpec((1, d), lambda i: (0, 0)),
                  pl.BlockSpec((None, 1, d), lambda i: (i // tpb, 0, 0)),
                  pl.BlockSpec((None, 1, d), lambda i: (i // tpb, 0, 0)),
                  pl.BlockSpec((None, d, e), lambda i: (layer, 0, 0))],
        out_specs=[pl.BlockSpec((tm, d), lambda i: (i, 0)), pl.BlockSpec((tm, e), lambda i: (i, 0))],
        out_shape=[jax.ShapeDtypeStruct((m, d), BF16), jax.ShapeDtypeStruct((m, e), F32)],
        compiler_params=_cparams(("parallel",)), name="norm_router")(x, gain.reshape(1, d), sc, sh, w_router)


FFN_PIECE = 256


def _ffn_kernel(*refs, n_streams):
    x_refs = refs[:n_streams]
    wg_ref, wu_ref, wd_ref = refs[n_streams:n_streams + 3]
    gate_refs = refs[n_streams + 3:2 * n_streams + 3]
    o_refs = refs[2 * n_streams + 3:3 * n_streams + 3]
    acc_ref = refs[3 * n_streams + 3]
    f = pl.program_id(2)
    x = x_refs[0][...] if n_streams == 1 else jnp.concatenate([r[...] for r in x_refs], axis=0)
    @pl.when(f == 0)
    def _():
        acc_ref[...] = jnp.zeros(acc_ref.shape, F32)

    tf = wg_ref.shape[1]
    acts = []
    for c0 in range(0, tf, FFN_PIECE):
        a = jnp.dot(x, wg_ref[:, c0:c0 + FFN_PIECE].astype(BF16), preferred_element_type=F32)
        u = jnp.dot(x, wu_ref[:, c0:c0 + FFN_PIECE].astype(BF16), preferred_element_type=F32)
        acts.append(((a * (1.0 / (1.0 + jnp.exp(-a)))) * u).astype(BF16))
    d = wd_ref.shape[1]
    for n0 in range(0, d, FFN_PIECE):
        y = None
        for i, act in enumerate(acts):
            part = jnp.dot(act, wd_ref[i * FFN_PIECE:(i + 1) * FFN_PIECE, n0:n0 + FFN_PIECE].astype(BF16),
                           preferred_element_type=F32)
            y = part if y is None else y + part
        acc_ref[:, n0:n0 + FFN_PIECE] += y

    @pl.when(f == pl.num_programs(2) - 1)
    def _():
        row = 0
        for o_ref, g_ref in zip(o_refs, gate_refs):
            c = o_ref.shape[0]
            y = acc_ref[row:row + c, :] * g_ref[...]
            if o_ref.dtype == BF16:
                hi = y.astype(BF16)
                o_ref[:, :d] = hi
                o_ref[:, d:] = (y - hi.astype(F32)).astype(BF16)
            else:
                o_ref[...] = y
            row += c


def _moe_ffn(xs, gates, w_gate, w_up, w_down, layer, *, tf=512):
    b, e, _, d = xs[0].shape
    f = w_gate.shape[-1]
    tf = min(tf, f)
    slot = lambda c, w: pl.BlockSpec((None, None, c, w), lambda ei, bi, fi: (bi, ei, 0, 0))
    caps = [t.shape[2] for t in xs]
    in_specs = ([slot(c, d) for c in caps]
                + [pl.BlockSpec((None, None, d, tf), lambda ei, bi, fi: (layer, ei, 0, fi)),
                   pl.BlockSpec((None, None, d, tf), lambda ei, bi, fi: (layer, ei, 0, fi)),
                   pl.BlockSpec((None, None, tf, d), lambda ei, bi, fi: (layer, ei, fi, 0))]
                + [slot(c, 1) for c in caps])
    return pl.pallas_call(
        functools.partial(_ffn_kernel, n_streams=len(xs)), grid=(e, b, f // tf), in_specs=in_specs,
        out_specs=[slot(caps[0], 2 * d)] + [slot(c, d) for c in caps[1:]],
        out_shape=([jax.ShapeDtypeStruct((b, e, caps[0], 2 * d), BF16)]
                   + [jax.ShapeDtypeStruct((b, e, c, d), F32) for c in caps[1:]]),
        scratch_shapes=[pltpu.VMEM((sum(caps), d), F32)],
        compiler_params=_cparams(("parallel", "parallel", "arbitrary")), name="moe_ffn",
    )(*xs, w_gate, w_up, w_down, *gates)


def _count_upto(flags):
    lead, n = flags.shape[:-1], flags.shape[-1]
    blocks = flags.reshape(*lead, n // LANES, LANES).astype(BF16)
    tri = jnp.asarray(np.triu(np.ones((LANES, LANES), np.float32))).astype(BF16)
    inside = jnp.einsum('...i,ij->...j', blocks, tri, preferred_element_type=F32)
    totals = inside[..., -1]
    before = jnp.cumsum(totals, axis=-1) - totals
    return (inside + before[..., None]).reshape(*lead, n).astype(jnp.int32)


def _route(aff, h, batch, n_tok):
    e = aff.shape[-1]
    cap = EC_CAPACITY_FACTOR * n_tok // e
    aff_t = jnp.swapaxes(aff.reshape(batch, n_tok, e), 1, 2)
    gate, idx = lax.top_k(aff_t, cap)
    thr = gate[..., -1:]
    above, tied = aff_t > thr, aff_t == thr
    n_tied = cap - jnp.sum(above, axis=-1, keepdims=True, dtype=jnp.int32)
    chosen = above | (tied & (_count_upto(tied) <= n_tied))
    pos = jnp.where(chosen, _count_upto(chosen) - 1, -1)
    pos = jnp.swapaxes(pos, 1, 2).reshape(batch * n_tok, e)
    idx, gate = lax.sort((idx, gate), dimension=2, num_keys=1)
    xs = jax.vmap(lambda hb, ib: hb[ib])(h.reshape(batch, n_tok, -1), idx)
    return gate, idx, xs, pos


COMBINE_TOKENS = 256
COMBINE_WINDOW = 64
SLOT_ALIGN = 16


def _combine_kernel(lo_ref, nr_ref, x_ref, g_ref, pos_ref, fn_ref, y_hbm, o_ref, stage, sem, acc, *, n_tiles, n_total,
                    n_exp, cap, final_norm):
    w = COMBINE_WINDOW
    per = LANES // w
    tile = pl.program_id(0) * n_tiles + pl.program_id(1)
    slot = tile % 2
    lane = lax.broadcasted_iota(jnp.int32, (1, LANES), 1)

    def window(tl, e, r):
        first = lo_ref[tl * n_exp + e] + r * w
        return first, pl.multiple_of(jnp.minimum(first, cap - w), SLOT_ALIGN)

    def copy(tl, e, start, sl):
        return pltpu.make_async_copy(y_hbm.at[tl // n_tiles, e, pl.ds(start, w), :],
                                     stage.at[sl, pl.ds(e * w, w), :], sem.at[sl, e])

    def fetch(tl, r, sl):
        for e in range(n_exp):
            copy(tl, e, window(tl, e, r)[1], sl).start()

    def add_round(r):
        pos = pos_ref[...]
        blocks = []
        for e0 in range(0, n_exp, per):
            first, start = window(tile, e0, r)
            col = pos[:, e0:e0 + 1]
            for k in range(1, per):
                f_k, s_k = window(tile, e0 + k, r)
                here = lane >= k * w
                first, start = jnp.where(here, f_k, first), jnp.where(here, s_k, start)
                col = jnp.where(here, pos[:, e0 + k:e0 + k + 1], col)
            hit = jnp.logical_and(col - start == lane % w, col >= first)
            blocks.append(jnp.where(hit, 1.0, 0.0).astype(BF16))
        onehot = jnp.concatenate(blocks, axis=1)
        for e in range(n_exp):
            copy(tile, e, 0, slot).wait()
        d = acc.shape[1]
        acc[...] += (jnp.dot(onehot, stage[slot, :, :d], preferred_element_type=F32)
                     + jnp.dot(onehot, stage[slot, :, d:], preferred_element_type=F32))

    @pl.when(tile == 0)
    def _():
        fetch(tile, 0, slot)

    @pl.when(tile + 1 < n_total)
    def _():
        fetch(tile + 1, 0, 1 - slot)

    acc[...] = jnp.zeros(acc.shape, F32)
    add_round(0)

    @pl.loop(1, nr_ref[tile])
    def _(r):
        fetch(tile, r, slot)
        add_round(r)

    out = x_ref[...] + g_ref[...] * acc[...]
    if final_norm:
        out = out * lax.rsqrt(jnp.mean(out * out, axis=-1, keepdims=True) + NORM_EPS) * fn_ref[...]
    o_ref[...] = out


def _combine_lat(x, g2, y, idx, pos, batch, n_tok, final_gain=None):
    d = x.shape[-1]
    _, n_exp, cap = idx.shape
    tt, w = COMBINE_TOKENS, COMBINE_WINDOW
    assert n_tok % tt == 0 and cap % SLOT_ALIGN == 0 and w % SLOT_ALIGN == 0 and cap >= w
    nt = n_tok // tt
    bounds = jnp.arange(nt + 1, dtype=jnp.int32) * tt
    below = jnp.sum((idx[..., None] < bounds).astype(jnp.int32), axis=2)
    lo = (below[..., :-1] // SLOT_ALIGN) * SLOT_ALIGN
    rounds = jnp.maximum(jnp.max((below[..., 1:] - lo + w - 1) // w, axis=1), 1)
    lo_flat = jnp.transpose(lo, (0, 2, 1)).reshape(-1)
    grid_spec = pltpu.PrefetchScalarGridSpec(
        num_scalar_prefetch=2, grid=(batch, nt),
        in_specs=[pl.BlockSpec((tt, d), lambda b, t, lo_r, nr_r: (b * nt + t, 0)),
                  pl.BlockSpec((None, 1, d), lambda b, t, lo_r, nr_r: (b, 0, 0)),
                  pl.BlockSpec((tt, n_exp), lambda b, t, lo_r, nr_r: (b * nt + t, 0)),
                  pl.BlockSpec((1, d), lambda b, t, lo_r, nr_r: (0, 0)),
                  pl.BlockSpec(memory_space=pl.ANY)],
        out_specs=pl.BlockSpec((tt, d), lambda b, t, lo_r, nr_r: (b * nt + t, 0)),
        scratch_shapes=[pltpu.VMEM((2, n_exp * w, 2 * d), BF16), pltpu.SemaphoreType.DMA((2, n_exp)),
                        pltpu.VMEM((tt, d), F32)])
    return pl.pallas_call(
        functools.partial(_combine_kernel, n_tiles=nt, n_total=batch * nt, n_exp=n_exp, cap=cap,
                          final_norm=final_gain is not None),
        grid_spec=grid_spec, out_shape=jax.ShapeDtypeStruct(x.shape, F32),
        compiler_params=_cparams(("arbitrary", "arbitrary")), name="moe_combine",
    )(lo_flat, rounds.reshape(-1), x, g2, pos, (g2[0] if final_gain is None else final_gain.reshape(1, d)), y)


def _combine(x, g2, y, idx, batch, n_tok):
    d = x.shape[-1]
    upd = jax.vmap(lambda ib, yb: jnp.zeros((n_tok, d), F32).at[ib.reshape(-1)].add(yb.reshape(-1, d)))(idx, y)
    return (x.reshape(batch, n_tok, d) + g2 * upd).reshape(batch * n_tok, d)


def _rope_tables(n_tok, rot_dim, pair_of_lane):
    t = jnp.arange(n_tok)
    rows = (t // GRID_W).astype(F32)
    cols = (t % GRID_W).astype(F32)
    n_freq = rot_dim // 4
    inv_freq = ROPE_BASE ** (-jnp.arange(n_freq, dtype=F32) / n_freq)
    ang = jnp.concatenate([rows[:, None] * inv_freq, cols[:, None] * inv_freq], axis=-1)
    cos, sin = jnp.cos(ang), jnp.sin(ang)
    pair = np.concatenate([pair_of_lane, pair_of_lane])
    used = jnp.asarray(pair >= 0)[None]
    sign = jnp.asarray(np.where(np.arange(LANES) < LANES // 2, -1.0, 1.0).astype(np.float32))[None]
    cos_l = jnp.where(used, cos[:, np.maximum(pair, 0)], 1.0)
    sin_l = jnp.where(used, sin[:, np.maximum(pair, 0)] * sign, 0.0)
    return cos_l, sin_l


def kernel(x, c, ctx, c_ctx, ada_w, ada_b, norm_mix, norm_ffn, norm_final, da_w_qkv, da_w_o, da_lambda_q1,
           da_lambda_k1, da_lambda_q2, da_lambda_k2, da_subln, fn_w_o, na_w_qkv, na_w_o, na_rpb, mla_w_dq,
           mla_q_norm, mla_w_uq, mla_w_dkv, mla_kv_norm, mla_w_uk, mla_w_uv, mla_w_o, moe_w_router, moe_w_gate,
           moe_w_up, moe_w_down):
    batch, n_lat, d = x.shape
    n_ctx = ctx.shape[1]
    depth = ada_w.shape[0]
    n_mixers = 4
    xl = x.reshape(batch * n_lat, d)
    xc = ctx.reshape(batch * n_ctx, d)

    cond = jnp.concatenate([c, c_ctx[None], jnp.zeros((8 - batch - 1, d), F32)], axis=0)
    mods = [
        _fused_mm(cond, ada_w, w_index=i, bias=ada_b[i], silu_in=True, precision=lax.Precision.HIGHEST, tn=1024,
                  name="ada_mod")
        for i in range(depth)
    ]

    def chunks(i):
        lat = [mods[i][:batch, k * d:(k + 1) * d].reshape(batch, 1, d) for k in range(ADA_CHUNKS)]
        cx = [mods[i][batch:batch + 1, k * d:(k + 1) * d].reshape(1, 1, d) for k in range(ADA_CHUNKS)]
        return lat, cx

    for i in range(depth):
        kind, j = i % n_mixers, i // n_mixers
        keep_ctx = i < depth - 1
        (sh1, sc1, g1, sh2, sc2, g2), (csh1, csc1, cg1, csh2, csc2, cg2) = chunks(i)
        lat_in = dict(gain=norm_mix[i], mod=(sc1, sh1), rows_per_batch=n_lat)
        ctx_in = dict(gain=norm_mix[i], mod=(csc1, csh1), rows_per_batch=batch * n_ctx, tm=n_ctx)

        if kind == 0:
            lambda_init = 0.8 - 0.6 * math.exp(-0.3 * i)
            scale = jnp.concatenate([jnp.full((d,), LOG2E * DA_HEAD_DIM ** -0.5, F32), jnp.ones((2 * d,), F32)])
            groups = d // LANES
            half = DA_HEAD_DIM // 2
            w = da_w_qkv[j]

            def reorder(cols):
                return cols.reshape(d, groups, 2, 2, half).transpose(0, 1, 3, 2, 4).reshape(d, d)

            w = jnp.concatenate([reorder(w[:, :d]), reorder(w[:, d:2 * d]), w[:, 2 * d:]], axis=1)
            rope = _rope_tables(n_lat, DA_HEAD_DIM, np.arange(LANES // 2) % half)
            qkv_l = _fused_mm(xl, w, colscale=scale, rope=rope, rope_cols=2 * d, out_dtype=BF16, name="da_qkv_lat",
                              **lat_in)
            qkv_c = _fused_mm(xc, w, colscale=scale, out_dtype=BF16, name="da_qkv_ctx", **ctx_in)
            lam_params = jnp.stack([da_lambda_q1[j], da_lambda_k1[j], da_lambda_q2[j], da_lambda_k2[j]])
            diff = (lam_params, da_subln[j], lambda_init)
            ctx_kv = (qkv_c, groups, qkv_c, 2 * groups, n_ctx)
            lat_kv = (qkv_l, groups, qkv_l, 2 * groups, n_lat)
            o_l = _attention(qkv_l, 0, [ctx_kv, lat_kv], batch=batch, n_q=n_lat, groups=groups, sub_width=half,
                             diff=diff, name="da_attn_lat")
            o_c = _attention(qkv_c, 0, [ctx_kv], batch=batch, n_q=n_ctx, groups=groups, sub_width=half, diff=diff,
                             name="da_attn_ctx") if keep_ctx else None
            w_o = da_w_o
        elif kind == 1:
            w_cd = _channel_dft_weight(d)
            ab_l = _fused_mm(xl, w_cd, out_dtype=BF16, name="fn_chan_lat", **lat_in)
            scale_l = 1.0 / math.sqrt(n_lat * (d // FN_GROUPS))
            o_l = _pos_dft(*_dft_tables(n_lat), ab_l, batch=batch, n_tok=n_lat, d=d, scale=scale_l, name="fn_pos_lat")
            o_c = None
            if keep_ctx:
                ab_c = _fused_mm(xc, w_cd, out_dtype=BF16, name="fn_chan_ctx", **ctx_in)
                scale_c = 1.0 / math.sqrt(n_ctx * (d // FN_GROUPS))
                o_c = _pos_dft(*_dft_tables(n_ctx), ab_c, batch=batch, n_tok=n_ctx, d=d, scale=scale_c,
                               name="fn_pos_ctx")
            w_o = fn_w_o
        elif kind == 2:
            scale = jnp.concatenate([jnp.full((d,), LOG2E * NA_HEAD_DIM ** -0.5, F32), jnp.ones((2 * d,), F32)])
            qkv_l = _fused_mm(xl, na_w_qkv, w_index=j, colscale=scale, out_dtype=BF16, name="na_qkv_lat", **lat_in)
            qkv_c = _fused_mm(xc, na_w_qkv, w_index=j, colscale=scale, out_dtype=BF16, name="na_qkv_ctx", **ctx_in)
            groups = d // LANES
            bias_tab = _na_bias_tables(na_rpb[j] * LOG2E)
            o_l = _na_attention(qkv_l, qkv_c, bias_tab, batch=batch, n_lat=n_lat, n_ctx=n_ctx)
            o_c = _attention(qkv_c, 0, [(qkv_c, groups, qkv_c, 2 * groups, n_ctx)], batch=batch, n_q=n_ctx,
                             groups=groups, name="na_attn_ctx") if keep_ctx else None
            w_o = na_w_o
        else:
            q_rank = mla_w_dq.shape[-1]
            kv_rank = mla_w_uk.shape[-2]
            hq = MLA_NOPE_DIM + MLA_ROPE_DIM
            down_w = 3 * 384
            assert q_rank == 768 and kv_rank == 256
            w_down = jnp.concatenate([mla_w_dq[j], mla_w_dkv[j], jnp.zeros((d, down_w - q_rank - kv_rank - MLA_ROPE_DIM), F32)],
                                     axis=1)
            rh = MLA_ROPE_DIM // 2
            n_a = LANES // 2 - rh

            def head_lanes(nope, rope_part):
                z = jnp.zeros(nope.shape[:-1] + (LANES - hq,), F32)
                return jnp.concatenate([rope_part[..., :rh], nope[..., :n_a], rope_part[..., rh:], nope[..., n_a:], z],
                                       axis=-1)

            wq = mla_w_uq[j].reshape(q_rank, MLA_HEADS, hq)
            wq = head_lanes(wq[..., :MLA_NOPE_DIM], wq[..., MLA_NOPE_DIM:]).reshape(q_rank, MLA_HEADS * LANES)
            wk_nope = head_lanes(mla_w_uk[j].reshape(kv_rank, MLA_HEADS, MLA_NOPE_DIM),
                                 jnp.zeros((kv_rank, MLA_HEADS, MLA_ROPE_DIM), F32))
            eye = jnp.broadcast_to(jnp.eye(MLA_ROPE_DIM, dtype=F32)[:, None, :], (MLA_ROPE_DIM, MLA_HEADS, MLA_ROPE_DIM))
            wk_rope = head_lanes(jnp.zeros((MLA_ROPE_DIM, MLA_HEADS, MLA_NOPE_DIM), F32), eye)
            kw = MLA_HEADS * LANES
            vw = MLA_HEADS * MLA_V_DIM
            pad_rows = 384 - kv_rank - MLA_ROPE_DIM
            w_kv = jnp.concatenate([
                jnp.concatenate([wk_nope.reshape(kv_rank, kw), mla_w_uv[j]], axis=1),
                jnp.concatenate([wk_rope.reshape(MLA_ROPE_DIM, kw), jnp.zeros((MLA_ROPE_DIM, vw), F32)], axis=1),
                jnp.zeros((pad_rows, kw + vw), F32)], axis=0)
            kv_gain = jnp.concatenate([mla_kv_norm[j], jnp.ones((384 - kv_rank,), F32)])
            rope = _rope_tables(n_lat, MLA_ROPE_DIM, np.where(np.arange(LANES // 2) < rh, np.arange(LANES // 2), -1))
            qscale = jnp.full((kw,), LOG2E * hq ** -0.5, F32)

            def project(xs, inp, n_tok, rope_tabs, tag):
                low = _fused_mm(xs, w_down, tn=384, name="mla_down_" + tag, **inp)
                rp = dict(rope=rope_tabs) if rope_tabs is not None else {}
                tmx = inp.get("tm", 1024)
                q = _fused_mm(low, wq, a_cols=(0, q_rank), gain=mla_q_norm[j], colscale=qscale, rope_cols=kw,
                              rows_per_batch=inp["rows_per_batch"], tm=tmx, out_dtype=BF16, name="mla_q_" + tag, **rp)
                kv = _fused_mm(low, w_kv, a_cols=(q_rank, 384), gain=kv_gain, norm_cols=kv_rank, rope_cols=kw,
                               rows_per_batch=inp["rows_per_batch"], tm=tmx, out_dtype=BF16, name="mla_kv_" + tag, **rp)
                return q, kv

            q_l, kv_l = project(xl, lat_in, n_lat, rope, "lat")
            q_c, kv_c = project(xc, ctx_in, n_ctx, None, "ctx")
            groups = MLA_HEADS // 2
            ctx_kv = (kv_c, 0, kv_c, 2 * groups, n_ctx)
            lat_kv = (kv_l, 0, kv_l, 2 * groups, n_lat)
            o_l = _attention(q_l, 0, [ctx_kv, lat_kv], batch=batch, n_q=n_lat, groups=groups, slice_mode=True,
                             name="mla_attn_lat")
            o_c = _attention(q_c, 0, [ctx_kv], batch=batch, n_q=n_ctx, groups=groups, slice_mode=True,
                             name="mla_attn_ctx") if keep_ctx else None
            w_o = mla_w_o

        xl = _fused_mm(o_l, w_o, w_index=j, res=xl, gate=g1, rows_per_batch=n_lat, name="mix_out_lat")
        if keep_ctx:
            xc = _fused_mm(o_c, w_o, w_index=j, res=xc, gate=jnp.broadcast_to(cg1, (1, 1, d)),
                           rows_per_batch=batch * n_ctx, tm=n_ctx, name="mix_out_ctx")

        h_l, aff_l = _norm_router(xl, norm_ffn[i], sc2, sh2, moe_w_router, i, rows_per_batch=n_lat)
        gate_l, idx_l, xs_l, pos_l = _route(aff_l, h_l, batch, n_lat)
        xs, gates = [xs_l], [gate_l[..., None]]
        if keep_ctx:
            h_c, aff_c = _norm_router(xc, norm_ffn[i], csc2, csh2, moe_w_router, i, rows_per_batch=batch * n_ctx,
                                      tm=n_ctx)
            gate_c, idx_c, xs_c, _ = _route(aff_c, h_c, batch, n_ctx)
            xs.append(xs_c)
            gates.append(gate_c[..., None])
        ys = _moe_ffn(xs, gates, moe_w_gate, moe_w_up, moe_w_down, i)
        xl = _combine_lat(xl, g2, ys[0], idx_l, pos_l, batch, n_lat,
                          final_gain=norm_final if i == depth - 1 else None)
        if keep_ctx:
            xc = _combine(xc, cg2, ys[1], idx_c, batch, n_ctx)

    return xl.reshape(batch, n_lat, d)
```

```python
import functools
import math

import numpy as np
import jax
import jax.numpy as jnp
from jax import lax
from jax.experimental import pallas as pl
from jax.experimental.pallas import tpu as pltpu

F32 = jnp.float32
BF16 = jnp.bfloat16

LANES = 128
VMEM_LIMIT_BYTES = 56 * 1024 * 1024

GRID_W = 64
ROPE_BASE = 10000.0
NORM_EPS = 1e-6
ADA_CHUNKS = 6
DA_HEAD_DIM = 64
NA_HEADS = 16
NA_HEAD_DIM = 64
NA_WIN_ROWS = 8
NA_WIN_COLS = 16
NA_Q_ROWS = 8
NA_K_ROWS = 16
MLA_HEADS = 16
MLA_NOPE_DIM = 64
MLA_ROPE_DIM = 32
MLA_V_DIM = 64
FN_GROUPS = 4
EC_CAPACITY_FACTOR = 2
MASK_VALUE = -1e30
LOG2E = math.log2(math.e)


def _cparams(sem):
    return pltpu.CompilerParams(dimension_semantics=sem, vmem_limit_bytes=VMEM_LIMIT_BYTES)


def _rope_lanes(acc, cos, sin):
    outs = []
    for g in range(acc.shape[1] // LANES):
        blk = acc[:, g * LANES:(g + 1) * LANES]
        outs.append(blk * cos + pltpu.roll(blk, LANES // 2, 1) * sin)
    return outs[0] if len(outs) == 1 else jnp.concatenate(outs, axis=1)


def _mm_kernel(*refs, norm_cols, has_gain, has_mod, has_colscale, rope_tiles, has_bias, has_res,
               silu_in, precision):
    it = iter(refs)
    a_ref = next(it)
    w_ref = next(it)
    gain_ref = next(it) if has_gain else None
    sc_ref, sh_ref = (next(it), next(it)) if has_mod else (None, None)
    cs_ref = next(it) if has_colscale else None
    cos_ref, sin_ref = (next(it), next(it)) if rope_tiles else (None, None)
    bias_ref = next(it) if has_bias else None
    res_ref, gate_ref = (next(it), next(it)) if has_res else (None, None)
    o_ref = next(it)
    h_scr = next(it, None)
    j = pl.program_id(1)
    hi = precision is not None

    def prologue():
        a = a_ref[...].astype(F32)
        if silu_in:
            a = a * (1.0 / (1.0 + jnp.exp(-a)))
        if has_gain:
            k = a.shape[1]
            if norm_cols < k:
                col = lax.broadcasted_iota(jnp.int32, (1, k), 1)
                sq = jnp.where(col < norm_cols, a * a, 0.0)
            else:
                sq = a * a
            ms = jnp.sum(sq, axis=-1, keepdims=True) * (1.0 / norm_cols)
            y = a * lax.rsqrt(ms + NORM_EPS) * gain_ref[...]
            a = jnp.where(col < norm_cols, y, a) if norm_cols < k else y
        if has_mod:
            a = a * (1.0 + sc_ref[...]) + sh_ref[...]
        return a if hi else a.astype(BF16)

    if h_scr is not None:
        @pl.when(j == 0)
        def _():
            h_scr[...] = prologue()
        h = h_scr[...]
    else:
        h = a_ref[...] if hi else a_ref[...].astype(BF16)
    w = w_ref[...] if hi else w_ref[...].astype(BF16)
    acc = jnp.dot(h, w, preferred_element_type=F32, precision=precision)
    if has_colscale:
        acc = acc * cs_ref[...]

    def finish(v):
        if has_bias:
            v = v + bias_ref[...]
        if has_res:
            v = res_ref[...] + gate_ref[...] * v
        o_ref[...] = v.astype(o_ref.dtype)

    if rope_tiles:
        @pl.when(j < rope_tiles)
        def _():
            finish(_rope_lanes(acc, cos_ref[...], sin_ref[...]))

        @pl.when(j >= rope_tiles)
        def _():
            finish(acc)
    else:
        finish(acc)


def _fused_mm(a, w, *, w_index=None, a_cols=None, gain=None, norm_cols=None, mod=None, colscale=None, rope=None,
              rope_cols=0, bias=None, res=None, gate=None, silu_in=False, rows_per_batch=None,
              out_dtype=F32, tm=1024, tn=1024, precision=None, name="mm"):
    m = a.shape[0]
    k0, k = a_cols if a_cols is not None else (0, a.shape[1])
    if precision is None and w.dtype != BF16:
        w = (w[w_index] if w.ndim == 3 else w).astype(BF16)
    n = w.shape[-1]
    assert w.shape[-2] == k and k0 % k == 0
    tm = min(tm, m)
    tn = min(tn, n)
    assert m % tm == 0 and n % tn == 0
    rows_per_batch = rows_per_batch or m
    assert rows_per_batch % tm == 0
    tpb = rows_per_batch // tm
    has_prologue = gain is not None or mod is not None or silu_in
    in_specs = [pl.BlockSpec((tm, k), lambda i, j: (i, k0 // k))]
    args = [a]
    if w.ndim == 3:
        in_specs.append(pl.BlockSpec((None, k, tn), lambda i, j: (w_index, 0, j)))
    else:
        in_specs.append(pl.BlockSpec((k, tn), lambda i, j: (0, j)))
    args.append(w)
    if gain is not None:
        in_specs.append(pl.BlockSpec((1, k), lambda i, j: (0, 0)))
        args.append(gain.reshape(1, k).astype(F32))
    if mod is not None:
        for t in mod:
            in_specs.append(pl.BlockSpec((None, 1, k), lambda i, j: (i // tpb, 0, 0)))
            args.append(t)
    if colscale is not None:
        in_specs.append(pl.BlockSpec((1, tn), lambda i, j: (0, j)))
        args.append(colscale.reshape(1, n).astype(F32))
    rope_tiles = 0
    if rope is not None:
        assert rope_cols % tn == 0
        rope_tiles = rope_cols // tn
        for t in rope:
            in_specs.append(pl.BlockSpec((tm, LANES), lambda i, j: (i % tpb, 0)))
            args.append(t)
    if bias is not None:
        in_specs.append(pl.BlockSpec((1, tn), lambda i, j: (0, j)))
        args.append(bias.reshape(1, n).astype(F32))
    if res is not None:
        in_specs.append(pl.BlockSpec((tm, tn), lambda i, j: (i, j)))
        args.append(res)
        in_specs.append(pl.BlockSpec((None, 1, tn), lambda i, j: (i // tpb, 0, j)))
        args.append(gate)
    scratch = [pltpu.VMEM((tm, k), F32 if precision is not None else BF16)] if has_prologue else []
    kern = functools.partial(
        _mm_kernel, norm_cols=norm_cols or k, has_gain=gain is not None, has_mod=mod is not None,
        has_colscale=colscale is not None, rope_tiles=rope_tiles, has_bias=bias is not None,
        has_res=res is not None, silu_in=silu_in, precision=precision)
    return pl.pallas_call(
        kern, grid=(m // tm, n // tn), in_specs=in_specs,
        out_specs=pl.BlockSpec((tm, tn), lambda i, j: (i, j)),
        out_shape=jax.ShapeDtypeStruct((m, n), out_dtype), scratch_shapes=scratch,
        compiler_params=_cparams(("parallel", "arbitrary")), name=name)(*args)


def _attn_kernel(*refs, n_src, tk, rb, slice_mode, sub_width, diff_mode, lambda_init):
    q_ref = refs[0]
    k_refs, v_refs = refs[1:1 + 2 * n_src:2], refs[2:2 + 2 * n_src:2]
    pos = 1 + 2 * n_src
    if diff_mode:
        lam_ref, subln_ref = refs[pos], refs[pos + 1]
        pos += 2
    o_ref = refs[pos]
    per = [refs[pos + 1 + 9 * s:pos + 10 + 9 * s] for s in range(2)]
    qm_scr = [t[0] for t in per]
    s_scr = [t[1:3] for t in per]
    p_scr = [t[3:5] for t in per]
    m_scr = [t[5] for t in per]
    a_scr = [t[6:8] for t in per]
    acc_scr = [t[8] for t in per]
    tq = q_ref.shape[0]
    q = q_ref[...]
    lane = lax.broadcasted_iota(jnp.int32, (1, LANES), 1)
    for s in range(2):
        if slice_mode:
            qm_scr[s][...] = q[:, s * LANES:(s + 1) * LANES]
        else:
            qm_scr[s][...] = jnp.where((lane // sub_width) % 2 == s, q, jnp.zeros_like(q))
        m_scr[s][...] = jnp.full(m_scr[s].shape, MASK_VALUE, F32)
        acc_scr[s][...] = jnp.zeros(acc_scr[s].shape, F32)

    lengths = [r.shape[0] for r in k_refs]
    n_chunks = sum(lengths) // tk

    def chunk_of(src_refs, c):
        pieces, first, begin = [], c * tk, 0
        for ref, n in zip(src_refs, lengths):
            lo, hi = max(first, begin), min(first + tk, begin + n)
            if lo < hi:
                pieces.append(ref[lo - begin:hi - begin, :])
            begin += n
        return pieces[0] if len(pieces) == 1 else jnp.concatenate(pieces, axis=0)

    def scores(c, slot):
        k = chunk_of(k_refs, c)
        for s in range(2):
            ks = k[:, s * LANES:(s + 1) * LANES] if slice_mode else k
            s_scr[s][slot][...] = lax.dot_general(qm_scr[s][...], ks, (((1,), (1,)), ((), ())),
                                                  preferred_element_type=F32)

    def softmax(slot):
        for s in range(2):
            for r in range(tq // rb):
                rows = slice(r * rb, (r + 1) * rb)
                sb = s_scr[s][slot][rows, :]
                m_old = m_scr[s][rows, :]
                m_new = jnp.maximum(m_old, jnp.max(sb, axis=-1, keepdims=True))
                m_scr[s][rows, :] = m_new
                a_scr[s][slot][rows, :] = jnp.exp2(m_old - m_new)
                p_scr[s][slot][rows, :] = jnp.exp2(sb - jnp.tile(m_new, (1, tk // LANES))).astype(BF16)

    def weighted_sum(c, slot):
        v = chunk_of(v_refs, c)
        vext = jnp.concatenate([v, jnp.ones((tk, LANES), BF16)], axis=1)
        for s in range(2):
            alpha = a_scr[s][slot][...]
            pv = jnp.dot(p_scr[s][slot][...], vext, preferred_element_type=F32)
            acc_scr[s][...] = jnp.concatenate([alpha, alpha], axis=1) * acc_scr[s][...] + pv

    def step(c, slot):
        scores(c, slot)
        softmax(1 - slot)
        weighted_sum(c - 2, slot)

    scores(0, 0)
    if n_chunks > 1:
        scores(1, 1)
    softmax(0)
    for c in range(2, n_chunks):
        step(c, c % 2)
    if n_chunks > 1:
        softmax((n_chunks - 1) % 2)
        weighted_sum(n_chunks - 2, n_chunks % 2)
    weighted_sum(n_chunks - 1, (n_chunks - 1) % 2)
    o0 = acc_scr[0][:, :LANES] / acc_scr[0][:, LANES:]
    o1 = acc_scr[1][:, :LANES] / acc_scr[1][:, LANES:]
    if diff_mode:
        lp = lam_ref[...]
        lam = (jnp.exp(jnp.sum(lp[0:1] * lp[1:2], axis=-1, keepdims=True))
               - jnp.exp(jnp.sum(lp[2:3] * lp[3:4], axis=-1, keepdims=True)) + lambda_init)
        o = o0 - lam * o1
        ms = jnp.mean(o * o, axis=-1, keepdims=True)
        o = o * lax.rsqrt(ms + NORM_EPS) * subln_ref[...] * (1.0 - lambda_init)
    else:
        o = jnp.where(lane < LANES // 2, o0, o1)
    o_ref[...] = o.astype(o_ref.dtype)


def _attention(q_arr, q_blk, srcs, *, batch, n_q, groups, slice_mode=False, diff=None, sub_width=LANES // 2,
               tq=1024, tk=768, rb=16, name="attn"):
    qw = 2 * LANES if slice_mode else LANES
    n_k = sum(s[4] for s in srcs)
    tq = min(tq, n_q)
    tk = min(tk, n_k)
    assert n_q % tq == 0 and n_k % tk == 0 and tk % LANES == 0 and tq % rb == 0
    nqt = n_q // tq
    in_specs = [pl.BlockSpec((tq, qw), lambda b, g, i: (b * nqt + i, q_blk + g))]
    args = [q_arr]
    for k_arr, k_blk, v_arr, v_blk, nk in srcs:
        in_specs.append(pl.BlockSpec((nk, qw), lambda b, g, i, k_blk=k_blk: (b, k_blk + g)))
        in_specs.append(pl.BlockSpec((nk, LANES), lambda b, g, i, v_blk=v_blk: (b, v_blk + g)))
        args += [k_arr, v_arr]
    lambda_init = 0.0
    if diff is not None:
        lam_params, subln, lambda_init = diff
        in_specs.append(pl.BlockSpec((4, DA_HEAD_DIM), lambda b, g, i: (0, 0)))
        in_specs.append(pl.BlockSpec((1, LANES), lambda b, g, i: (0, 0)))
        args += [lam_params, subln.reshape(1, LANES)]
    kern = functools.partial(_attn_kernel, n_src=len(srcs), tk=tk, rb=rb, slice_mode=slice_mode,
                             sub_width=sub_width, diff_mode=diff is not None, lambda_init=lambda_init)
    scratch = 2 * ([pltpu.VMEM((tq, LANES), BF16)] + 2 * [pltpu.VMEM((tq, tk), F32)] + 2 * [pltpu.VMEM((tq, tk), BF16)]
                   + [pltpu.VMEM((tq, LANES), F32)] + 2 * [pltpu.VMEM((tq, LANES), F32)]
                   + [pltpu.VMEM((tq, 2 * LANES), F32)])
    return pl.pallas_call(
        kern, grid=(batch, groups, nqt), in_specs=in_specs,
        out_specs=pl.BlockSpec((tq, LANES), lambda b, g, i: (b * nqt + i, g)),
        out_shape=jax.ShapeDtypeStruct((batch * n_q, groups * LANES), BF16), scratch_shapes=scratch,
        compiler_params=_cparams(("parallel", "parallel", "arbitrary")), name=name)(*args)


def _na_kernel(q_ref, kc_ref, vc_ref, kl_ref, vl_ref, bias_ref, o_ref, s0_scr, s1_scr, p0_scr, p1_scr, *, n_rows, rb):
    i = pl.program_id(2)
    k0 = jnp.clip(i * NA_Q_ROWS - NA_WIN_ROWS // 2, 0, n_rows - NA_K_ROWS)
    start = pl.multiple_of(k0 * GRID_W, GRID_W * 4)
    nkw = NA_K_ROWS * GRID_W
    kw = kl_ref[pl.ds(start, nkw), :]
    kc = kc_ref[...]
    q = q_ref[...]
    tq = q.shape[0]
    n_all = nkw + kc.shape[0]
    lane = lax.broadcasted_iota(jnp.int32, (1, LANES), 1)
    s_scr, p_scr = (s0_scr, s1_scr), (p0_scr, p1_scr)
    dn = (((1,), (1,)), ((), ()))
    key_row = lax.broadcasted_iota(jnp.int32, (1, nkw), 1) // GRID_W
    for s in range(2):
        qs = jnp.where((lane < LANES // 2) if s == 0 else (lane >= LANES // 2), q, jnp.zeros_like(q))
        s_win = lax.dot_general(qs, kw, dn, preferred_element_type=F32)
        for a in range(NA_Q_ROWS):
            qr = i * NA_Q_ROWS + a
            r0 = jnp.clip(qr - NA_WIN_ROWS // 2, 0, n_rows - NA_WIN_ROWS)
            u = k0 - i * NA_Q_ROWS + (NA_WIN_ROWS - 1 - a + NA_BIAS_PAD)
            off = pl.multiple_of((u // 2) * LANES, LANES)
            parity = (NA_WIN_ROWS - 1 - a + NA_BIAS_PAD) % 2
            bias = bias_ref[s, parity, :, pl.ds(off, nkw)]
            seen = jnp.logical_and(key_row >= r0 - k0, key_row < r0 - k0 + NA_WIN_ROWS)
            rows = slice(a * GRID_W, (a + 1) * GRID_W)
            s_scr[s][rows, :nkw] = jnp.where(seen, s_win[rows, :] + bias, MASK_VALUE)
        s_scr[s][:, nkw:] = lax.dot_general(qs, kc, dn, preferred_element_type=F32)
    for s in range(2):
        for r in range(tq // rb):
            rows = slice(r * rb, (r + 1) * rb)
            sb = s_scr[s][rows, :]
            p_scr[s][rows, :] = jnp.exp2(sb - jnp.max(sb, axis=-1, keepdims=True)).astype(BF16)
    v_all = jnp.concatenate([vl_ref[pl.ds(start, nkw), :], vc_ref[...]], axis=0)
    vext = jnp.concatenate([v_all, jnp.ones((n_all, LANES), BF16)], axis=1)
    outs = []
    for s in range(2):
        pv = jnp.dot(p_scr[s][...], vext, preferred_element_type=F32)
        outs.append(pv[:, :LANES] / pv[:, LANES:])
    o_ref[...] = jnp.where(lane < LANES // 2, outs[0], outs[1]).astype(o_ref.dtype)


NA_BIAS_PAD = 8
NA_BIAS_LANES = 2048


def _na_bias_tables(rpb):
    h = rpb.shape[0]
    n_off = 2 * NA_WIN_ROWS - 1
    dc = np.arange(GRID_W)[None, :] - np.arange(GRID_W)[:, None] + NA_WIN_COLS - 1
    onehot = (dc[None] == np.arange(2 * NA_WIN_COLS - 1)[:, None, None]).astype(np.float32)
    toep = jnp.einsum('hrd,dqk->hqrk', rpb.astype(F32), jnp.asarray(onehot), precision=lax.Precision.HIGHEST)
    cols = np.arange(GRID_W)
    col_start = np.clip(cols - NA_WIN_COLS // 2, 0, GRID_W - NA_WIN_COLS)
    col_ok = (cols[None, :] >= col_start[:, None]) & (cols[None, :] < col_start[:, None] + NA_WIN_COLS)
    toep = jnp.where(jnp.asarray(col_ok)[None, :, None, :], toep, MASK_VALUE).reshape(h, GRID_W, n_off * GRID_W)
    copies = []
    for shift in (0, GRID_W):
        left = NA_BIAS_PAD * GRID_W - shift
        copies.append(jnp.pad(toep, ((0, 0), (0, 0), (left, NA_BIAS_LANES - left - n_off * GRID_W))))
    return jnp.stack(copies, axis=1)


def _na_attention(qkv_lat, qkv_ctx, bias_tab, *, batch, n_lat, n_ctx):
    d = NA_HEADS * NA_HEAD_DIM
    groups = d // LANES
    n_rows = n_lat // GRID_W
    nb = n_rows // NA_Q_ROWS
    tq = NA_Q_ROWS * GRID_W
    assert n_rows >= NA_K_ROWS and n_rows % NA_Q_ROWS == 0
    in_specs = [
        pl.BlockSpec((tq, LANES), lambda b, g, i: (b * nb + i, g)),
        pl.BlockSpec((n_ctx, LANES), lambda b, g, i: (b, groups + g)),
        pl.BlockSpec((n_ctx, LANES), lambda b, g, i: (b, 2 * groups + g)),
        pl.BlockSpec((n_lat, LANES), lambda b, g, i: (b, groups + g)),
        pl.BlockSpec((n_lat, LANES), lambda b, g, i: (b, 2 * groups + g)),
        pl.BlockSpec((2, 2, GRID_W, NA_BIAS_LANES), lambda b, g, i: (g, 0, 0, 0)),
    ]
    n_all = NA_K_ROWS * GRID_W + n_ctx
    return pl.pallas_call(
        functools.partial(_na_kernel, n_rows=n_rows, rb=16), grid=(batch, groups, nb), in_specs=in_specs,
        out_specs=pl.BlockSpec((tq, LANES), lambda b, g, i: (b * nb + i, g)),
        out_shape=jax.ShapeDtypeStruct((batch * n_lat, d), BF16),
        scratch_shapes=2 * [pltpu.VMEM((tq, n_all), F32)] + 2 * [pltpu.VMEM((tq, n_all), BF16)],
        compiler_params=_cparams(("parallel", "parallel", "arbitrary")), name="na_attn",
    )(qkv_lat, qkv_ctx, qkv_ctx, qkv_lat, qkv_lat, bias_tab)


def _dft_kernel(c_ref, ms_ref, ab_ref, mid_ref, o_ref, acc_ref, *, scale):
    n = pl.program_id(2)
    d = o_ref.shape[1]
    part = (jnp.dot(c_ref[...], ab_ref[:, :d], preferred_element_type=F32)
            + jnp.dot(ms_ref[...], ab_ref[:, d:], preferred_element_type=F32))

    @pl.when(n == 0)
    def _():
        acc_ref[...] = part

    @pl.when(n > 0)
    def _():
        acc_ref[...] += part

    @pl.when(n == pl.num_programs(2) - 1)
    def _():
        tk = o_ref.shape[0]
        k = pl.program_id(1) * tk + lax.broadcasted_iota(jnp.int32, (tk, 1), 0)
        sign = jnp.where(k % 2 == 0, 1.0, -1.0)
        o_ref[...] = ((acc_ref[...] + sign * mid_ref[...].astype(F32)) * scale).astype(o_ref.dtype)


def _fold_kernel(x_ref, src_ref, edge_ref, perm_ref, sign_ref, o_ref):
    i = pl.program_id(1)
    mirror = jnp.dot(perm_ref[...], src_ref[...], preferred_element_type=F32)
    first = lax.broadcasted_iota(jnp.int32, (mirror.shape[0], 1), 0) == 0
    edge = jnp.where(i > 0, edge_ref[0:1, :].astype(F32), 0.0)
    mirror = jnp.where(first, edge, mirror)
    o_ref[...] = (x_ref[...].astype(F32) + sign_ref[...] * mirror).astype(o_ref.dtype)


def _dft_fold(ab, *, batch, n_tok, d, name):
    half = n_tok // 2
    t = min(512, half)
    nt = n_tok // t
    perm = np.zeros((t, t), np.float32)
    perm[np.arange(1, t), t - np.arange(1, t)] = 1.0
    sign = jnp.concatenate([jnp.ones((d,), F32), -jnp.ones((d,), F32)]).reshape(1, 2 * d)
    edge_blocks = t // 8
    return pl.pallas_call(
        _fold_kernel, grid=(batch, nt // 2),
        in_specs=[pl.BlockSpec((t, 2 * d), lambda b, i: (b * nt + i, 0)),
                  pl.BlockSpec((t, 2 * d), lambda b, i: (b * nt + nt - 1 - i, 0)),
                  pl.BlockSpec((8, 2 * d), lambda b, i: ((b * nt + jnp.maximum(nt - i, 1) % nt) * edge_blocks, 0)),
                  pl.BlockSpec((t, t), lambda b, i: (0, 0)),
                  pl.BlockSpec((1, 2 * d), lambda b, i: (0, 0))],
        out_specs=pl.BlockSpec((t, 2 * d), lambda b, i: (b * (nt // 2) + i, 0)),
        out_shape=jax.ShapeDtypeStruct((batch * half, 2 * d), BF16),
        compiler_params=_cparams(("parallel", "parallel")), name=name)(ab, ab, ab, jnp.asarray(perm, BF16), sign)


def _pos_dft(cos_t, msin_t, ab, *, batch, n_tok, d, scale, name):
    half = n_tok // 2
    folded = _dft_fold(ab, batch=batch, n_tok=n_tok, d=d, name=name + "_fold")
    mid = ab.reshape(batch, n_tok, 2 * d)[:, half:half + 1, :d]
    tk = min(1024, n_tok)
    tn = min(1024, half)
    nk, nn = n_tok // tk, half // tn
    return pl.pallas_call(
        functools.partial(_dft_kernel, scale=scale), grid=(batch, nk, nn),
        in_specs=[pl.BlockSpec((tk, tn), lambda b, k, n: (k, n)),
                  pl.BlockSpec((tk, tn), lambda b, k, n: (k, n)),
                  pl.BlockSpec((tn, 2 * d), lambda b, k, n: (b * nn + n, 0)),
                  pl.BlockSpec((None, 1, d), lambda b, k, n: (b, 0, 0))],
        out_specs=pl.BlockSpec((tk, d), lambda b, k, n: (b * nk + k, 0)),
        out_shape=jax.ShapeDtypeStruct((batch * n_tok, d), BF16),
        scratch_shapes=[pltpu.VMEM((tk, d), F32)],
        compiler_params=_cparams(("parallel", "parallel", "arbitrary")), name=name)(cos_t, msin_t, folded, mid)


def _dft_tables(n):
    split = 64 if n % 64 == 0 and n > 64 else 1
    m = jnp.arange(n // 2, dtype=jnp.int32)[None, :]

    def phase(kvals):
        ang = ((kvals[:, None] * m) % n).astype(F32) * (2.0 * math.pi / n)
        return jnp.cos(ang), jnp.sin(ang)

    c1, s1 = phase(jnp.arange(n // split, dtype=jnp.int32) * split)
    c2, s2 = phase(jnp.arange(split, dtype=jnp.int32))
    c = c1[:, None, :] * c2[None, :, :] - s1[:, None, :] * s2[None, :, :]
    s = s1[:, None, :] * c2[None, :, :] + c1[:, None, :] * s2[None, :, :]
    return c.reshape(n, n // 2).astype(BF16), (-s).reshape(n, n // 2).astype(BF16)


def _channel_dft_weight(d):
    cg = d // FN_GROUPS
    idx = np.arange(cg)
    ang = 2.0 * np.pi * ((idx[:, None] * idx[None, :]) % cg) / cg
    eye = np.eye(FN_GROUPS)
    return jnp.asarray(np.concatenate([np.kron(eye, np.cos(ang)), np.kron(eye, np.sin(ang))], axis=1), F32)


def _router_kernel(x_ref, gain_ref, sc_ref, sh_ref, wr_ref, h_ref, aff_ref):
    x = x_ref[...]
    ms = jnp.mean(x * x, axis=-1, keepdims=True)
    h = x * lax.rsqrt(ms + NORM_EPS) * gain_ref[...]
    h = h * (1.0 + sc_ref[...]) + sh_ref[...]
    h_ref[...] = h.astype(h_ref.dtype)
    w = wr_ref[...]
    h_hi, w_hi = h.astype(BF16), w.astype(BF16)
    h_lo, w_lo = (h - h_hi.astype(F32)).astype(BF16), (w - w_hi.astype(F32)).astype(BF16)
    logits = (jnp.dot(h_hi, w_hi, preferred_element_type=F32) + jnp.dot(h_lo, w_hi, preferred_element_type=F32)
              + jnp.dot(h_hi, w_lo, preferred_element_type=F32))
    e = jnp.exp(logits - jnp.max(logits, axis=-1, keepdims=True))
    aff_ref[...] = e / jnp.sum(e, axis=-1, keepdims=True)


def _norm_router(x, gain, sc, sh, w_router, layer, *, rows_per_batch, tm=512):
    m, d = x.shape
    e = w_router.shape[-1]
    tm = min(tm, rows_per_batch)
    tpb = rows_per_batch // tm
    return pl.pallas_call(
        _router_kernel, grid=(m // tm,),
        in_specs=[pl.BlockSpec((tm, d), lambda i: (i, 0)),
                  pl.BlockSpec((1, d), lambda i: (0, 0)),
                  pl.BlockSpec((None, 1, d), lambda i: (i // tpb, 0, 0)),
                  pl.BlockSpec((None, 1, d), lambda i: (i // tpb, 0, 0)),
                  pl.BlockSpec((None, d, e), lambda i: (layer, 0, 0))],
        out_specs=[pl.BlockSpec((tm, d), lambda i: (i, 0)), pl.BlockSpec((tm, e), lambda i: (i, 0))],
        out_shape=[jax.ShapeDtypeStruct((m, d), BF16), jax.ShapeDtypeStruct((m, e), F32)],
        compiler_params=_cparams(("parallel",)), name="norm_router")(x, gain.reshape(1, d), sc, sh, w_router)


FFN_PIECE = 256


def _ffn_kernel(*refs, n_streams):
    x_refs = refs[:n_streams]
    wg_ref, wu_ref, wd_ref = refs[n_streams:n_streams + 3]
    gate_refs = refs[n_streams + 3:2 * n_streams + 3]
    o_refs = refs[2 * n_streams + 3:3 * n_streams + 3]
    acc_ref, wg_bf, wu_bf, wd_bf = refs[3 * n_streams + 3:3 * n_streams + 7]
    f = pl.program_id(2)
    x = x_refs[0][...] if n_streams == 1 else jnp.concatenate([r[...] for r in x_refs], axis=0)

    @pl.when(f == 0)
    def _():
        acc_ref[...] = jnp.zeros(acc_ref.shape, F32)

    @pl.when(pl.program_id(1) == 0)
    def _():
        wg_bf[f] = wg_ref[...].astype(BF16)
        wu_bf[f] = wu_ref[...].astype(BF16)
        wd_bf[f] = wd_ref[...].astype(BF16)

    tf = wg_ref.shape[1]
    acts = []
    for c0 in range(0, tf, FFN_PIECE):
        a = jnp.dot(x, wg_bf[f, :, c0:c0 + FFN_PIECE], preferred_element_type=F32)
        u = jnp.dot(x, wu_bf[f, :, c0:c0 + FFN_PIECE], preferred_element_type=F32)
        acts.append(((a * (1.0 / (1.0 + jnp.exp(-a)))) * u).astype(BF16))
    d = wd_ref.shape[1]
    for n0 in range(0, d, FFN_PIECE):
        y = None
        for i, act in enumerate(acts):
            part = jnp.dot(act, wd_bf[f, i * FFN_PIECE:(i + 1) * FFN_PIECE, n0:n0 + FFN_PIECE],
                           preferred_element_type=F32)
            y = part if y is None else y + part
        acc_ref[:, n0:n0 + FFN_PIECE] += y

    @pl.when(f == pl.num_programs(2) - 1)
    def _():
        row = 0
        for o_ref, g_ref in zip(o_refs, gate_refs):
            c = o_ref.shape[0]
            y = acc_ref[row:row + c, :] * g_ref[...]
            if o_ref.dtype == BF16:
                hi = y.astype(BF16)
                o_ref[:, :d] = hi
                o_ref[:, d:] = (y - hi.astype(F32)).astype(BF16)
            else:
                o_ref[...] = y
            row += c


def _moe_ffn(xs, gates, w_gate, w_up, w_down, layer, *, tf=512):
    b, e, _, d = xs[0].shape
    f = w_gate.shape[-1]
    tf = min(tf, f)
    slot = lambda c, w: pl.BlockSpec((None, None, c, w), lambda ei, bi, fi: (bi, ei, 0, 0))
    caps = [t.shape[2] for t in xs]
    nf = f // tf

    def w_tile(bi, fi):
        return jnp.where(bi == 0, fi, nf - 1)

    in_specs = ([slot(c, d) for c in caps]
                + [pl.BlockSpec((None, None, d, tf), lambda ei, bi, fi: (layer, ei, 0, w_tile(bi, fi))),
                   pl.BlockSpec((None, None, d, tf), lambda ei, bi, fi: (layer, ei, 0, w_tile(bi, fi))),
                   pl.BlockSpec((None, None, tf, d), lambda ei, bi, fi: (layer, ei, w_tile(bi, fi), 0))]
                + [slot(c, 1) for c in caps])
    return pl.pallas_call(
        functools.partial(_ffn_kernel, n_streams=len(xs)), grid=(e, b, nf), in_specs=in_specs,
        out_specs=[slot(caps[0], 2 * d)] + [slot(c, d) for c in caps[1:]],
        out_shape=([jax.ShapeDtypeStruct((b, e, caps[0], 2 * d), BF16)]
                   + [jax.ShapeDtypeStruct((b, e, c, d), F32) for c in caps[1:]]),
        scratch_shapes=[pltpu.VMEM((sum(caps), d), F32), pltpu.VMEM((nf, d, tf), BF16), pltpu.VMEM((nf, d, tf), BF16),
                        pltpu.VMEM((nf, tf, d), BF16)],
        compiler_params=_cparams(("parallel", "arbitrary", "arbitrary")), name="moe_ffn",
    )(*xs, w_gate, w_up, w_down, *gates)


def _count_upto(flags):
    lead, n = flags.shape[:-1], flags.shape[-1]
    blocks = flags.reshape(*lead, n // LANES, LANES).astype(BF16)
    tri = jnp.asarray(np.triu(np.ones((LANES, LANES), np.float32))).astype(BF16)
    inside = jnp.einsum('...i,ij->...j', blocks, tri, preferred_element_type=F32)
    totals = inside[..., -1]
    before = jnp.cumsum(totals, axis=-1) - totals
    return (inside + before[..., None]).reshape(*lead, n).astype(jnp.int32)


def _route(aff, h, batch, n_tok):
    e = aff.shape[-1]
    cap = EC_CAPACITY_FACTOR * n_tok // e
    aff_t = jnp.swapaxes(aff.reshape(batch, n_tok, e), 1, 2)
    gate, idx = lax.top_k(aff_t, cap)
    thr = gate[..., -1:]
    above, tied = aff_t > thr, aff_t == thr
    n_tied = cap - jnp.sum(above, axis=-1, keepdims=True, dtype=jnp.int32)
    chosen = above | (tied & (_count_upto(tied) <= n_tied))
    pos = jnp.where(chosen, _count_upto(chosen) - 1, -1)
    pos = jnp.swapaxes(pos, 1, 2).reshape(batch * n_tok, e)
    idx, gate = lax.sort((idx, gate), dimension=2, num_keys=1)
    xs = jax.vmap(lambda hb, ib: hb[ib])(h.reshape(batch, n_tok, -1), idx)
    return gate, idx, xs, pos


COMBINE_TOKENS = 256
COMBINE_WINDOW = 64
SLOT_ALIGN = 16


def _combine_kernel(lo_ref, nr_ref, x_ref, g_ref, pos_ref, fn_ref, y_hbm, o_ref, stage, sem, acc, *, n_tiles, n_total,
                    n_exp, cap, final_norm):
    w = COMBINE_WINDOW
    per = LANES // w
    tile = pl.program_id(0) * n_tiles + pl.program_id(1)
    slot = tile % 2
    lane = lax.broadcasted_iota(jnp.int32, (1, LANES), 1)

    def window(tl, e, r):
        first = lo_ref[tl * n_exp + e] + r * w
        return first, pl.multiple_of(jnp.minimum(first, cap - w), SLOT_ALIGN)

    def copy(tl, e, start, sl):
        return pltpu.make_async_copy(y_hbm.at[tl // n_tiles, e, pl.ds(start, w), :],
                                     stage.at[sl, pl.ds(e * w, w), :], sem.at[sl, e])

    def fetch(tl, r, sl):
        for e in range(n_exp):
            copy(tl, e, window(tl, e, r)[1], sl).start()

    def add_round(r):
        pos = pos_ref[...]
        blocks = []
        for e0 in range(0, n_exp, per):
            first, start = window(tile, e0, r)
            col = pos[:, e0:e0 + 1]
            for k in range(1, per):
                f_k, s_k = window(tile, e0 + k, r)
                here = lane >= k * w
                first, start = jnp.where(here, f_k, first), jnp.where(here, s_k, start)
                col = jnp.where(here, pos[:, e0 + k:e0 + k + 1], col)
            hit = jnp.logical_and(col - start == lane % w, col >= first)
            blocks.append(jnp.where(hit, 1.0, 0.0).astype(BF16))
        onehot = jnp.concatenate(blocks, axis=1)
        for e in range(n_exp):
            copy(tile, e, 0, slot).wait()
        d = acc.shape[1]
        acc[...] += (jnp.dot(onehot, stage[slot, :, :d], preferred_element_type=F32)
                     + jnp.dot(onehot, stage[slot, :, d:], preferred_element_type=F32))

    @pl.when(tile == 0)
    def _():
        fetch(tile, 0, slot)

    @pl.when(tile + 1 < n_total)
    def _():
        fetch(tile + 1, 0, 1 - slot)

    acc[...] = jnp.zeros(acc.shape, F32)
    add_round(0)

    @pl.loop(1, nr_ref[tile])
    def _(r):
        fetch(tile, r, slot)
        add_round(r)

    out = x_ref[...] + g_ref[...] * acc[...]
    if final_norm:
        out = out * lax.rsqrt(jnp.mean(out * out, axis=-1, keepdims=True) + NORM_EPS) * fn_ref[...]
    o_ref[...] = out


def _combine_lat(x, g2, y, idx, pos, batch, n_tok, final_gain=None):
    d = x.shape[-1]
    _, n_exp, cap = idx.shape
    tt, w = COMBINE_TOKENS, COMBINE_WINDOW
    assert n_tok % tt == 0 and cap % SLOT_ALIGN == 0 and w % SLOT_ALIGN == 0 and cap >= w
    nt = n_tok // tt
    bounds = jnp.arange(nt + 1, dtype=jnp.int32) * tt
    below = jnp.sum((idx[..., None] < bounds).astype(jnp.int32), axis=2)
    lo = (below[..., :-1] // SLOT_ALIGN) * SLOT_ALIGN
    rounds = jnp.maximum(jnp.max((below[..., 1:] - lo + w - 1) // w, axis=1), 1)
    lo_flat = jnp.transpose(lo, (0, 2, 1)).reshape(-1)
    grid_spec = pltpu.PrefetchScalarGridSpec(
        num_scalar_prefetch=2, grid=(batch, nt),
        in_specs=[pl.BlockSpec((tt, d), lambda b, t, lo_r, nr_r: (b * nt + t, 0)),
                  pl.BlockSpec((None, 1, d), lambda b, t, lo_r, nr_r: (b, 0, 0)),
                  pl.BlockSpec((tt, n_exp), lambda b, t, lo_r, nr_r: (b * nt + t, 0)),
                  pl.BlockSpec((1, d), lambda b, t, lo_r, nr_r: (0, 0)),
                  pl.BlockSpec(memory_space=pl.ANY)],
        out_specs=pl.BlockSpec((tt, d), lambda b, t, lo_r, nr_r: (b * nt + t, 0)),
        scratch_shapes=[pltpu.VMEM((2, n_exp * w, 2 * d), BF16), pltpu.SemaphoreType.DMA((2, n_exp)),
                        pltpu.VMEM((tt, d), F32)])
    return pl.pallas_call(
        functools.partial(_combine_kernel, n_tiles=nt, n_total=batch * nt, n_exp=n_exp, cap=cap,
                          final_norm=final_gain is not None),
        grid_spec=grid_spec, out_shape=jax.ShapeDtypeStruct(x.shape, F32),
        compiler_params=_cparams(("arbitrary", "arbitrary")), name="moe_combine",
    )(lo_flat, rounds.reshape(-1), x, g2, pos, (g2[0] if final_gain is None else final_gain.reshape(1, d)), y)


def _combine(x, g2, y, idx, batch, n_tok):
    d = x.shape[-1]
    upd = jax.vmap(lambda ib, yb: jnp.zeros((n_tok, d), F32).at[ib.reshape(-1)].add(yb.reshape(-1, d)))(idx, y)
    return (x.reshape(batch, n_tok, d) + g2 * upd).reshape(batch * n_tok, d)


def _rope_tables(n_tok, rot_dim, pair_of_lane):
    t = jnp.arange(n_tok)
    rows = (t // GRID_W).astype(F32)
    cols = (t % GRID_W).astype(F32)
    n_freq = rot_dim // 4
    inv_freq = ROPE_BASE ** (-jnp.arange(n_freq, dtype=F32) / n_freq)
    ang = jnp.concatenate([rows[:, None] * inv_freq, cols[:, None] * inv_freq], axis=-1)
    cos, sin = jnp.cos(ang), jnp.sin(ang)
    pair = np.concatenate([pair_of_lane, pair_of_lane])
    used = jnp.asarray(pair >= 0)[None]
    sign = jnp.asarray(np.where(np.arange(LANES) < LANES // 2, -1.0, 1.0).astype(np.float32))[None]
    cos_l = jnp.where(used, cos[:, np.maximum(pair, 0)], 1.0)
    sin_l = jnp.where(used, sin[:, np.maximum(pair, 0)] * sign, 0.0)
    return cos_l, sin_l


def kernel(x, c, ctx, c_ctx, ada_w, ada_b, norm_mix, norm_ffn, norm_final, da_w_qkv, da_w_o, da_lambda_q1,
           da_lambda_k1, da_lambda_q2, da_lambda_k2, da_subln, fn_w_o, na_w_qkv, na_w_o, na_rpb, mla_w_dq,
           mla_q_norm, mla_w_uq, mla_w_dkv, mla_kv_norm, mla_w_uk, mla_w_uv, mla_w_o, moe_w_router, moe_w_gate,
           moe_w_up, moe_w_down):
    batch, n_lat, d = x.shape
    n_ctx = ctx.shape[1]
    depth = ada_w.shape[0]
    n_mixers = 4
    xl = x.reshape(batch * n_lat, d)
    xc = ctx.reshape(batch * n_ctx, d)

    cond = jnp.concatenate([c, c_ctx[None], jnp.zeros((8 - batch - 1, d), F32)], axis=0)
    mods = [
        _fused_mm(cond, ada_w, w_index=i, bias=ada_b[i], silu_in=True, precision=lax.Precision.HIGHEST, tn=1024,
                  name="ada_mod")
        for i in range(depth)
    ]

    def chunks(i):
        lat = [mods[i][:batch, k * d:(k + 1) * d].reshape(batch, 1, d) for k in range(ADA_CHUNKS)]
        cx = [mods[i][batch:batch + 1, k * d:(k + 1) * d].reshape(1, 1, d) for k in range(ADA_CHUNKS)]
        return lat, cx

    for i in range(depth):
        kind, j = i % n_mixers, i // n_mixers
        keep_ctx = i < depth - 1
        (sh1, sc1, g1, sh2, sc2, g2), (csh1, csc1, cg1, csh2, csc2, cg2) = chunks(i)
        lat_in = dict(gain=norm_mix[i], mod=(sc1, sh1), rows_per_batch=n_lat)
        ctx_in = dict(gain=norm_mix[i], mod=(csc1, csh1), rows_per_batch=batch * n_ctx, tm=n_ctx)

        if kind == 0:
            lambda_init = 0.8 - 0.6 * math.exp(-0.3 * i)
            scale = jnp.concatenate([jnp.full((d,), LOG2E * DA_HEAD_DIM ** -0.5, F32), jnp.ones((2 * d,), F32)])
            groups = d // LANES
            half = DA_HEAD_DIM // 2
            w = da_w_qkv[j]

            def reorder(cols):
                return cols.reshape(d, groups, 2, 2, half).transpose(0, 1, 3, 2, 4).reshape(d, d)

            w = jnp.concatenate([reorder(w[:, :d]), reorder(w[:, d:2 * d]), w[:, 2 * d:]], axis=1)
            rope = _rope_tables(n_lat, DA_HEAD_DIM, np.arange(LANES // 2) % half)
            qkv_l = _fused_mm(xl, w, colscale=scale, rope=rope, rope_cols=2 * d, out_dtype=BF16, name="da_qkv_lat",
                              **lat_in)
            qkv_c = _fused_mm(xc, w, colscale=scale, out_dtype=BF16, name="da_qkv_ctx", **ctx_in)
            lam_params = jnp.stack([da_lambda_q1[j], da_lambda_k1[j], da_lambda_q2[j], da_lambda_k2[j]])
            diff = (lam_params, da_subln[j], lambda_init)
            ctx_kv = (qkv_c, groups, qkv_c, 2 * groups, n_ctx)
            lat_kv = (qkv_l, groups, qkv_l, 2 * groups, n_lat)
            o_l = _attention(qkv_l, 0, [ctx_kv, lat_kv], batch=batch, n_q=n_lat, groups=groups, sub_width=half,
                             diff=diff, name="da_attn_lat")
            o_c = _attention(qkv_c, 0, [ctx_kv], batch=batch, n_q=n_ctx, groups=groups, sub_width=half, diff=diff,
                             name="da_attn_ctx") if keep_ctx else None
            w_o = da_w_o
        elif kind == 1:
            w_cd = _channel_dft_weight(d)
            ab_l = _fused_mm(xl, w_cd, out_dtype=BF16, name="fn_chan_lat", **lat_in)
            scale_l = 1.0 / math.sqrt(n_lat * (d // FN_GROUPS))
            o_l = _pos_dft(*_dft_tables(n_lat), ab_l, batch=batch, n_tok=n_lat, d=d, scale=scale_l, name="fn_pos_lat")
            o_c = None
            if keep_ctx:
                ab_c = _fused_mm(xc, w_cd, out_dtype=BF16, name="fn_chan_ctx", **ctx_in)
                scale_c = 1.0 / math.sqrt(n_ctx * (d // FN_GROUPS))
                o_c = _pos_dft(*_dft_tables(n_ctx), ab_c, batch=batch, n_tok=n_ctx, d=d, scale=scale_c,
                               name="fn_pos_ctx")
            w_o = fn_w_o
        elif kind == 2:
            scale = jnp.concatenate([jnp.full((d,), LOG2E * NA_HEAD_DIM ** -0.5, F32), jnp.ones((2 * d,), F32)])
            qkv_l = _fused_mm(xl, na_w_qkv, w_index=j, colscale=scale, out_dtype=BF16, name="na_qkv_lat", **lat_in)
            qkv_c = _fused_mm(xc, na_w_qkv, w_index=j, colscale=scale, out_dtype=BF16, name="na_qkv_ctx", **ctx_in)
            groups = d // LANES
            bias_tab = _na_bias_tables(na_rpb[j] * LOG2E)
            o_l = _na_attention(qkv_l, qkv_c, bias_tab, batch=batch, n_lat=n_lat, n_ctx=n_ctx)
            o_c = _attention(qkv_c, 0, [(qkv_c, groups, qkv_c, 2 * groups, n_ctx)], batch=batch, n_q=n_ctx,
                             groups=groups, name="na_attn_ctx") if keep_ctx else None
            w_o = na_w_o
        else:
            q_rank = mla_w_dq.shape[-1]
            kv_rank = mla_w_uk.shape[-2]
            hq = MLA_NOPE_DIM + MLA_ROPE_DIM
            down_w = 3 * 384
            assert q_rank == 768 and kv_rank == 256
            w_down = jnp.concatenate([mla_w_dq[j], mla_w_dkv[j], jnp.zeros((d, down_w - q_rank - kv_rank - MLA_ROPE_DIM), F32)],
                                     axis=1)
            rh = MLA_ROPE_DIM // 2
            n_a = LANES // 2 - rh

            def head_lanes(nope, rope_part):
                z = jnp.zeros(nope.shape[:-1] + (LANES - hq,), F32)
                return jnp.concatenate([rope_part[..., :rh], nope[..., :n_a], rope_part[..., rh:], nope[..., n_a:], z],
                                       axis=-1)

            wq = mla_w_uq[j].reshape(q_rank, MLA_HEADS, hq)
            wq = head_lanes(wq[..., :MLA_NOPE_DIM], wq[..., MLA_NOPE_DIM:]).reshape(q_rank, MLA_HEADS * LANES)
            wk_nope = head_lanes(mla_w_uk[j].reshape(kv_rank, MLA_HEADS, MLA_NOPE_DIM),
                                 jnp.zeros((kv_rank, MLA_HEADS, MLA_ROPE_DIM), F32))
            eye = jnp.broadcast_to(jnp.eye(MLA_ROPE_DIM, dtype=F32)[:, None, :], (MLA_ROPE_DIM, MLA_HEADS, MLA_ROPE_DIM))
            wk_rope = head_lanes(jnp.zeros((MLA_ROPE_DIM, MLA_HEADS, MLA_NOPE_DIM), F32), eye)
            kw = MLA_HEADS * LANES
            vw = MLA_HEADS * MLA_V_DIM
            pad_rows = 384 - kv_rank - MLA_ROPE_DIM
            w_kv = jnp.concatenate([
                jnp.concatenate([wk_nope.reshape(kv_rank, kw), mla_w_uv[j]], axis=1),
                jnp.concatenate([wk_rope.reshape(MLA_ROPE_DIM, kw), jnp.zeros((MLA_ROPE_DIM, vw), F32)], axis=1),
                jnp.zeros((pad_rows, kw + vw), F32)], axis=0)
            kv_gain = jnp.concatenate([mla_kv_norm[j], jnp.ones((384 - kv_rank,), F32)])
            rope = _rope_tables(n_lat, MLA_ROPE_DIM, np.where(np.arange(LANES // 2) < rh, np.arange(LANES // 2), -1))
            qscale = jnp.full((kw,), LOG2E * hq ** -0.5, F32)

            def project(xs, inp, n_tok, rope_tabs, tag):
                low = _fused_mm(xs, w_down, tn=384, name="mla_down_" + tag, **inp)
                rp = dict(rope=rope_tabs) if rope_tabs is not None else {}
                tmx = inp.get("tm", 1024)
                q = _fused_mm(low, wq, a_cols=(0, q_rank), gain=mla_q_norm[j], colscale=qscale, rope_cols=kw,
                              rows_per_batch=inp["rows_per_batch"], tm=tmx, out_dtype=BF16, name="mla_q_" + tag, **rp)
                kv = _fused_mm(low, w_kv, a_cols=(q_rank, 384), gain=kv_gain, norm_cols=kv_rank, rope_cols=kw,
                               rows_per_batch=inp["rows_per_batch"], tm=tmx, out_dtype=BF16, name="mla_kv_" + tag, **rp)
                return q, kv

            q_l, kv_l = project(xl, lat_in, n_lat, rope, "lat")
            q_c, kv_c = project(xc, ctx_in, n_ctx, None, "ctx")
            groups = MLA_HEADS // 2
            ctx_kv = (kv_c, 0, kv_c, 2 * groups, n_ctx)
            lat_kv = (kv_l, 0, kv_l, 2 * groups, n_lat)
            o_l = _attention(q_l, 0, [ctx_kv, lat_kv], batch=batch, n_q=n_lat, groups=groups, slice_mode=True,
                             name="mla_attn_lat")
            o_c = _attention(q_c, 0, [ctx_kv], batch=batch, n_q=n_ctx, groups=groups, slice_mode=True,
                             name="mla_attn_ctx") if keep_ctx else None
            w_o = mla_w_o

        xl = _fused_mm(o_l, w_o, w_index=j, res=xl, gate=g1, rows_per_batch=n_lat, name="mix_out_lat")
        if keep_ctx:
            xc = _fused_mm(o_c, w_o, w_index=j, res=xc, gate=jnp.broadcast_to(cg1, (1, 1, d)),
                           rows_per_batch=batch * n_ctx, tm=n_ctx, name="mix_out_ctx")

        h_l, aff_l = _norm_router(xl, norm_ffn[i], sc2, sh2, moe_w_router, i, rows_per_batch=n_lat)
        gate_l, idx_l, xs_l, pos_l = _route(aff_l, h_l, batch, n_lat)
        xs, gates = [xs_l], [gate_l[..., None]]
        if keep_ctx:
            h_c, aff_c = _norm_router(xc, norm_ffn[i], csc2, csh2, moe_w_router, i, rows_per_batch=batch * n_ctx,
                                      tm=n_ctx)
            gate_c, idx_c, xs_c, _ = _route(aff_c, h_c, batch, n_ctx)
            xs.append(xs_c)
            gates.append(gate_c[..., None])
        ys = _moe_ffn(xs, gates, moe_w_gate, moe_w_up, moe_w_down, i)
        xl = _combine_lat(xl, g2, ys[0], idx_l, pos_l, batch, n_lat,
                          final_gain=norm_final if i == depth - 1 else None)
        if keep_ctx:
            xc = _combine(xc, cg2, ys[1], idx_c, batch, n_ctx)

    return xl.reshape(batch, n_lat, d)
```

```python
import functools
import math

import numpy as np
import jax
import jax.numpy as jnp
from jax import lax
from jax.experimental import pallas as pl
from jax.experimental.pallas import tpu as pltpu

F32 = jnp.float32
BF16 = jnp.bfloat16

LANES = 128
VMEM_LIMIT_BYTES = 56 * 1024 * 1024

GRID_W = 64
ROPE_BASE = 10000.0
NORM_EPS = 1e-6
ADA_CHUNKS = 6
DA_HEAD_DIM = 64
NA_HEADS = 16
NA_HEAD_DIM = 64
NA_WIN_ROWS = 8
NA_WIN_COLS = 16
NA_Q_ROWS = 8
NA_K_ROWS = 16
MLA_HEADS = 16
MLA_NOPE_DIM = 64
MLA_ROPE_DIM = 32
MLA_V_DIM = 64
FN_GROUPS = 4
EC_CAPACITY_FACTOR = 2
MASK_VALUE = -1e30
LOG2E = math.log2(math.e)


def _cparams(sem):
    return pltpu.CompilerParams(dimension_semantics=sem, vmem_limit_bytes=VMEM_LIMIT_BYTES)


def _rope_lanes(acc, cos, sin):
    outs = []
    for g in range(acc.shape[1] // LANES):
        blk = acc[:, g * LANES:(g + 1) * LANES]
        outs.append(blk * cos + pltpu.roll(blk, LANES // 2, 1) * sin)
    return outs[0] if len(outs) == 1 else jnp.concatenate(outs, axis=1)


def _mm_kernel(*refs, norm_cols, has_gain, has_mod, has_colscale, rope_tiles, has_bias, has_res,
               silu_in, precision):
    it = iter(refs)
    a_ref = next(it)
    w_ref = next(it)
    gain_ref = next(it) if has_gain else None
    sc_ref, sh_ref = (next(it), next(it)) if has_mod else (None, None)
    cs_ref = next(it) if has_colscale else None
    cos_ref, sin_ref = (next(it), next(it)) if rope_tiles else (None, None)
    bias_ref = next(it) if has_bias else None
    res_ref, gate_ref = (next(it), next(it)) if has_res else (None, None)
    o_ref = next(it)
    h_scr = next(it, None)
    j = pl.program_id(1)
    hi = precision is not None

    def prologue():
        a = a_ref[...].astype(F32)
        if silu_in:
            a = a * (1.0 / (1.0 + jnp.exp(-a)))
        if has_gain:
            k = a.shape[1]
            if norm_cols < k:
                col = lax.broadcasted_iota(jnp.int32, (1, k), 1)
                sq = jnp.where(col < norm_cols, a * a, 0.0)
            else:
                sq = a * a
            ms = jnp.sum(sq, axis=-1, keepdims=True) * (1.0 / norm_cols)
            y = a * lax.rsqrt(ms + NORM_EPS) * gain_ref[...]
            a = jnp.where(col < norm_cols, y, a) if norm_cols < k else y
        if has_mod:
            a = a * (1.0 + sc_ref[...]) + sh_ref[...]
        return a if hi else a.astype(BF16)

    if h_scr is not None:
        @pl.when(j == 0)
        def _():
            h_scr[...] = prologue()
        h = h_scr[...]
    else:
        h = a_ref[...] if hi else a_ref[...].astype(BF16)
    w = w_ref[...] if hi else w_ref[...].astype(BF16)
    acc = jnp.dot(h, w, preferred_element_type=F32, precision=precision)
    if has_colscale:
        acc = acc * cs_ref[...]

    def finish(v):
        if has_bias:
            v = v + bias_ref[...]
        if has_res:
            v = res_ref[...] + gate_ref[...] * v
        o_ref[...] = v.astype(o_ref.dtype)

    if rope_tiles:
        @pl.when(j < rope_tiles)
        def _():
            finish(_rope_lanes(acc, cos_ref[...], sin_ref[...]))

        @pl.when(j >= rope_tiles)
        def _():
            finish(acc)
    else:
        finish(acc)


def _fused_mm(a, w, *, w_index=None, a_cols=None, gain=None, norm_cols=None, mod=None, colscale=None, rope=None,
              rope_cols=0, bias=None, res=None, gate=None, silu_in=False, rows_per_batch=None,
              out_dtype=F32, tm=1024, tn=1024, precision=None, name="mm"):
    m = a.shape[0]
    k0, k = a_cols if a_cols is not None else (0, a.shape[1])
    if precision is None and w.dtype != BF16:
        w = (w[w_index] if w.ndim == 3 else w).astype(BF16)
    n = w.shape[-1]
    assert w.shape[-2] == k and k0 % k == 0
    tm = min(tm, m)
    tn = min(tn, n)
    assert m % tm == 0 and n % tn == 0
    rows_per_batch = rows_per_batch or m
    assert rows_per_batch % tm == 0
    tpb = rows_per_batch // tm
    has_prologue = gain is not None or mod is not None or silu_in
    in_specs = [pl.BlockSpec((tm, k), lambda i, j: (i, k0 // k))]
    args = [a]
    if w.ndim == 3:
        in_specs.append(pl.BlockSpec((None, k, tn), lambda i, j: (w_index, 0, j)))
    else:
        in_specs.append(pl.BlockSpec((k, tn), lambda i, j: (0, j)))
    args.append(w)
    if gain is not None:
        in_specs.append(pl.BlockSpec((1, k), lambda i, j: (0, 0)))
        args.append(gain.reshape(1, k).astype(F32))
    if mod is not None:
        for t in mod:
            in_specs.append(pl.BlockSpec((None, 1, k), lambda i, j: (i // tpb, 0, 0)))
            args.append(t)
    if colscale is not None:
        in_specs.append(pl.BlockSpec((1, tn), lambda i, j: (0, j)))
        args.append(colscale.reshape(1, n).astype(F32))
    rope_tiles = 0
    if rope is not None:
        assert rope_cols % tn == 0
        rope_tiles = rope_cols // tn
        for t in rope:
            in_specs.append(pl.BlockSpec((tm, LANES), lambda i, j: (i % tpb, 0)))
            args.append(t)
    if bias is not None:
        in_specs.append(pl.BlockSpec((1, tn), lambda i, j: (0, j)))
        args.append(bias.reshape(1, n).astype(F32))
    if res is not None:
        in_specs.append(pl.BlockSpec((tm, tn), lambda i, j: (i, j)))
        args.append(res)
        in_specs.append(pl.BlockSpec((None, 1, tn), lambda i, j: (i // tpb, 0, j)))
        args.append(gate)
    scratch = [pltpu.VMEM((tm, k), F32 if precision is not None else BF16)] if has_prologue else []
    kern = functools.partial(
        _mm_kernel, norm_cols=norm_cols or k, has_gain=gain is not None, has_mod=mod is not None,
        has_colscale=colscale is not None, rope_tiles=rope_tiles, has_bias=bias is not None,
        has_res=res is not None, silu_in=silu_in, precision=precision)
    return pl.pallas_call(
        kern, grid=(m // tm, n // tn), in_specs=in_specs,
        out_specs=pl.BlockSpec((tm, tn), lambda i, j: (i, j)),
        out_shape=jax.ShapeDtypeStruct((m, n), out_dtype), scratch_shapes=scratch,
        compiler_params=_cparams(("parallel", "arbitrary")), name=name)(*args)


def _attn_kernel(*refs, n_src, tk, rb, slice_mode, sub_width, diff_mode, lambda_init):
    q_ref = refs[0]
    k_refs, v_refs = refs[1:1 + 2 * n_src:2], refs[2:2 + 2 * n_src:2]
    pos = 1 + 2 * n_src
    if diff_mode:
        lam_ref, subln_ref = refs[pos], refs[pos + 1]
        pos += 2
    o_ref = refs[pos]
    per = [refs[pos + 1 + 9 * s:pos + 10 + 9 * s] for s in range(2)]
    qm_scr = [t[0] for t in per]
    s_scr = [t[1:3] for t in per]
    p_scr = [t[3:5] for t in per]
    m_scr = [t[5] for t in per]
    a_scr = [t[6:8] for t in per]
    acc_scr = [t[8] for t in per]
    tq = q_ref.shape[0]
    q = q_ref[...]
    lane = lax.broadcasted_iota(jnp.int32, (1, LANES), 1)
    for s in range(2):
        if slice_mode:
            qm_scr[s][...] = q[:, s * LANES:(s + 1) * LANES]
        else:
            qm_scr[s][...] = jnp.where((lane // sub_width) % 2 == s, q, jnp.zeros_like(q))
        m_scr[s][...] = jnp.full(m_scr[s].shape, MASK_VALUE, F32)
        acc_scr[s][...] = jnp.zeros(acc_scr[s].shape, F32)

    lengths = [r.shape[0] for r in k_refs]
    n_chunks = sum(lengths) // tk

    def chunk_of(src_refs, c):
        pieces, first, begin = [], c * tk, 0
        for ref, n in zip(src_refs, lengths):
            lo, hi = max(first, begin), min(first + tk, begin + n)
            if lo < hi:
                pieces.append(ref[lo - begin:hi - begin, :])
            begin += n
        return pieces[0] if len(pieces) == 1 else jnp.concatenate(pieces, axis=0)

    def scores(c, slot):
        k = chunk_of(k_refs, c)
        for s in range(2):
            ks = k[:, s * LANES:(s + 1) * LANES] if slice_mode else k
            s_scr[s][slot][...] = lax.dot_general(qm_scr[s][...], ks, (((1,), (1,)), ((), ())),
                                                  preferred_element_type=F32)

    def softmax(slot):
        for s in range(2):
            for r in range(tq // rb):
                rows = slice(r * rb, (r + 1) * rb)
                sb = s_scr[s][slot][rows, :]
                m_old = m_scr[s][rows, :]
                m_new = jnp.maximum(m_old, jnp.max(sb, axis=-1, keepdims=True))
                m_scr[s][rows, :] = m_new
                a_scr[s][slot][rows, :] = jnp.exp2(m_old - m_new)
                p_scr[s][slot][rows, :] = jnp.exp2(sb - jnp.tile(m_new, (1, tk // LANES))).astype(BF16)

    def weighted_sum(c, slot):
        v = chunk_of(v_refs, c)
        vext = jnp.concatenate([v, jnp.ones((tk, LANES), BF16)], axis=1)
        for s in range(2):
            alpha = a_scr[s][slot][...]
            pv = jnp.dot(p_scr[s][slot][...], vext, preferred_element_type=F32)
            acc_scr[s][...] = jnp.concatenate([alpha, alpha], axis=1) * acc_scr[s][...] + pv

    def step(c, slot):
        scores(c, slot)
        softmax(1 - slot)
        weighted_sum(c - 2, slot)

    scores(0, 0)
    if n_chunks > 1:
        scores(1, 1)
    softmax(0)
    for c in range(2, n_chunks):
        step(c, c % 2)
    if n_chunks > 1:
        softmax((n_chunks - 1) % 2)
        weighted_sum(n_chunks - 2, n_chunks % 2)
    weighted_sum(n_chunks - 1, (n_chunks - 1) % 2)
    o0 = acc_scr[0][:, :LANES] / acc_scr[0][:, LANES:]
    o1 = acc_scr[1][:, :LANES] / acc_scr[1][:, LANES:]
    if diff_mode:
        lp = lam_ref[...]
        lam = (jnp.exp(jnp.sum(lp[0:1] * lp[1:2], axis=-1, keepdims=True))
               - jnp.exp(jnp.sum(lp[2:3] * lp[3:4], axis=-1, keepdims=True)) + lambda_init)
        o = o0 - lam * o1
        ms = jnp.mean(o * o, axis=-1, keepdims=True)
        o = o * lax.rsqrt(ms + NORM_EPS) * subln_ref[...] * (1.0 - lambda_init)
    else:
        o = jnp.where(lane < LANES // 2, o0, o1)
    o_ref[...] = o.astype(o_ref.dtype)


def _attention(q_arr, q_blk, srcs, *, batch, n_q, groups, slice_mode=False, diff=None, sub_width=LANES // 2,
               tq=1024, tk=768, rb=16, name="attn"):
    qw = 2 * LANES if slice_mode else LANES
    n_k = sum(s[4] for s in srcs)
    tq = min(tq, n_q)
    tk = min(tk, n_k)
    assert n_q % tq == 0 and n_k % tk == 0 and tk % LANES == 0 and tq % rb == 0
    nqt = n_q // tq
    in_specs = [pl.BlockSpec((tq, qw), lambda b, g, i: (b * nqt + i, q_blk + g))]
    args = [q_arr]
    for k_arr, k_blk, v_arr, v_blk, nk in srcs:
        in_specs.append(pl.BlockSpec((nk, qw), lambda b, g, i, k_blk=k_blk: (b, k_blk + g)))
        in_specs.append(pl.BlockSpec((nk, LANES), lambda b, g, i, v_blk=v_blk: (b, v_blk + g)))
        args += [k_arr, v_arr]
    lambda_init = 0.0
    if diff is not None:
        lam_params, subln, lambda_init = diff
        in_specs.append(pl.BlockSpec((4, DA_HEAD_DIM), lambda b, g, i: (0, 0)))
        in_specs.append(pl.BlockSpec((1, LANES), lambda b, g, i: (0, 0)))
        args += [lam_params, subln.reshape(1, LANES)]
    kern = functools.partial(_attn_kernel, n_src=len(srcs), tk=tk, rb=rb, slice_mode=slice_mode,
                             sub_width=sub_width, diff_mode=diff is not None, lambda_init=lambda_init)
    scratch = 2 * ([pltpu.VMEM((tq, LANES), BF16)] + 2 * [pltpu.VMEM((tq, tk), F32)] + 2 * [pltpu.VMEM((tq, tk), BF16)]
                   + [pltpu.VMEM((tq, LANES), F32)] + 2 * [pltpu.VMEM((tq, LANES), F32)]
                   + [pltpu.VMEM((tq, 2 * LANES), F32)])
    return pl.pallas_call(
        kern, grid=(batch, groups, nqt), in_specs=in_specs,
        out_specs=pl.BlockSpec((tq, LANES), lambda b, g, i: (b * nqt + i, g)),
        out_shape=jax.ShapeDtypeStruct((batch * n_q, groups * LANES), BF16), scratch_shapes=scratch,
        compiler_params=_cparams(("parallel", "parallel", "arbitrary")), name=name)(*args)


def _na_kernel(q_ref, kc_ref, vc_ref, kl_ref, vl_ref, bias_ref, o_ref, s0_scr, s1_scr, p0_scr, p1_scr, *, n_rows, rb):
    i = pl.program_id(2)
    k0 = jnp.clip(i * NA_Q_ROWS - NA_WIN_ROWS // 2, 0, n_rows - NA_K_ROWS)
    start = pl.multiple_of(k0 * GRID_W, GRID_W * 4)
    nkw = NA_K_ROWS * GRID_W
    kw = kl_ref[pl.ds(start, nkw), :]
    kc = kc_ref[...]
    q = q_ref[...]
    tq = q.shape[0]
    n_all = nkw + kc.shape[0]
    lane = lax.broadcasted_iota(jnp.int32, (1, LANES), 1)
    s_scr, p_scr = (s0_scr, s1_scr), (p0_scr, p1_scr)
    dn = (((1,), (1,)), ((), ()))
    key_row = lax.broadcasted_iota(jnp.int32, (1, nkw), 1) // GRID_W
    for s in range(2):
        qs = jnp.where((lane < LANES // 2) if s == 0 else (lane >= LANES // 2), q, jnp.zeros_like(q))
        s_win = lax.dot_general(qs, kw, dn, preferred_element_type=F32)
        for a in range(NA_Q_ROWS):
            qr = i * NA_Q_ROWS + a
            r0 = jnp.clip(qr - NA_WIN_ROWS // 2, 0, n_rows - NA_WIN_ROWS)
            u = k0 - i * NA_Q_ROWS + (NA_WIN_ROWS - 1 - a + NA_BIAS_PAD)
            off = pl.multiple_of((u // 2) * LANES, LANES)
            parity = (NA_WIN_ROWS - 1 - a + NA_BIAS_PAD) % 2
            bias = bias_ref[s, parity, :, pl.ds(off, nkw)]
            seen = jnp.logical_and(key_row >= r0 - k0, key_row < r0 - k0 + NA_WIN_ROWS)
            rows = slice(a * GRID_W, (a + 1) * GRID_W)
            s_scr[s][rows, :nkw] = jnp.where(seen, s_win[rows, :] + bias, MASK_VALUE)
        s_scr[s][:, nkw:] = lax.dot_general(qs, kc, dn, preferred_element_type=F32)
    for s in range(2):
        for r in range(tq // rb):
            rows = slice(r * rb, (r + 1) * rb)
            sb = s_scr[s][rows, :]
            p_scr[s][rows, :] = jnp.exp2(sb - jnp.max(sb, axis=-1, keepdims=True)).astype(BF16)
    v_all = jnp.concatenate([vl_ref[pl.ds(start, nkw), :], vc_ref[...]], axis=0)
    vext = jnp.concatenate([v_all, jnp.ones((n_all, LANES), BF16)], axis=1)
    outs = []
    for s in range(2):
        pv = jnp.dot(p_scr[s][...], vext, preferred_element_type=F32)
        outs.append(pv[:, :LANES] / pv[:, LANES:])
    o_ref[...] = jnp.where(lane < LANES // 2, outs[0], outs[1]).astype(o_ref.dtype)


NA_BIAS_PAD = 8
NA_BIAS_LANES = 2048


def _na_bias_tables(rpb):
    h = rpb.shape[0]
    n_off = 2 * NA_WIN_ROWS - 1
    dc = np.arange(GRID_W)[None, :] - np.arange(GRID_W)[:, None] + NA_WIN_COLS - 1
    onehot = (dc[None] == np.arange(2 * NA_WIN_COLS - 1)[:, None, None]).astype(np.float32)
    toep = jnp.einsum('hrd,dqk->hqrk', rpb.astype(F32), jnp.asarray(onehot), precision=lax.Precision.HIGHEST)
    cols = np.arange(GRID_W)
    col_start = np.clip(cols - NA_WIN_COLS // 2, 0, GRID_W - NA_WIN_COLS)
    col_ok = (cols[None, :] >= col_start[:, None]) & (cols[None, :] < col_start[:, None] + NA_WIN_COLS)
    toep = jnp.where(jnp.asarray(col_ok)[None, :, None, :], toep, MASK_VALUE).reshape(h, GRID_W, n_off * GRID_W)
    copies = []
    for shift in (0, GRID_W):
        left = NA_BIAS_PAD * GRID_W - shift
        copies.append(jnp.pad(toep, ((0, 0), (0, 0), (left, NA_BIAS_LANES - left - n_off * GRID_W))))
    return jnp.stack(copies, axis=1)


def _na_attention(qkv_lat, qkv_ctx, bias_tab, *, batch, n_lat, n_ctx):
    d = NA_HEADS * NA_HEAD_DIM
    groups = d // LANES
    n_rows = n_lat // GRID_W
    nb = n_rows // NA_Q_ROWS
    tq = NA_Q_ROWS * GRID_W
    assert n_rows >= NA_K_ROWS and n_rows % NA_Q_ROWS == 0
    in_specs = [
        pl.BlockSpec((tq, LANES), lambda b, g, i: (b * nb + i, g)),
        pl.BlockSpec((n_ctx, LANES), lambda b, g, i: (b, groups + g)),
        pl.BlockSpec((n_ctx, LANES), lambda b, g, i: (b, 2 * groups + g)),
        pl.BlockSpec((n_lat, LANES), lambda b, g, i: (b, groups + g)),
        pl.BlockSpec((n_lat, LANES), lambda b, g, i: (b, 2 * groups + g)),
        pl.BlockSpec((2, 2, GRID_W, NA_BIAS_LANES), lambda b, g, i: (g, 0, 0, 0)),
    ]
    n_all = NA_K_ROWS * GRID_W + n_ctx
    return pl.pallas_call(
        functools.partial(_na_kernel, n_rows=n_rows, rb=16), grid=(batch, groups, nb), in_specs=in_specs,
        out_specs=pl.BlockSpec((tq, LANES), lambda b, g, i: (b * nb + i, g)),
        out_shape=jax.ShapeDtypeStruct((batch * n_lat, d), BF16),
        scratch_shapes=2 * [pltpu.VMEM((tq, n_all), F32)] + 2 * [pltpu.VMEM((tq, n_all), BF16)],
        compiler_params=_cparams(("parallel", "parallel", "arbitrary")), name="na_attn",
    )(qkv_lat, qkv_ctx, qkv_ctx, qkv_lat, qkv_lat, bias_tab)


def _dft_kernel(c_ref, ms_ref, ab_ref, mid_ref, o_ref, acc_ref, *, scale):
    n = pl.program_id(2)
    d = o_ref.shape[1]
    part = (jnp.dot(c_ref[...], ab_ref[:, :d], preferred_element_type=F32)
            + jnp.dot(ms_ref[...], ab_ref[:, d:], preferred_element_type=F32))

    @pl.when(n == 0)
    def _():
        acc_ref[...] = part

    @pl.when(n > 0)
    def _():
        acc_ref[...] += part

    @pl.when(n == pl.num_programs(2) - 1)
    def _():
        tk = o_ref.shape[0]
        k = pl.program_id(1) * tk + lax.broadcasted_iota(jnp.int32, (tk, 1), 0)
        sign = jnp.where(k % 2 == 0, 1.0, -1.0)
        o_ref[...] = ((acc_ref[...] + sign * mid_ref[...].astype(F32)) * scale).astype(o_ref.dtype)


def _fold_kernel(x_ref, src_ref, edge_ref, perm_ref, sign_ref, o_ref):
    i = pl.program_id(1)
    mirror = jnp.dot(perm_ref[...], src_ref[...], preferred_element_type=F32)
    first = lax.broadcasted_iota(jnp.int32, (mirror.shape[0], 1), 0) == 0
    edge = jnp.where(i > 0, edge_ref[0:1, :].astype(F32), 0.0)
    mirror = jnp.where(first, edge, mirror)
    o_ref[...] = (x_ref[...].astype(F32) + sign_ref[...] * mirror).astype(o_ref.dtype)


def _dft_fold(ab, *, batch, n_tok, d, name):
    half = n_tok // 2
    t = min(512, half)
    nt = n_tok // t
    perm = np.zeros((t, t), np.float32)
    perm[np.arange(1, t), t - np.arange(1, t)] = 1.0
    sign = jnp.concatenate([jnp.ones((d,), F32), -jnp.ones((d,), F32)]).reshape(1, 2 * d)
    edge_blocks = t // 8
    return pl.pallas_call(
        _fold_kernel, grid=(batch, nt // 2),
        in_specs=[pl.BlockSpec((t, 2 * d), lambda b, i: (b * nt + i, 0)),
                  pl.BlockSpec((t, 2 * d), lambda b, i: (b * nt + nt - 1 - i, 0)),
                  pl.BlockSpec((8, 2 * d), lambda b, i: ((b * nt + jnp.maximum(nt - i, 1) % nt) * edge_blocks, 0)),
                  pl.BlockSpec((t, t), lambda b, i: (0, 0)),
                  pl.BlockSpec((1, 2 * d), lambda b, i: (0, 0))],
        out_specs=pl.BlockSpec((t, 2 * d), lambda b, i: (b * (nt // 2) + i, 0)),
        out_shape=jax.ShapeDtypeStruct((batch * half, 2 * d), BF16),
        compiler_params=_cparams(("parallel", "parallel")), name=name)(ab, ab, ab, jnp.asarray(perm, BF16), sign)


def _pos_dft(cos_t, msin_t, ab, *, batch, n_tok, d, scale, name):
    half = n_tok // 2
    folded = _dft_fold(ab, batch=batch, n_tok=n_tok, d=d, name=name + "_fold")
    mid = ab.reshape(batch, n_tok, 2 * d)[:, half:half + 1, :d]
    tk = min(1024, n_tok)
    tn = min(1024, half)
    nk, nn = n_tok // tk, half // tn
    return pl.pallas_call(
        functools.partial(_dft_kernel, scale=scale), grid=(batch, nk, nn),
        in_specs=[pl.BlockSpec((tk, tn), lambda b, k, n: (k, n)),
                  pl.BlockSpec((tk, tn), lambda b, k, n: (k, n)),
                  pl.BlockSpec((tn, 2 * d), lambda b, k, n: (b * nn + n, 0)),
                  pl.BlockSpec((None, 1, d), lambda b, k, n: (b, 0, 0))],
        out_specs=pl.BlockSpec((tk, d), lambda b, k, n: (b * nk + k, 0)),
        out_shape=jax.ShapeDtypeStruct((batch * n_tok, d), BF16),
        scratch_shapes=[pltpu.VMEM((tk, d), F32)],
        compiler_params=_cparams(("parallel", "parallel", "arbitrary")), name=name)(cos_t, msin_t, folded, mid)


def _dft_tables(n):
    split = 64 if n % 64 == 0 and n > 64 else 1
    m = jnp.arange(n // 2, dtype=jnp.int32)[None, :]

    def phase(kvals):
        ang = ((kvals[:, None] * m) % n).astype(F32) * (2.0 * math.pi / n)
        return jnp.cos(ang), jnp.sin(ang)

    c1, s1 = phase(jnp.arange(n // split, dtype=jnp.int32) * split)
    c2, s2 = phase(jnp.arange(split, dtype=jnp.int32))
    c = c1[:, None, :] * c2[None, :, :] - s1[:, None, :] * s2[None, :, :]
    s = s1[:, None, :] * c2[None, :, :] + c1[:, None, :] * s2[None, :, :]
    return c.reshape(n, n // 2).astype(BF16), (-s).reshape(n, n // 2).astype(BF16)


def _channel_dft_weight(d):
    cg = d // FN_GROUPS
    idx = np.arange(cg)
    ang = 2.0 * np.pi * ((idx[:, None] * idx[None, :]) % cg) / cg
    eye = np.eye(FN_GROUPS)
    return jnp.asarray(np.concatenate([np.kron(eye, np.cos(ang)), np.kron(eye, np.sin(ang))], axis=1), F32)


def _router_kernel(x_ref, gain_ref, sc_ref, sh_ref, wr_ref, h_ref, aff_ref):
    x = x_ref[...]
    ms = jnp.mean(x * x, axis=-1, keepdims=True)
    h = x * lax.rsqrt(ms + NORM_EPS) * gain_ref[...]
    h = h * (1.0 + sc_ref[...]) + sh_ref[...]
    h_ref[...] = h.astype(h_ref.dtype)
    w = wr_ref[...]
    h_hi, w_hi = h.astype(BF16), w.astype(BF16)
    h_lo, w_lo = (h - h_hi.astype(F32)).astype(BF16), (w - w_hi.astype(F32)).astype(BF16)
    logits = (jnp.dot(h_hi, w_hi, preferred_element_type=F32) + jnp.dot(h_lo, w_hi, preferred_element_type=F32)
              + jnp.dot(h_hi, w_lo, preferred_element_type=F32))
    e = jnp.exp(logits - jnp.max(logits, axis=-1, keepdims=True))
    aff_ref[...] = e / jnp.sum(e, axis=-1, keepdims=True)


def _norm_router(x, gain, sc, sh, w_router, layer, *, rows_per_batch, tm=512):
    m, d = x.shape
    e = w_router.shape[-1]
    tm = min(tm, rows_per_batch)
    tpb = rows_per_batch // tm
    return pl.pallas_call(
        _router_kernel, grid=(m // tm,),
        in_specs=[pl.BlockSpec((tm, d), lambda i: (i, 0)),
                  pl.BlockSpec((1, d), lambda i: (0, 0)),
                  pl.BlockSpec((None, 1, d), lambda i: (i // tpb, 0, 0)),
                  pl.BlockSpec((None, 1, d), lambda i: (i // tpb, 0, 0)),
                  pl.BlockSpec((None, d, e), lambda i: (layer, 0, 0))],
        out_specs=[pl.BlockSpec((tm, d), lambda i: (i, 0)), pl.BlockSpec((tm, e), lambda i: (i, 0))],
        out_shape=[jax.ShapeDtypeStruct((m, d), BF16), jax.ShapeDtypeStruct((m, e), F32)],
        compiler_params=_cparams(("parallel",)), name="norm_router")(x, gain.reshape(1, d), sc, sh, w_router)


FFN_PIECE = 256


def _ffn_kernel(*refs, n_streams):
    x_refs = refs[:n_streams]
    wg_ref, wu_ref, wd_ref = refs[n_streams:n_streams + 3]
    gate_refs = refs[n_streams + 3:2 * n_streams + 3]
    o_refs = refs[2 * n_streams + 3:3 * n_streams + 3]
    acc_ref = refs[3 * n_streams + 3]
    f = pl.program_id(2)
    x = x_refs[0][...] if n_streams == 1 else jnp.concatenate([r[...] for r in x_refs], axis=0)

    @pl.when(f == 0)
    def _():
        acc_ref[...] = jnp.zeros(acc_ref.shape, F32)

    tf = wg_ref.shape[1]
    acts = []
    for c0 in range(0, tf, FFN_PIECE):
        a = jnp.dot(x, wg_ref[:, c0:c0 + FFN_PIECE].astype(BF16), preferred_element_type=F32)
        u = jnp.dot(x, wu_ref[:, c0:c0 + FFN_PIECE].astype(BF16), preferred_element_type=F32)
        acts.append(((a * (1.0 / (1.0 + jnp.exp(-a)))) * u).astype(BF16))
    d = wd_ref.shape[1]
    for n0 in range(0, d, FFN_PIECE):
        y = None
        for i, act in enumerate(acts):
            part = jnp.dot(act, wd_ref[i * FFN_PIECE:(i + 1) * FFN_PIECE, n0:n0 + FFN_PIECE].astype(BF16),
                           preferred_element_type=F32)
            y = part if y is None else y + part
        acc_ref[:, n0:n0 + FFN_PIECE] += y

    @pl.when(f == pl.num_programs(2) - 1)
    def _():
        row = 0
        for o_ref, g_ref in zip(o_refs, gate_refs):
            c = o_ref.shape[0]
            y = acc_ref[row:row + c, :] * g_ref[...]
            if o_ref.dtype == BF16:
                hi = y.astype(BF16)
                o_ref[:, :d] = hi
                o_ref[:, d:] = (y - hi.astype(F32)).astype(BF16)
            else:
                o_ref[...] = y
            row += c


def _moe_ffn(xs, gates, w_gate, w_up, w_down, layer, *, tf=1024):
    b, e, _, d = xs[0].shape
    f = w_gate.shape[-1]
    tf = min(tf, f)
    slot = lambda c, w: pl.BlockSpec((None, None, c, w), lambda ei, bi, fi: (bi, ei, 0, 0))
    caps = [t.shape[2] for t in xs]
    in_specs = ([slot(c, d) for c in caps]
                + [pl.BlockSpec((None, None, d, tf), lambda ei, bi, fi: (layer, ei, 0, fi)),
                   pl.BlockSpec((None, None, d, tf), lambda ei, bi, fi: (layer, ei, 0, fi)),
                   pl.BlockSpec((None, None, tf, d), lambda ei, bi, fi: (layer, ei, fi, 0))]
                + [slot(c, 1) for c in caps])
    return pl.pallas_call(
        functools.partial(_ffn_kernel, n_streams=len(xs)), grid=(e, b, f // tf), in_specs=in_specs,
        out_specs=[slot(caps[0], 2 * d)] + [slot(c, d) for c in caps[1:]],
        out_shape=([jax.ShapeDtypeStruct((b, e, caps[0], 2 * d), BF16)]
                   + [jax.ShapeDtypeStruct((b, e, c, d), F32) for c in caps[1:]]),
        scratch_shapes=[pltpu.VMEM((sum(caps), d), F32)],
        compiler_params=_cparams(("parallel", "parallel", "arbitrary")), name="moe_ffn",
    )(*xs, w_gate, w_up, w_down, *gates)


def _count_upto(flags):
    lead, n = flags.shape[:-1], flags.shape[-1]
    blocks = flags.reshape(*lead, n // LANES, LANES).astype(BF16)
    tri = jnp.asarray(np.triu(np.ones((LANES, LANES), np.float32))).astype(BF16)
    inside = jnp.einsum('...i,ij->...j', blocks, tri, preferred_element_type=F32)
    totals = inside[..., -1]
    before = jnp.cumsum(totals, axis=-1) - totals
    return (inside + before[..., None]).reshape(*lead, n).astype(jnp.int32)


def _route(aff, h, batch, n_tok):
    e = aff.shape[-1]
    cap = EC_CAPACITY_FACTOR * n_tok // e
    aff_t = jnp.swapaxes(aff.reshape(batch, n_tok, e), 1, 2)
    gate, idx = lax.top_k(aff_t, cap)
    thr = gate[..., -1:]
    above, tied = aff_t > thr, aff_t == thr
    n_tied = cap - jnp.sum(above, axis=-1, keepdims=True, dtype=jnp.int32)
    chosen = above | (tied & (_count_upto(tied) <= n_tied))
    pos = jnp.where(chosen, _count_upto(chosen) - 1, -1)
    pos = jnp.swapaxes(pos, 1, 2).reshape(batch * n_tok, e)
    idx, gate = lax.sort((idx, gate), dimension=2, num_keys=1)
    xs = jax.vmap(lambda hb, ib: hb[ib])(h.reshape(batch, n_tok, -1), idx)
    return gate, idx, xs, pos


COMBINE_TOKENS = 256
COMBINE_WINDOW = 64
SLOT_ALIGN = 16


def _combine_kernel(lo_ref, nr_ref, x_ref, g_ref, pos_ref, fn_ref, y_hbm, o_ref, stage, sem, acc, *, n_tiles, n_total,
                    n_exp, cap, final_norm):
    w = COMBINE_WINDOW
    per = LANES // w
    tile = pl.program_id(0) * n_tiles + pl.program_id(1)
    slot = tile % 2
    lane = lax.broadcasted_iota(jnp.int32, (1, LANES), 1)

    def window(tl, e, r):
        first = lo_ref[tl * n_exp + e] + r * w
        return first, pl.multiple_of(jnp.minimum(first, cap - w), SLOT_ALIGN)

    def copy(tl, e, start, sl):
        return pltpu.make_async_copy(y_hbm.at[tl // n_tiles, e, pl.ds(start, w), :],
                                     stage.at[sl, pl.ds(e * w, w), :], sem.at[sl, e])

    def fetch(tl, r, sl):
        for e in range(n_exp):
            copy(tl, e, window(tl, e, r)[1], sl).start()

    def add_round(r):
        pos = pos_ref[...]
        blocks = []
        for e0 in range(0, n_exp, per):
            first, start = window(tile, e0, r)
            col = pos[:, e0:e0 + 1]
            for k in range(1, per):
                f_k, s_k = window(tile, e0 + k, r)
                here = lane >= k * w
                first, start = jnp.where(here, f_k, first), jnp.where(here, s_k, start)
                col = jnp.where(here, pos[:, e0 + k:e0 + k + 1], col)
            hit = jnp.logical_and(col - start == lane % w, col >= first)
            blocks.append(jnp.where(hit, 1.0, 0.0).astype(BF16))
        onehot = jnp.concatenate(blocks, axis=1)
        for e in range(n_exp):
            copy(tile, e, 0, slot).wait()
        d = acc.shape[1]
        acc[...] += (jnp.dot(onehot, stage[slot, :, :d], preferred_element_type=F32)
                     + jnp.dot(onehot, stage[slot, :, d:], preferred_element_type=F32))

    @pl.when(tile == 0)
    def _():
        fetch(tile, 0, slot)

    @pl.when(tile + 1 < n_total)
    def _():
        fetch(tile + 1, 0, 1 - slot)

    acc[...] = jnp.zeros(acc.shape, F32)
    add_round(0)

    @pl.loop(1, nr_ref[tile])
    def _(r):
        fetch(tile, r, slot)
        add_round(r)

    out = x_ref[...] + g_ref[...] * acc[...]
    if final_norm:
        out = out * lax.rsqrt(jnp.mean(out * out, axis=-1, keepdims=True) + NORM_EPS) * fn_ref[...]
    o_ref[...] = out


def _combine_lat(x, g2, y, idx, pos, batch, n_tok, final_gain=None):
    d = x.shape[-1]
    _, n_exp, cap = idx.shape
    tt, w = COMBINE_TOKENS, COMBINE_WINDOW
    assert n_tok % tt == 0 and cap % SLOT_ALIGN == 0 and w % SLOT_ALIGN == 0 and cap >= w
    nt = n_tok // tt
    bounds = jnp.arange(nt + 1, dtype=jnp.int32) * tt
    below = jnp.sum((idx[..., None] < bounds).astype(jnp.int32), axis=2)
    lo = (below[..., :-1] // SLOT_ALIGN) * SLOT_ALIGN
    rounds = jnp.maximum(jnp.max((below[..., 1:] - lo + w - 1) // w, axis=1), 1)
    lo_flat = jnp.transpose(lo, (0, 2, 1)).reshape(-1)
    grid_spec = pltpu.PrefetchScalarGridSpec(
        num_scalar_prefetch=2, grid=(batch, nt),
        in_specs=[pl.BlockSpec((tt, d), lambda b, t, lo_r, nr_r: (b * nt + t, 0)),
                  pl.BlockSpec((None, 1, d), lambda b, t, lo_r, nr_r: (b, 0, 0)),
                  pl.BlockSpec((tt, n_exp), lambda b, t, lo_r, nr_r: (b * nt + t, 0)),
                  pl.BlockSpec((1, d), lambda b, t, lo_r, nr_r: (0, 0)),
                  pl.BlockSpec(memory_space=pl.ANY)],
        out_specs=pl.BlockSpec((tt, d), lambda b, t, lo_r, nr_r: (b * nt + t, 0)),
        scratch_shapes=[pltpu.VMEM((2, n_exp * w, 2 * d), BF16), pltpu.SemaphoreType.DMA((2, n_exp)),
                        pltpu.VMEM((tt, d), F32)])
    return pl.pallas_call(
        functools.partial(_combine_kernel, n_tiles=nt, n_total=batch * nt, n_exp=n_exp, cap=cap,
                          final_norm=final_gain is not None),
        grid_spec=grid_spec, out_shape=jax.ShapeDtypeStruct(x.shape, F32),
        compiler_params=_cparams(("arbitrary", "arbitrary")), name="moe_combine",
    )(lo_flat, rounds.reshape(-1), x, g2, pos, (g2[0] if final_gain is None else final_gain.reshape(1, d)), y)


def _combine(x, g2, y, idx, batch, n_tok):
    d = x.shape[-1]
    upd = jax.vmap(lambda ib, yb: jnp.zeros((n_tok, d), F32).at[ib.reshape(-1)].add(yb.reshape(-1, d)))(idx, y)
    return (x.reshape(batch, n_tok, d) + g2 * upd).reshape(batch * n_tok, d)


def _rope_tables(n_tok, rot_dim, pair_of_lane):
    t = jnp.arange(n_tok)
    rows = (t // GRID_W).astype(F32)
    cols = (t % GRID_W).astype(F32)
    n_freq = rot_dim // 4
    inv_freq = ROPE_BASE ** (-jnp.arange(n_freq, dtype=F32) / n_freq)
    ang = jnp.concatenate([rows[:, None] * inv_freq, cols[:, None] * inv_freq], axis=-1)
    cos, sin = jnp.cos(ang), jnp.sin(ang)
    pair = np.concatenate([pair_of_lane, pair_of_lane])
    used = jnp.asarray(pair >= 0)[None]
    sign = jnp.asarray(np.where(np.arange(LANES) < LANES // 2, -1.0, 1.0).astype(np.float32))[None]
    cos_l = jnp.where(used, cos[:, np.maximum(pair, 0)], 1.0)
    sin_l = jnp.where(used, sin[:, np.maximum(pair, 0)] * sign, 0.0)
    return cos_l, sin_l


def kernel(x, c, ctx, c_ctx, ada_w, ada_b, norm_mix, norm_ffn, norm_final, da_w_qkv, da_w_o, da_lambda_q1,
           da_lambda_k1, da_lambda_q2, da_lambda_k2, da_subln, fn_w_o, na_w_qkv, na_w_o, na_rpb, mla_w_dq,
           mla_q_norm, mla_w_uq, mla_w_dkv, mla_kv_norm, mla_w_uk, mla_w_uv, mla_w_o, moe_w_router, moe_w_gate,
           moe_w_up, moe_w_down):
    batch, n_lat, d = x.shape
    n_ctx = ctx.shape[1]
    depth = ada_w.shape[0]
    n_mixers = 4
    xl = x.reshape(batch * n_lat, d)
    xc = ctx.reshape(batch * n_ctx, d)

    cond = jnp.concatenate([c, c_ctx[None], jnp.zeros((8 - batch - 1, d), F32)], axis=0)
    mods = [
        _fused_mm(cond, ada_w, w_index=i, bias=ada_b[i], silu_in=True, precision=lax.Precision.HIGHEST, tn=1024,
                  name="ada_mod")
        for i in range(depth)
    ]

    def chunks(i):
        lat = [mods[i][:batch, k * d:(k + 1) * d].reshape(batch, 1, d) for k in range(ADA_CHUNKS)]
        cx = [mods[i][batch:batch + 1, k * d:(k + 1) * d].reshape(1, 1, d) for k in range(ADA_CHUNKS)]
        return lat, cx

    for i in range(depth):
        kind, j = i % n_mixers, i // n_mixers
        keep_ctx = i < depth - 1
        (sh1, sc1, g1, sh2, sc2, g2), (csh1, csc1, cg1, csh2, csc2, cg2) = chunks(i)
        lat_in = dict(gain=norm_mix[i], mod=(sc1, sh1), rows_per_batch=n_lat)
        ctx_in = dict(gain=norm_mix[i], mod=(csc1, csh1), rows_per_batch=batch * n_ctx, tm=n_ctx)

        if kind == 0:
            lambda_init = 0.8 - 0.6 * math.exp(-0.3 * i)
            scale = jnp.concatenate([jnp.full((d,), LOG2E * DA_HEAD_DIM ** -0.5, F32), jnp.ones((2 * d,), F32)])
            groups = d // LANES
            half = DA_HEAD_DIM // 2
            w = da_w_qkv[j]

            def reorder(cols):
                return cols.reshape(d, groups, 2, 2, half).transpose(0, 1, 3, 2, 4).reshape(d, d)

            w = jnp.concatenate([reorder(w[:, :d]), reorder(w[:, d:2 * d]), w[:, 2 * d:]], axis=1)
            rope = _rope_tables(n_lat, DA_HEAD_DIM, np.arange(LANES // 2) % half)
            qkv_l = _fused_mm(xl, w, colscale=scale, rope=rope, rope_cols=2 * d, out_dtype=BF16, name="da_qkv_lat",
                              **lat_in)
            qkv_c = _fused_mm(xc, w, colscale=scale, out_dtype=BF16, name="da_qkv_ctx", **ctx_in)
            lam_params = jnp.stack([da_lambda_q1[j], da_lambda_k1[j], da_lambda_q2[j], da_lambda_k2[j]])
            diff = (lam_params, da_subln[j], lambda_init)
            ctx_kv = (qkv_c, groups, qkv_c, 2 * groups, n_ctx)
            lat_kv = (qkv_l, groups, qkv_l, 2 * groups, n_lat)
            o_l = _attention(qkv_l, 0, [ctx_kv, lat_kv], batch=batch, n_q=n_lat, groups=groups, sub_width=half,
                             diff=diff, name="da_attn_lat")
            o_c = _attention(qkv_c, 0, [ctx_kv], batch=batch, n_q=n_ctx, groups=groups, sub_width=half, diff=diff,
                             name="da_attn_ctx") if keep_ctx else None
            w_o = da_w_o
        elif kind == 1:
            w_cd = _channel_dft_weight(d)
            ab_l = _fused_mm(xl, w_cd, out_dtype=BF16, name="fn_chan_lat", **lat_in)
            scale_l = 1.0 / math.sqrt(n_lat * (d // FN_GROUPS))
            o_l = _pos_dft(*_dft_tables(n_lat), ab_l, batch=batch, n_tok=n_lat, d=d, scale=scale_l, name="fn_pos_lat")
            o_c = None
            if keep_ctx:
                ab_c = _fused_mm(xc, w_cd, out_dtype=BF16, name="fn_chan_ctx", **ctx_in)
                scale_c = 1.0 / math.sqrt(n_ctx * (d // FN_GROUPS))
                o_c = _pos_dft(*_dft_tables(n_ctx), ab_c, batch=batch, n_tok=n_ctx, d=d, scale=scale_c,
                               name="fn_pos_ctx")
            w_o = fn_w_o
        elif kind == 2:
            scale = jnp.concatenate([jnp.full((d,), LOG2E * NA_HEAD_DIM ** -0.5, F32), jnp.ones((2 * d,), F32)])
            qkv_l = _fused_mm(xl, na_w_qkv, w_index=j, colscale=scale, out_dtype=BF16, name="na_qkv_lat", **lat_in)
            qkv_c = _fused_mm(xc, na_w_qkv, w_index=j, colscale=scale, out_dtype=BF16, name="na_qkv_ctx", **ctx_in)
            groups = d // LANES
            bias_tab = _na_bias_tables(na_rpb[j] * LOG2E)
            o_l = _na_attention(qkv_l, qkv_c, bias_tab, batch=batch, n_lat=n_lat, n_ctx=n_ctx)
            o_c = _attention(qkv_c, 0, [(qkv_c, groups, qkv_c, 2 * groups, n_ctx)], batch=batch, n_q=n_ctx,
                             groups=groups, name="na_attn_ctx") if keep_ctx else None
            w_o = na_w_o
        else:
            q_rank = mla_w_dq.shape[-1]
            kv_rank = mla_w_uk.shape[-2]
            hq = MLA_NOPE_DIM + MLA_ROPE_DIM
            down_w = 3 * 384
            assert q_rank == 768 and kv_rank == 256
            w_down = jnp.concatenate([mla_w_dq[j], mla_w_dkv[j], jnp.zeros((d, down_w - q_rank - kv_rank - MLA_ROPE_DIM), F32)],
                                     axis=1)
            rh = MLA_ROPE_DIM // 2
            n_a = LANES // 2 - rh

            def head_lanes(nope, rope_part):
                z = jnp.zeros(nope.shape[:-1] + (LANES - hq,), F32)
                return jnp.concatenate([rope_part[..., :rh], nope[..., :n_a], rope_part[..., rh:], nope[..., n_a:], z],
                                       axis=-1)

            wq = mla_w_uq[j].reshape(q_rank, MLA_HEADS, hq)
            wq = head_lanes(wq[..., :MLA_NOPE_DIM], wq[..., MLA_NOPE_DIM:]).reshape(q_rank, MLA_HEADS * LANES)
            wk_nope = head_lanes(mla_w_uk[j].reshape(kv_rank, MLA_HEADS, MLA_NOPE_DIM),
                                 jnp.zeros((kv_rank, MLA_HEADS, MLA_ROPE_DIM), F32))
            eye = jnp.broadcast_to(jnp.eye(MLA_ROPE_DIM, dtype=F32)[:, None, :], (MLA_ROPE_DIM, MLA_HEADS, MLA_ROPE_DIM))
            wk_rope = head_lanes(jnp.zeros((MLA_ROPE_DIM, MLA_HEADS, MLA_NOPE_DIM), F32), eye)
            kw = MLA_HEADS * LANES
            vw = MLA_HEADS * MLA_V_DIM
            pad_rows = 384 - kv_rank - MLA_ROPE_DIM
            w_kv = jnp.concatenate([
                jnp.concatenate([wk_nope.reshape(kv_rank, kw), mla_w_uv[j]], axis=1),
                jnp.concatenate([wk_rope.reshape(MLA_ROPE_DIM, kw), jnp.zeros((MLA_ROPE_DIM, vw), F32)], axis=1),
                jnp.zeros((pad_rows, kw + vw), F32)], axis=0)
            kv_gain = jnp.concatenate([mla_kv_norm[j], jnp.ones((384 - kv_rank,), F32)])
            rope = _rope_tables(n_lat, MLA_ROPE_DIM, np.where(np.arange(LANES // 2) < rh, np.arange(LANES // 2), -1))
            qscale = jnp.full((kw,), LOG2E * hq ** -0.5, F32)

            def project(xs, inp, n_tok, rope_tabs, tag):
                low = _fused_mm(xs, w_down, tn=384, name="mla_down_" + tag, **inp)
                rp = dict(rope=rope_tabs) if rope_tabs is not None else {}
                tmx = inp.get("tm", 1024)
                q = _fused_mm(low, wq, a_cols=(0, q_rank), gain=mla_q_norm[j], colscale=qscale, rope_cols=kw,
                              rows_per_batch=inp["rows_per_batch"], tm=tmx, out_dtype=BF16, name="mla_q_" + tag, **rp)
                kv = _fused_mm(low, w_kv, a_cols=(q_rank, 384), gain=kv_gain, norm_cols=kv_rank, rope_cols=kw,
                               rows_per_batch=inp["rows_per_batch"], tm=tmx, out_dtype=BF16, name="mla_kv_" + tag, **rp)
                return q, kv

            q_l, kv_l = project(xl, lat_in, n_lat, rope, "lat")
            q_c, kv_c = project(xc, ctx_in, n_ctx, None, "ctx")
            groups = MLA_HEADS // 2
            ctx_kv = (kv_c, 0, kv_c, 2 * groups, n_ctx)
            lat_kv = (kv_l, 0, kv_l, 2 * groups, n_lat)
            o_l = _attention(q_l, 0, [ctx_kv, lat_kv], batch=batch, n_q=n_lat, groups=groups, slice_mode=True,
                             name="mla_attn_lat")
            o_c = _attention(q_c, 0, [ctx_kv], batch=batch, n_q=n_ctx, groups=groups, slice_mode=True,
                             name="mla_attn_ctx") if keep_ctx else None
            w_o = mla_w_o

        xl = _fused_mm(o_l, w_o, w_index=j, res=xl, gate=g1, rows_per_batch=n_lat, name="mix_out_lat")
        if keep_ctx:
            xc = _fused_mm(o_c, w_o, w_index=j, res=xc, gate=jnp.broadcast_to(cg1, (1, 1, d)),
                           rows_per_batch=batch * n_ctx, tm=n_ctx, name="mix_out_ctx")

        h_l, aff_l = _norm_router(xl, norm_ffn[i], sc2, sh2, moe_w_router, i, rows_per_batch=n_lat)
        gate_l, idx_l, xs_l, pos_l = _route(aff_l, h_l, batch, n_lat)
        xs, gates = [xs_l], [gate_l[..., None]]
        if keep_ctx:
            h_c, aff_c = _norm_router(xc, norm_ffn[i], csc2, csh2, moe_w_router, i, rows_per_batch=batch * n_ctx,
                                      tm=n_ctx)
            gate_c, idx_c, xs_c, _ = _route(aff_c, h_c, batch, n_ctx)
            xs.append(xs_c)
            gates.append(gate_c[..., None])
        ys = _moe_ffn(xs, gates, moe_w_gate, moe_w_up, moe_w_down, i)
        xl = _combine_lat(xl, g2, ys[0], idx_l, pos_l, batch, n_lat,
                          final_gain=norm_final if i == depth - 1 else None)
        if keep_ctx:
            xc = _combine(xc, cg2, ys[1], idx_c, batch, n_ctx)

    return xl.reshape(batch, n_lat, d)
```

```python
import functools
import math

import numpy as np
import jax
import jax.numpy as jnp
from jax import lax
from jax.experimental import pallas as pl
from jax.experimental.pallas import tpu as pltpu

F32 = jnp.float32
BF16 = jnp.bfloat16

LANES = 128
VMEM_LIMIT_BYTES = 56 * 1024 * 1024

GRID_W = 64
ROPE_BASE = 10000.0
NORM_EPS = 1e-6
ADA_CHUNKS = 6
DA_HEAD_DIM = 64
NA_HEADS = 16
NA_HEAD_DIM = 64
NA_WIN_ROWS = 8
NA_WIN_COLS = 16
NA_Q_ROWS = 8
NA_K_ROWS = 16
MLA_HEADS = 16
MLA_NOPE_DIM = 64
MLA_ROPE_DIM = 32
MLA_V_DIM = 64
FN_GROUPS = 4
EC_CAPACITY_FACTOR = 2
MASK_VALUE = -1e30
LOG2E = math.log2(math.e)


def _cparams(sem):
    return pltpu.CompilerParams(dimension_semantics=sem, vmem_limit_bytes=VMEM_LIMIT_BYTES)


def _rope_lanes(acc, cos, sin):
    outs = []
    for g in range(acc.shape[1] // LANES):
        blk = acc[:, g * LANES:(g + 1) * LANES]
        outs.append(blk * cos + pltpu.roll(blk, LANES // 2, 1) * sin)
    return outs[0] if len(outs) == 1 else jnp.concatenate(outs, axis=1)


def _mm_kernel(*refs, norm_cols, has_gain, has_mod, has_colscale, rope_tiles, has_bias, has_res,
               silu_in, precision):
    it = iter(refs)
    a_ref = next(it)
    w_ref = next(it)
    gain_ref = next(it) if has_gain else None
    sc_ref, sh_ref = (next(it), next(it)) if has_mod else (None, None)
    cs_ref = next(it) if has_colscale else None
    cos_ref, sin_ref = (next(it), next(it)) if rope_tiles else (None, None)
    bias_ref = next(it) if has_bias else None
    res_ref, gate_ref = (next(it), next(it)) if has_res else (None, None)
    o_ref = next(it)
    h_scr = next(it, None)
    j = pl.program_id(1)
    hi = precision is not None

    def prologue():
        a = a_ref[...].astype(F32)
        if silu_in:
            a = a * (1.0 / (1.0 + jnp.exp(-a)))
        if has_gain:
            k = a.shape[1]
            if norm_cols < k:
                col = lax.broadcasted_iota(jnp.int32, (1, k), 1)
                sq = jnp.where(col < norm_cols, a * a, 0.0)
            else:
                sq = a * a
            ms = jnp.sum(sq, axis=-1, keepdims=True) * (1.0 / norm_cols)
            y = a * lax.rsqrt(ms + NORM_EPS) * gain_ref[...]
            a = jnp.where(col < norm_cols, y, a) if norm_cols < k else y
        if has_mod:
            a = a * (1.0 + sc_ref[...]) + sh_ref[...]
        return a if hi else a.astype(BF16)

    if h_scr is not None:
        @pl.when(j == 0)
        def _():
            h_scr[...] = prologue()
        h = h_scr[...]
    else:
        h = a_ref[...] if hi else a_ref[...].astype(BF16)
    w = w_ref[...] if hi else w_ref[...].astype(BF16)
    acc = jnp.dot(h, w, preferred_element_type=F32, precision=precision)
    if has_colscale:
        acc = acc * cs_ref[...]

    def finish(v):
        if has_bias:
            v = v + bias_ref[...]
        if has_res:
            v = res_ref[...] + gate_ref[...] * v
        o_ref[...] = v.astype(o_ref.dtype)

    if rope_tiles:
        @pl.when(j < rope_tiles)
        def _():
            finish(_rope_lanes(acc, cos_ref[...], sin_ref[...]))

        @pl.when(j >= rope_tiles)
        def _():
            finish(acc)
    else:
        finish(acc)


def _fused_mm(a, w, *, w_index=None, a_cols=None, gain=None, norm_cols=None, mod=None, colscale=None, rope=None,
              rope_cols=0, bias=None, res=None, gate=None, silu_in=False, rows_per_batch=None,
              out_dtype=F32, tm=1024, tn=1024, precision=None, name="mm"):
    m = a.shape[0]
    k0, k = a_cols if a_cols is not None else (0, a.shape[1])
    if precision is None and w.dtype != BF16:
        w = (w[w_index] if w.ndim == 3 else w).astype(BF16)
    n = w.shape[-1]
    assert w.shape[-2] == k and k0 % k == 0
    tm = min(tm, m)
    tn = min(tn, n)
    assert m % tm == 0 and n % tn == 0
    rows_per_batch = rows_per_batch or m
    assert rows_per_batch % tm == 0
    tpb = rows_per_batch // tm
    has_prologue = gain is not None or mod is not None or silu_in
    in_specs = [pl.BlockSpec((tm, k), lambda i, j: (i, k0 // k))]
    args = [a]
    if w.ndim == 3:
        in_specs.append(pl.BlockSpec((None, k, tn), lambda i, j: (w_index, 0, j)))
    else:
        in_specs.append(pl.BlockSpec((k, tn), lambda i, j: (0, j)))
    args.append(w)
    if gain is not None:
        in_specs.append(pl.BlockSpec((1, k), lambda i, j: (0, 0)))
        args.append(gain.reshape(1, k).astype(F32))
    if mod is not None:
        for t in mod:
            in_specs.append(pl.BlockSpec((None, 1, k), lambda i, j: (i // tpb, 0, 0)))
            args.append(t)
    if colscale is not None:
        in_specs.append(pl.BlockSpec((1, tn), lambda i, j: (0, j)))
        args.append(colscale.reshape(1, n).astype(F32))
    rope_tiles = 0
    if rope is not None:
        assert rope_cols % tn == 0
        rope_tiles = rope_cols // tn
        for t in rope:
            in_specs.append(pl.BlockSpec((tm, LANES), lambda i, j: (i % tpb, 0)))
            args.append(t)
    if bias is not None:
        in_specs.append(pl.BlockSpec((1, tn), lambda i, j: (0, j)))
        args.append(bias.reshape(1, n).astype(F32))
    if res is not None:
        in_specs.append(pl.BlockSpec((tm, tn), lambda i, j: (i, j)))
        args.append(res)
        in_specs.append(pl.BlockSpec((None, 1, tn), lambda i, j: (i // tpb, 0, j)))
        args.append(gate)
    scratch = [pltpu.VMEM((tm, k), F32 if precision is not None else BF16)] if has_prologue else []
    kern = functools.partial(
        _mm_kernel, norm_cols=norm_cols or k, has_gain=gain is not None, has_mod=mod is not None,
        has_colscale=colscale is not None, rope_tiles=rope_tiles, has_bias=bias is not None,
        has_res=res is not None, silu_in=silu_in, precision=precision)
    return pl.pallas_call(
        kern, grid=(m // tm, n // tn), in_specs=in_specs,
        out_specs=pl.BlockSpec((tm, tn), lambda i, j: (i, j)),
        out_shape=jax.ShapeDtypeStruct((m, n), out_dtype), scratch_shapes=scratch,
        compiler_params=_cparams(("parallel", "arbitrary")), name=name)(*args)


def _attn_kernel(*refs, n_src, tk, rb, slice_mode, sub_width, diff_mode, lambda_init):
    q_ref = refs[0]
    k_refs, v_refs = refs[1:1 + 2 * n_src:2], refs[2:2 + 2 * n_src:2]
    pos = 1 + 2 * n_src
    if diff_mode:
        lam_ref, subln_ref = refs[pos], refs[pos + 1]
        pos += 2
    o_ref = refs[pos]
    per = [refs[pos + 1 + 9 * s:pos + 10 + 9 * s] for s in range(2)]
    qm_scr = [t[0] for t in per]
    s_scr = [t[1:3] for t in per]
    p_scr = [t[3:5] for t in per]
    m_scr = [t[5] for t in per]
    a_scr = [t[6:8] for t in per]
    acc_scr = [t[8] for t in per]
    tq = q_ref.shape[0]
    q = q_ref[...]
    lane = lax.broadcasted_iota(jnp.int32, (1, LANES), 1)
    for s in range(2):
        if slice_mode:
            qm_scr[s][...] = q[:, s * LANES:(s + 1) * LANES]
        else:
            qm_scr[s][...] = jnp.where((lane // sub_width) % 2 == s, q, jnp.zeros_like(q))
        m_scr[s][...] = jnp.full(m_scr[s].shape, MASK_VALUE, F32)
        acc_scr[s][...] = jnp.zeros(acc_scr[s].shape, F32)

    lengths = [r.shape[0] for r in k_refs]
    n_chunks = sum(lengths) // tk

    def chunk_of(src_refs, c):
        pieces, first, begin = [], c * tk, 0
        for ref, n in zip(src_refs, lengths):
            lo, hi = max(first, begin), min(first + tk, begin + n)
            if lo < hi:
                pieces.append(ref[lo - begin:hi - begin, :])
            begin += n
        return pieces[0] if len(pieces) == 1 else jnp.concatenate(pieces, axis=0)

    def scores(c, slot):
        k = chunk_of(k_refs, c)
        for s in range(2):
            ks = k[:, s * LANES:(s + 1) * LANES] if slice_mode else k
            s_scr[s][slot][...] = lax.dot_general(qm_scr[s][...], ks, (((1,), (1,)), ((), ())),
                                                  preferred_element_type=F32)

    def softmax(slot):
        for s in range(2):
            for r in range(tq // rb):
                rows = slice(r * rb, (r + 1) * rb)
                sb = s_scr[s][slot][rows, :]
                m_old = m_scr[s][rows, :]
                m_new = jnp.maximum(m_old, jnp.max(sb, axis=-1, keepdims=True))
                m_scr[s][rows, :] = m_new
                a_scr[s][slot][rows, :] = jnp.exp2(m_old - m_new)
                p_scr[s][slot][rows, :] = jnp.exp2(sb - jnp.tile(m_new, (1, tk // LANES))).astype(BF16)

    def weighted_sum(c, slot):
        v = chunk_of(v_refs, c)
        vext = jnp.concatenate([v, jnp.ones((tk, LANES), BF16)], axis=1)
        for s in range(2):
            alpha = a_scr[s][slot][...]
            pv = jnp.dot(p_scr[s][slot][...], vext, preferred_element_type=F32)
            acc_scr[s][...] = jnp.concatenate([alpha, alpha], axis=1) * acc_scr[s][...] + pv

    def step(c, slot):
        scores(c, slot)
        softmax(1 - slot)
        weighted_sum(c - 2, slot)

    scores(0, 0)
    if n_chunks > 1:
        scores(1, 1)
    softmax(0)
    for c in range(2, n_chunks):
        step(c, c % 2)
    if n_chunks > 1:
        softmax((n_chunks - 1) % 2)
        weighted_sum(n_chunks - 2, n_chunks % 2)
    weighted_sum(n_chunks - 1, (n_chunks - 1) % 2)
    o0 = acc_scr[0][:, :LANES] / acc_scr[0][:, LANES:]
    o1 = acc_scr[1][:, :LANES] / acc_scr[1][:, LANES:]
    if diff_mode:
        lp = lam_ref[...]
        lam = (jnp.exp(jnp.sum(lp[0:1] * lp[1:2], axis=-1, keepdims=True))
               - jnp.exp(jnp.sum(lp[2:3] * lp[3:4], axis=-1, keepdims=True)) + lambda_init)
        o = o0 - lam * o1
        ms = jnp.mean(o * o, axis=-1, keepdims=True)
        o = o * lax.rsqrt(ms + NORM_EPS) * subln_ref[...] * (1.0 - lambda_init)
    else:
        o = jnp.where(lane < LANES // 2, o0, o1)
    o_ref[...] = o.astype(o_ref.dtype)


def _attention(q_arr, q_blk, srcs, *, batch, n_q, groups, slice_mode=False, diff=None, sub_width=LANES // 2,
               tq=1024, tk=768, rb=16, name="attn"):
    qw = 2 * LANES if slice_mode else LANES
    n_k = sum(s[4] for s in srcs)
    tq = min(tq, n_q)
    tk = min(tk, n_k)
    assert n_q % tq == 0 and n_k % tk == 0 and tk % LANES == 0 and tq % rb == 0
    nqt = n_q // tq
    in_specs = [pl.BlockSpec((tq, qw), lambda b, g, i: (b * nqt + i, q_blk + g))]
    args = [q_arr]
    for k_arr, k_blk, v_arr, v_blk, nk in srcs:
        in_specs.append(pl.BlockSpec((nk, qw), lambda b, g, i, k_blk=k_blk: (b, k_blk + g)))
        in_specs.append(pl.BlockSpec((nk, LANES), lambda b, g, i, v_blk=v_blk: (b, v_blk + g)))
        args += [k_arr, v_arr]
    lambda_init = 0.0
    if diff is not None:
        lam_params, subln, lambda_init = diff
        in_specs.append(pl.BlockSpec((4, DA_HEAD_DIM), lambda b, g, i: (0, 0)))
        in_specs.append(pl.BlockSpec((1, LANES), lambda b, g, i: (0, 0)))
        args += [lam_params, subln.reshape(1, LANES)]
    kern = functools.partial(_attn_kernel, n_src=len(srcs), tk=tk, rb=rb, slice_mode=slice_mode,
                             sub_width=sub_width, diff_mode=diff is not None, lambda_init=lambda_init)
    scratch = 2 * ([pltpu.VMEM((tq, LANES), BF16)] + 2 * [pltpu.VMEM((tq, tk), F32)] + 2 * [pltpu.VMEM((tq, tk), BF16)]
                   + [pltpu.VMEM((tq, LANES), F32)] + 2 * [pltpu.VMEM((tq, LANES), F32)]
                   + [pltpu.VMEM((tq, 2 * LANES), F32)])
    return pl.pallas_call(
        kern, grid=(batch, groups, nqt), in_specs=in_specs,
        out_specs=pl.BlockSpec((tq, LANES), lambda b, g, i: (b * nqt + i, g)),
        out_shape=jax.ShapeDtypeStruct((batch * n_q, groups * LANES), BF16), scratch_shapes=scratch,
        compiler_params=_cparams(("parallel", "parallel", "arbitrary")), name=name)(*args)


def _na_kernel(q_ref, kc_ref, vc_ref, kl_ref, vl_ref, bias_ref, o_ref, s0_scr, s1_scr, p0_scr, p1_scr, *, n_rows, rb):
    i = pl.program_id(2)
    k0 = jnp.clip(i * NA_Q_ROWS - NA_WIN_ROWS // 2, 0, n_rows - NA_K_ROWS)
    start = pl.multiple_of(k0 * GRID_W, GRID_W * 4)
    nkw = NA_K_ROWS * GRID_W
    kw = kl_ref[pl.ds(start, nkw), :]
    kc = kc_ref[...]
    q = q_ref[...]
    tq = q.shape[0]
    n_all = nkw + kc.shape[0]
    lane = lax.broadcasted_iota(jnp.int32, (1, LANES), 1)
    s_scr, p_scr = (s0_scr, s1_scr), (p0_scr, p1_scr)
    dn = (((1,), (1,)), ((), ()))
    key_row = lax.broadcasted_iota(jnp.int32, (1, nkw), 1) // GRID_W
    for s in range(2):
        qs = jnp.where((lane < LANES // 2) if s == 0 else (lane >= LANES // 2), q, jnp.zeros_like(q))
        s_win = lax.dot_general(qs, kw, dn, preferred_element_type=F32)
        for a in range(NA_Q_ROWS):
            qr = i * NA_Q_ROWS + a
            r0 = jnp.clip(qr - NA_WIN_ROWS // 2, 0, n_rows - NA_WIN_ROWS)
            u = k0 - i * NA_Q_ROWS + (NA_WIN_ROWS - 1 - a + NA_BIAS_PAD)
            off = pl.multiple_of((u // 2) * LANES, LANES)
            parity = (NA_WIN_ROWS - 1 - a + NA_BIAS_PAD) % 2
            bias = bias_ref[s, parity, :, pl.ds(off, nkw)]
            seen = jnp.logical_and(key_row >= r0 - k0, key_row < r0 - k0 + NA_WIN_ROWS)
            rows = slice(a * GRID_W, (a + 1) * GRID_W)
            s_scr[s][rows, :nkw] = jnp.where(seen, s_win[rows, :] + bias, MASK_VALUE)
        s_scr[s][:, nkw:] = lax.dot_general(qs, kc, dn, preferred_element_type=F32)
    for s in range(2):
        for r in range(tq // rb):
            rows = slice(r * rb, (r + 1) * rb)
            sb = s_scr[s][rows, :]
            p_scr[s][rows, :] = jnp.exp2(sb - jnp.max(sb, axis=-1, keepdims=True)).astype(BF16)
    v_all = jnp.concatenate([vl_ref[pl.ds(start, nkw), :], vc_ref[...]], axis=0)
    vext = jnp.concatenate([v_all, jnp.ones((n_all, LANES), BF16)], axis=1)
    outs = []
    for s in range(2):
        pv = jnp.dot(p_scr[s][...], vext, preferred_element_type=F32)
        outs.append(pv[:, :LANES] / pv[:, LANES:])
    o_ref[...] = jnp.where(lane < LANES // 2, outs[0], outs[1]).astype(o_ref.dtype)


NA_BIAS_PAD = 8
NA_BIAS_LANES = 2048


def _na_bias_tables(rpb):
    h = rpb.shape[0]
    n_off = 2 * NA_WIN_ROWS - 1
    dc = np.arange(GRID_W)[None, :] - np.arange(GRID_W)[:, None] + NA_WIN_COLS - 1
    onehot = (dc[None] == np.arange(2 * NA_WIN_COLS - 1)[:, None, None]).astype(np.float32)
    toep = jnp.einsum('hrd,dqk->hqrk', rpb.astype(F32), jnp.asarray(onehot), precision=lax.Precision.HIGHEST)
    cols = np.arange(GRID_W)
    col_start = np.clip(cols - NA_WIN_COLS // 2, 0, GRID_W - NA_WIN_COLS)
    col_ok = (cols[None, :] >= col_start[:, None]) & (cols[None, :] < col_start[:, None] + NA_WIN_COLS)
    toep = jnp.where(jnp.asarray(col_ok)[None, :, None, :], toep, MASK_VALUE).reshape(h, GRID_W, n_off * GRID_W)
    copies = []
    for shift in (0, GRID_W):
        left = NA_BIAS_PAD * GRID_W - shift
        copies.append(jnp.pad(toep, ((0, 0), (0, 0), (left, NA_BIAS_LANES - left - n_off * GRID_W))))
    return jnp.stack(copies, axis=1)


def _na_attention(qkv_lat, qkv_ctx, bias_tab, *, batch, n_lat, n_ctx):
    d = NA_HEADS * NA_HEAD_DIM
    groups = d // LANES
    n_rows = n_lat // GRID_W
    nb = n_rows // NA_Q_ROWS
    tq = NA_Q_ROWS * GRID_W
    assert n_rows >= NA_K_ROWS and n_rows % NA_Q_ROWS == 0
    in_specs = [
        pl.BlockSpec((tq, LANES), lambda b, g, i: (b * nb + i, g)),
        pl.BlockSpec((n_ctx, LANES), lambda b, g, i: (b, groups + g)),
        pl.BlockSpec((n_ctx, LANES), lambda b, g, i: (b, 2 * groups + g)),
        pl.BlockSpec((n_lat, LANES), lambda b, g, i: (b, groups + g)),
        pl.BlockSpec((n_lat, LANES), lambda b, g, i: (b, 2 * groups + g)),
        pl.BlockSpec((2, 2, GRID_W, NA_BIAS_LANES), lambda b, g, i: (g, 0, 0, 0)),
    ]
    n_all = NA_K_ROWS * GRID_W + n_ctx
    return pl.pallas_call(
        functools.partial(_na_kernel, n_rows=n_rows, rb=16), grid=(batch, groups, nb), in_specs=in_specs,
        out_specs=pl.BlockSpec((tq, LANES), lambda b, g, i: (b * nb + i, g)),
        out_shape=jax.ShapeDtypeStruct((batch * n_lat, d), BF16),
        scratch_shapes=2 * [pltpu.VMEM((tq, n_all), F32)] + 2 * [pltpu.VMEM((tq, n_all), BF16)],
        compiler_params=_cparams(("parallel", "parallel", "arbitrary")), name="na_attn",
    )(qkv_lat, qkv_ctx, qkv_ctx, qkv_lat, qkv_lat, bias_tab)


def _dft_kernel(c_ref, ms_ref, ab_ref, mid_ref, o_ref, acc_ref, *, scale):
    n = pl.program_id(2)
    d = o_ref.shape[1]
    part = (jnp.dot(c_ref[...], ab_ref[:, :d], preferred_element_type=F32)
            + jnp.dot(ms_ref[...], ab_ref[:, d:], preferred_element_type=F32))

    @pl.when(n == 0)
    def _():
        acc_ref[...] = part

    @pl.when(n > 0)
    def _():
        acc_ref[...] += part

    @pl.when(n == pl.num_programs(2) - 1)
    def _():
        tk = o_ref.shape[0]
        k = pl.program_id(1) * tk + lax.broadcasted_iota(jnp.int32, (tk, 1), 0)
        sign = jnp.where(k % 2 == 0, 1.0, -1.0)
        o_ref[...] = ((acc_ref[...] + sign * mid_ref[...].astype(F32)) * scale).astype(o_ref.dtype)


def _fold_kernel(x_ref, src_ref, edge_ref, perm_ref, sign_ref, o_ref):
    i = pl.program_id(1)
    mirror = jnp.dot(perm_ref[...], src_ref[...], preferred_element_type=F32)
    first = lax.broadcasted_iota(jnp.int32, (mirror.shape[0], 1), 0) == 0
    edge = jnp.where(i > 0, edge_ref[0:1, :].astype(F32), 0.0)
    mirror = jnp.where(first, edge, mirror)
    o_ref[...] = (x_ref[...].astype(F32) + sign_ref[...] * mirror).astype(o_ref.dtype)


def _dft_fold(ab, *, batch, n_tok, d, name):
    half = n_tok // 2
    t = min(512, half)
    nt = n_tok // t
    perm = np.zeros((t, t), np.float32)
    perm[np.arange(1, t), t - np.arange(1, t)] = 1.0
    sign = jnp.concatenate([jnp.ones((d,), F32), -jnp.ones((d,), F32)]).reshape(1, 2 * d)
    edge_blocks = t // 8
    return pl.pallas_call(
        _fold_kernel, grid=(batch, nt // 2),
        in_specs=[pl.BlockSpec((t, 2 * d), lambda b, i: (b * nt + i, 0)),
                  pl.BlockSpec((t, 2 * d), lambda b, i: (b * nt + nt - 1 - i, 0)),
                  pl.BlockSpec((8, 2 * d), lambda b, i: ((b * nt + jnp.maximum(nt - i, 1) % nt) * edge_blocks, 0)),
                  pl.BlockSpec((t, t), lambda b, i: (0, 0)),
                  pl.BlockSpec((1, 2 * d), lambda b, i: (0, 0))],
        out_specs=pl.BlockSpec((t, 2 * d), lambda b, i: (b * (nt // 2) + i, 0)),
        out_shape=jax.ShapeDtypeStruct((batch * half, 2 * d), BF16),
        compiler_params=_cparams(("parallel", "parallel")), name=name)(ab, ab, ab, jnp.asarray(perm, BF16), sign)


def _pos_dft(cos_t, msin_t, ab, *, batch, n_tok, d, scale, name):
    half = n_tok // 2
    folded = _dft_fold(ab, batch=batch, n_tok=n_tok, d=d, name=name + "_fold")
    mid = ab.reshape(batch, n_tok, 2 * d)[:, half:half + 1, :d]
    tk = min(1024, n_tok)
    tn = min(1024, half)
    nk, nn = n_tok // tk, half // tn
    return pl.pallas_call(
        functools.partial(_dft_kernel, scale=scale), grid=(batch, nk, nn),
        in_specs=[pl.BlockSpec((tk, tn), lambda b, k, n: (k, n)),
                  pl.BlockSpec((tk, tn), lambda b, k, n: (k, n)),
                  pl.BlockSpec((tn, 2 * d), lambda b, k, n: (b * nn + n, 0)),
                  pl.BlockSpec((None, 1, d), lambda b, k, n: (b, 0, 0))],
        out_specs=pl.BlockSpec((tk, d), lambda b, k, n: (b * nk + k, 0)),
        out_shape=jax.ShapeDtypeStruct((batch * n_tok, d), BF16),
        scratch_shapes=[pltpu.VMEM((tk, d), F32)],
        compiler_params=_cparams(("parallel", "parallel", "arbitrary")), name=name)(cos_t, msin_t, folded, mid)


def _dft_tables(n):
    split = 64 if n % 64 == 0 and n > 64 else 1
    m = jnp.arange(n // 2, dtype=jnp.int32)[None, :]

    def phase(kvals):
        ang = ((kvals[:, None] * m) % n).astype(F32) * (2.0 * math.pi / n)
        return jnp.cos(ang), jnp.sin(ang)

    c1, s1 = phase(jnp.arange(n // split, dtype=jnp.int32) * split)
    c2, s2 = phase(jnp.arange(split, dtype=jnp.int32))
    c = c1[:, None, :] * c2[None, :, :] - s1[:, None, :] * s2[None, :, :]
    s = s1[:, None, :] * c2[None, :, :] + c1[:, None, :] * s2[None, :, :]
    return c.reshape(n, n // 2).astype(BF16), (-s).reshape(n, n // 2).astype(BF16)


def _channel_dft_weight(d):
    cg = d // FN_GROUPS
    idx = np.arange(cg)
    ang = 2.0 * np.pi * ((idx[:, None] * idx[None, :]) % cg) / cg
    return jnp.asarray(np.concatenate([np.cos(ang), np.sin(ang)], axis=1), BF16)


def _chan_dft_kernel(x_ref, gain_ref, sc_ref, sh_ref, w_ref, o_ref):
    x = x_ref[...]
    d = x.shape[1]
    cg = d // FN_GROUPS
    h = x * lax.rsqrt(jnp.mean(x * x, axis=-1, keepdims=True) + NORM_EPS) * gain_ref[...]
    h = (h * (1.0 + sc_ref[...]) + sh_ref[...]).astype(BF16)
    w = w_ref[...]
    for g in range(FN_GROUPS):
        ab = jnp.dot(h[:, g * cg:(g + 1) * cg], w, preferred_element_type=F32).astype(o_ref.dtype)
        o_ref[:, g * cg:(g + 1) * cg] = ab[:, :cg]
        o_ref[:, d + g * cg:d + (g + 1) * cg] = ab[:, cg:]


def _chan_dft(x, gain, mod, *, rows_per_batch, tm=1024, name):
    m, d = x.shape
    tm = min(tm, rows_per_batch)
    tpb = rows_per_batch // tm
    cg = d // FN_GROUPS
    per_sample = pl.BlockSpec((None, 1, d), lambda i: (i // tpb, 0, 0))
    return pl.pallas_call(
        _chan_dft_kernel, grid=(m // tm,),
        in_specs=[pl.BlockSpec((tm, d), lambda i: (i, 0)), pl.BlockSpec((1, d), lambda i: (0, 0)), per_sample,
                  per_sample, pl.BlockSpec((cg, 2 * cg), lambda i: (0, 0))],
        out_specs=pl.BlockSpec((tm, 2 * d), lambda i: (i, 0)),
        out_shape=jax.ShapeDtypeStruct((m, 2 * d), BF16),
        compiler_params=_cparams(("parallel",)), name=name,
    )(x, gain.reshape(1, d).astype(F32), mod[0], mod[1], _channel_dft_weight(d))


def _router_kernel(x_ref, gain_ref, sc_ref, sh_ref, wr_ref, h_ref, aff_ref):
    x = x_ref[...]
    ms = jnp.mean(x * x, axis=-1, keepdims=True)
    h = x * lax.rsqrt(ms + NORM_EPS) * gain_ref[...]
    h = h * (1.0 + sc_ref[...]) + sh_ref[...]
    h_ref[...] = h.astype(h_ref.dtype)
    w = wr_ref[...]
    h_hi, w_hi = h.astype(BF16), w.astype(BF16)
    h_lo, w_lo = (h - h_hi.astype(F32)).astype(BF16), (w - w_hi.astype(F32)).astype(BF16)
    logits = (jnp.dot(h_hi, w_hi, preferred_element_type=F32) + jnp.dot(h_lo, w_hi, preferred_element_type=F32)
              + jnp.dot(h_hi, w_lo, preferred_element_type=F32))
    e = jnp.exp(logits - jnp.max(logits, axis=-1, keepdims=True))
    aff_ref[...] = e / jnp.sum(e, axis=-1, keepdims=True)


def _norm_router(x, gain, sc, sh, w_router, layer, *, rows_per_batch, tm=512):
    m, d = x.shape
    e = w_router.shape[-1]
    tm = min(tm, rows_per_batch)
    tpb = rows_per_batch // tm
    return pl.pallas_call(
        _router_kernel, grid=(m // tm,),
        in_specs=[pl.BlockSpec((tm, d), lambda i: (i, 0)),
                  pl.BlockSpec((1, d), lambda i: (0, 0)),
                  pl.BlockSpec((None, 1, d), lambda i: (i // tpb, 0, 0)),
                  pl.BlockSpec((None, 1, d), lambda i: (i // tpb, 0, 0)),
                  pl.BlockSpec((None, d, e), lambda i: (layer, 0, 0))],
        out_specs=[pl.BlockSpec((tm, d), lambda i: (i, 0)), pl.BlockSpec((tm, e), lambda i: (i, 0))],
        out_shape=[jax.ShapeDtypeStruct((m, d), BF16), jax.ShapeDtypeStruct((m, e), F32)],
        compiler_params=_cparams(("parallel",)), name="norm_router")(x, gain.reshape(1, d), sc, sh, w_router)


FFN_PIECE = 256


def _ffn_kernel(*refs, n_streams):
    x_refs = refs[:n_streams]
    wg_ref, wu_ref, wd_ref = refs[n_streams:n_streams + 3]
    gate_refs = refs[n_streams + 3:2 * n_streams + 3]
    o_refs = refs[2 * n_streams + 3:3 * n_streams + 3]
    acc_ref = refs[3 * n_streams + 3]
    f = pl.program_id(2)
    x = x_refs[0][...] if n_streams == 1 else jnp.concatenate([r[...] for r in x_refs], axis=0)

    @pl.when(f == 0)
    def _():
        acc_ref[...] = jnp.zeros(acc_ref.shape, F32)

    tf = wg_ref.shape[1]
    acts = []
    for c0 in range(0, tf, FFN_PIECE):
        a = jnp.dot(x, wg_ref[:, c0:c0 + FFN_PIECE].astype(BF16), preferred_element_type=F32)
        u = jnp.dot(x, wu_ref[:, c0:c0 + FFN_PIECE].astype(BF16), preferred_element_type=F32)
        acts.append(((a * (1.0 / (1.0 + jnp.exp(-a)))) * u).astype(BF16))
    d = wd_ref.shape[1]
    for n0 in range(0, d, FFN_PIECE):
        y = None
        for i, act in enumerate(acts):
            part = jnp.dot(act, wd_ref[i * FFN_PIECE:(i + 1) * FFN_PIECE, n0:n0 + FFN_PIECE].astype(BF16),
                           preferred_element_type=F32)
            y = part if y is None else y + part
        acc_ref[:, n0:n0 + FFN_PIECE] += y

    @pl.when(f == pl.num_programs(2) - 1)
    def _():
        row = 0
        for o_ref, g_ref in zip(o_refs, gate_refs):
            c = o_ref.shape[0]
            y = acc_ref[row:row + c, :] * g_ref[...]
            if o_ref.dtype == BF16:
                hi = y.astype(BF16)
                o_ref[:, :d] = hi
                o_ref[:, d:] = (y - hi.astype(F32)).astype(BF16)
            else:
                o_ref[...] = y
            row += c


def _moe_ffn(xs, gates, w_gate, w_up, w_down, layer, *, tf=1024):
    b, e, _, d = xs[0].shape
    f = w_gate.shape[-1]
    tf = min(tf, f)
    slot = lambda c, w: pl.BlockSpec((None, None, c, w), lambda ei, bi, fi: (bi, ei, 0, 0))
    caps = [t.shape[2] for t in xs]
    in_specs = ([slot(c, d) for c in caps]
                + [pl.BlockSpec((None, None, d, tf), lambda ei, bi, fi: (layer, ei, 0, fi)),
                   pl.BlockSpec((None, None, d, tf), lambda ei, bi, fi: (layer, ei, 0, fi)),
                   pl.BlockSpec((None, None, tf, d), lambda ei, bi, fi: (layer, ei, fi, 0))]
                + [slot(c, 1) for c in caps])
    return pl.pallas_call(
        functools.partial(_ffn_kernel, n_streams=len(xs)), grid=(e, b, f // tf), in_specs=in_specs,
        out_specs=[slot(caps[0], 2 * d)] + [slot(c, d) for c in caps[1:]],
        out_shape=([jax.ShapeDtypeStruct((b, e, caps[0], 2 * d), BF16)]
                   + [jax.ShapeDtypeStruct((b, e, c, d), F32) for c in caps[1:]]),
        scratch_shapes=[pltpu.VMEM((sum(caps), d), F32)],
        compiler_params=_cparams(("parallel", "parallel", "arbitrary")), name="moe_ffn",
    )(*xs, w_gate, w_up, w_down, *gates)


def _count_upto(flags):
    lead, n = flags.shape[:-1], flags.shape[-1]
    blocks = flags.reshape(*lead, n // LANES, LANES).astype(BF16)
    tri = jnp.asarray(np.triu(np.ones((LANES, LANES), np.float32))).astype(BF16)
    inside = jnp.einsum('...i,ij->...j', blocks, tri, preferred_element_type=F32)
    totals = inside[..., -1]
    before = jnp.cumsum(totals, axis=-1) - totals
    return (inside + before[..., None]).reshape(*lead, n).astype(jnp.int32)


def _route(aff, h, batch, n_tok):
    e = aff.shape[-1]
    cap = EC_CAPACITY_FACTOR * n_tok // e
    aff_t = jnp.swapaxes(aff.reshape(batch, n_tok, e), 1, 2)
    gate, idx = lax.top_k(aff_t, cap)
    thr = gate[..., -1:]
    above, tied = aff_t > thr, aff_t == thr
    n_tied = cap - jnp.sum(above, axis=-1, keepdims=True, dtype=jnp.int32)
    chosen = above | (tied & (_count_upto(tied) <= n_tied))
    pos = jnp.where(chosen, _count_upto(chosen) - 1, -1)
    pos = jnp.swapaxes(pos, 1, 2).reshape(batch * n_tok, e)
    idx, gate = lax.sort((idx, gate), dimension=2, num_keys=1)
    xs = jax.vmap(lambda hb, ib: hb[ib])(h.reshape(batch, n_tok, -1), idx)
    return gate, idx, xs, pos


COMBINE_TOKENS = 256
COMBINE_WINDOW = 64
SLOT_ALIGN = 16


def _combine_kernel(lo_ref, nr_ref, x_ref, g_ref, pos_ref, fn_ref, y_hbm, o_ref, stage, sem, acc, *, n_tiles, n_total,
                    n_exp, cap, final_norm):
    w = COMBINE_WINDOW
    per = LANES // w
    tile = pl.program_id(0) * n_tiles + pl.program_id(1)
    slot = tile % 2
    lane = lax.broadcasted_iota(jnp.int32, (1, LANES), 1)

    def window(tl, e, r):
        first = lo_ref[tl * n_exp + e] + r * w
        return first, pl.multiple_of(jnp.minimum(first, cap - w), SLOT_ALIGN)

    def copy(tl, e, start, sl):
        return pltpu.make_async_copy(y_hbm.at[tl // n_tiles, e, pl.ds(start, w), :],
                                     stage.at[sl, pl.ds(e * w, w), :], sem.at[sl, e])

    def fetch(tl, r, sl):
        for e in range(n_exp):
            copy(tl, e, window(tl, e, r)[1], sl).start()

    def add_round(r):
        pos = pos_ref[...]
        blocks = []
        for e0 in range(0, n_exp, per):
            first, start = window(tile, e0, r)
            col = pos[:, e0:e0 + 1]
            for k in range(1, per):
                f_k, s_k = window(tile, e0 + k, r)
                here = lane >= k * w
                first, start = jnp.where(here, f_k, first), jnp.where(here, s_k, start)
                col = jnp.where(here, pos[:, e0 + k:e0 + k + 1], col)
            hit = jnp.logical_and(col - start == lane % w, col >= first)
            blocks.append(jnp.where(hit, 1.0, 0.0).astype(BF16))
        onehot = jnp.concatenate(blocks, axis=1)
        for e in range(n_exp):
            copy(tile, e, 0, slot).wait()
        d = acc.shape[1]
        acc[...] += (jnp.dot(onehot, stage[slot, :, :d], preferred_element_type=F32)
                     + jnp.dot(onehot, stage[slot, :, d:], preferred_element_type=F32))

    @pl.when(tile == 0)
    def _():
        fetch(tile, 0, slot)

    @pl.when(tile + 1 < n_total)
    def _():
        fetch(tile + 1, 0, 1 - slot)

    acc[...] = jnp.zeros(acc.shape, F32)
    add_round(0)

    @pl.loop(1, nr_ref[tile])
    def _(r):
        fetch(tile, r, slot)
        add_round(r)

    out = x_ref[...] + g_ref[...] * acc[...]
    if final_norm:
        out = out * lax.rsqrt(jnp.mean(out * out, axis=-1, keepdims=True) + NORM_EPS) * fn_ref[...]
    o_ref[...] = out


def _combine_lat(x, g2, y, idx, pos, batch, n_tok, final_gain=None):
    d = x.shape[-1]
    _, n_exp, cap = idx.shape
    tt, w = COMBINE_TOKENS, COMBINE_WINDOW
    assert n_tok % tt == 0 and cap % SLOT_ALIGN == 0 and w % SLOT_ALIGN == 0 and cap >= w
    nt = n_tok // tt
    bounds = jnp.arange(nt + 1, dtype=jnp.int32) * tt
    below = jnp.sum((idx[..., None] < bounds).astype(jnp.int32), axis=2)
    lo = (below[..., :-1] // SLOT_ALIGN) * SLOT_ALIGN
    rounds = jnp.maximum(jnp.max((below[..., 1:] - lo + w - 1) // w, axis=1), 1)
    lo_flat = jnp.transpose(lo, (0, 2, 1)).reshape(-1)
    grid_spec = pltpu.PrefetchScalarGridSpec(
        num_scalar_prefetch=2, grid=(batch, nt),
        in_specs=[pl.BlockSpec((tt, d), lambda b, t, lo_r, nr_r: (b * nt + t, 0)),
                  pl.BlockSpec((None, 1, d), lambda b, t, lo_r, nr_r: (b, 0, 0)),
                  pl.BlockSpec((tt, n_exp), lambda b, t, lo_r, nr_r: (b * nt + t, 0)),
                  pl.BlockSpec((1, d), lambda b, t, lo_r, nr_r: (0, 0)),
                  pl.BlockSpec(memory_space=pl.ANY)],
        out_specs=pl.BlockSpec((tt, d), lambda b, t, lo_r, nr_r: (b * nt + t, 0)),
        scratch_shapes=[pltpu.VMEM((2, n_exp * w, 2 * d), BF16), pltpu.SemaphoreType.DMA((2, n_exp)),
                        pltpu.VMEM((tt, d), F32)])
    return pl.pallas_call(
        functools.partial(_combine_kernel, n_tiles=nt, n_total=batch * nt, n_exp=n_exp, cap=cap,
                          final_norm=final_gain is not None),
        grid_spec=grid_spec, out_shape=jax.ShapeDtypeStruct(x.shape, F32),
        compiler_params=_cparams(("arbitrary", "arbitrary")), name="moe_combine",
    )(lo_flat, rounds.reshape(-1), x, g2, pos, (g2[0] if final_gain is None else final_gain.reshape(1, d)), y)


def _combine(x, g2, y, idx, batch, n_tok):
    d = x.shape[-1]
    upd = jax.vmap(lambda ib, yb: jnp.zeros((n_tok, d), F32).at[ib.reshape(-1)].add(yb.reshape(-1, d)))(idx, y)
    return (x.reshape(batch, n_tok, d) + g2 * upd).reshape(batch * n_tok, d)


def _rope_tables(n_tok, rot_dim, pair_of_lane):
    t = jnp.arange(n_tok)
    rows = (t // GRID_W).astype(F32)
    cols = (t % GRID_W).astype(F32)
    n_freq = rot_dim // 4
    inv_freq = ROPE_BASE ** (-jnp.arange(n_freq, dtype=F32) / n_freq)
    ang = jnp.concatenate([rows[:, None] * inv_freq, cols[:, None] * inv_freq], axis=-1)
    cos, sin = jnp.cos(ang), jnp.sin(ang)
    pair = np.concatenate([pair_of_lane, pair_of_lane])
    used = jnp.asarray(pair >= 0)[None]
    sign = jnp.asarray(np.where(np.arange(LANES) < LANES // 2, -1.0, 1.0).astype(np.float32))[None]
    cos_l = jnp.where(used, cos[:, np.maximum(pair, 0)], 1.0)
    sin_l = jnp.where(used, sin[:, np.maximum(pair, 0)] * sign, 0.0)
    return cos_l, sin_l


def kernel(x, c, ctx, c_ctx, ada_w, ada_b, norm_mix, norm_ffn, norm_final, da_w_qkv, da_w_o, da_lambda_q1,
           da_lambda_k1, da_lambda_q2, da_lambda_k2, da_subln, fn_w_o, na_w_qkv, na_w_o, na_rpb, mla_w_dq,
           mla_q_norm, mla_w_uq, mla_w_dkv, mla_kv_norm, mla_w_uk, mla_w_uv, mla_w_o, moe_w_router, moe_w_gate,
           moe_w_up, moe_w_down):
    batch, n_lat, d = x.shape
    n_ctx = ctx.shape[1]
    depth = ada_w.shape[0]
    n_mixers = 4
    xl = x.reshape(batch * n_lat, d)
    xc = ctx.reshape(batch * n_ctx, d)

    cond = jnp.concatenate([c, c_ctx[None], jnp.zeros((8 - batch - 1, d), F32)], axis=0)
    mods = [
        _fused_mm(cond, ada_w, w_index=i, bias=ada_b[i], silu_in=True, precision=lax.Precision.HIGHEST, tn=1024,
                  name="ada_mod")
        for i in range(depth)
    ]

    def chunks(i):
        lat = [mods[i][:batch, k * d:(k + 1) * d].reshape(batch, 1, d) for k in range(ADA_CHUNKS)]
        cx = [mods[i][batch:batch + 1, k * d:(k + 1) * d].reshape(1, 1, d) for k in range(ADA_CHUNKS)]
        return lat, cx

    for i in range(depth):
        kind, j = i % n_mixers, i // n_mixers
        keep_ctx = i < depth - 1
        (sh1, sc1, g1, sh2, sc2, g2), (csh1, csc1, cg1, csh2, csc2, cg2) = chunks(i)
        lat_in = dict(gain=norm_mix[i], mod=(sc1, sh1), rows_per_batch=n_lat)
        ctx_in = dict(gain=norm_mix[i], mod=(csc1, csh1), rows_per_batch=batch * n_ctx, tm=n_ctx)

        if kind == 0:
            lambda_init = 0.8 - 0.6 * math.exp(-0.3 * i)
            scale = jnp.concatenate([jnp.full((d,), LOG2E * DA_HEAD_DIM ** -0.5, F32), jnp.ones((2 * d,), F32)])
            groups = d // LANES
            half = DA_HEAD_DIM // 2
            w = da_w_qkv[j]

            def reorder(cols):
                return cols.reshape(d, groups, 2, 2, half).transpose(0, 1, 3, 2, 4).reshape(d, d)

            w = jnp.concatenate([reorder(w[:, :d]), reorder(w[:, d:2 * d]), w[:, 2 * d:]], axis=1)
            rope = _rope_tables(n_lat, DA_HEAD_DIM, np.arange(LANES // 2) % half)
            qkv_l = _fused_mm(xl, w, colscale=scale, rope=rope, rope_cols=2 * d, out_dtype=BF16, name="da_qkv_lat",
                              **lat_in)
            qkv_c = _fused_mm(xc, w, colscale=scale, out_dtype=BF16, name="da_qkv_ctx", **ctx_in)
            lam_params = jnp.stack([da_lambda_q1[j], da_lambda_k1[j], da_lambda_q2[j], da_lambda_k2[j]])
            diff = (lam_params, da_subln[j], lambda_init)
            ctx_kv = (qkv_c, groups, qkv_c, 2 * groups, n_ctx)
            lat_kv = (qkv_l, groups, qkv_l, 2 * groups, n_lat)
            o_l = _attention(qkv_l, 0, [ctx_kv, lat_kv], batch=batch, n_q=n_lat, groups=groups, sub_width=half,
                             diff=diff, name="da_attn_lat")
            o_c = _attention(qkv_c, 0, [ctx_kv], batch=batch, n_q=n_ctx, groups=groups, sub_width=half, diff=diff,
                             name="da_attn_ctx") if keep_ctx else None
            w_o = da_w_o
        elif kind == 1:
            ab_l = _chan_dft(xl, norm_mix[i], (sc1, sh1), rows_per_batch=n_lat, name="fn_chan_lat")
            scale_l = 1.0 / math.sqrt(n_lat * (d // FN_GROUPS))
            o_l = _pos_dft(*_dft_tables(n_lat), ab_l, batch=batch, n_tok=n_lat, d=d, scale=scale_l, name="fn_pos_lat")
            o_c = None
            if keep_ctx:
                ab_c = _chan_dft(xc, norm_mix[i], (csc1, csh1), rows_per_batch=batch * n_ctx, tm=n_ctx,
                                 name="fn_chan_ctx")
                scale_c = 1.0 / math.sqrt(n_ctx * (d // FN_GROUPS))
                o_c = _pos_dft(*_dft_tables(n_ctx), ab_c, batch=batch, n_tok=n_ctx, d=d, scale=scale_c,
                               name="fn_pos_ctx")
            w_o = fn_w_o
        elif kind == 2:
            scale = jnp.concatenate([jnp.full((d,), LOG2E * NA_HEAD_DIM ** -0.5, F32), jnp.ones((2 * d,), F32)])
            qkv_l = _fused_mm(xl, na_w_qkv, w_index=j, colscale=scale, out_dtype=BF16, name="na_qkv_lat", **lat_in)
            qkv_c = _fused_mm(xc, na_w_qkv, w_index=j, colscale=scale, out_dtype=BF16, name="na_qkv_ctx", **ctx_in)
            groups = d // LANES
            bias_tab = _na_bias_tables(na_rpb[j] * LOG2E)
            o_l = _na_attention(qkv_l, qkv_c, bias_tab, batch=batch, n_lat=n_lat, n_ctx=n_ctx)
            o_c = _attention(qkv_c, 0, [(qkv_c, groups, qkv_c, 2 * groups, n_ctx)], batch=batch, n_q=n_ctx,
                             groups=groups, name="na_attn_ctx") if keep_ctx else None
            w_o = na_w_o
        else:
            q_rank = mla_w_dq.shape[-1]
            kv_rank = mla_w_uk.shape[-2]
            hq = MLA_NOPE_DIM + MLA_ROPE_DIM
            down_w = 3 * 384
            assert q_rank == 768 and kv_rank == 256
            w_down = jnp.concatenate([mla_w_dq[j], mla_w_dkv[j], jnp.zeros((d, down_w - q_rank - kv_rank - MLA_ROPE_DIM), F32)],
                                     axis=1)
            rh = MLA_ROPE_DIM // 2
            n_a = LANES // 2 - rh

            def head_lanes(nope, rope_part):
                z = jnp.zeros(nope.shape[:-1] + (LANES - hq,), F32)
                return jnp.concatenate([rope_part[..., :rh], nope[..., :n_a], rope_part[..., rh:], nope[..., n_a:], z],
                                       axis=-1)

            wq = mla_w_uq[j].reshape(q_rank, MLA_HEADS, hq)
            wq = head_lanes(wq[..., :MLA_NOPE_DIM], wq[..., MLA_NOPE_DIM:]).reshape(q_rank, MLA_HEADS * LANES)
            wk_nope = head_lanes(mla_w_uk[j].reshape(kv_rank, MLA_HEADS, MLA_NOPE_DIM),
                                 jnp.zeros((kv_rank, MLA_HEADS, MLA_ROPE_DIM), F32))
            eye = jnp.broadcast_to(jnp.eye(MLA_ROPE_DIM, dtype=F32)[:, None, :], (MLA_ROPE_DIM, MLA_HEADS, MLA_ROPE_DIM))
            wk_rope = head_lanes(jnp.zeros((MLA_ROPE_DIM, MLA_HEADS, MLA_NOPE_DIM), F32), eye)
            kw = MLA_HEADS * LANES
            vw = MLA_HEADS * MLA_V_DIM
            pad_rows = 384 - kv_rank - MLA_ROPE_DIM
            w_kv = jnp.concatenate([
                jnp.concatenate([wk_nope.reshape(kv_rank, kw), mla_w_uv[j]], axis=1),
                jnp.concatenate([wk_rope.reshape(MLA_ROPE_DIM, kw), jnp.zeros((MLA_ROPE_DIM, vw), F32)], axis=1),
                jnp.zeros((pad_rows, kw + vw), F32)], axis=0)
            kv_gain = jnp.concatenate([mla_kv_norm[j], jnp.ones((384 - kv_rank,), F32)])
            rope = _rope_tables(n_lat, MLA_ROPE_DIM, np.where(np.arange(LANES // 2) < rh, np.arange(LANES // 2), -1))
            qscale = jnp.full((kw,), LOG2E * hq ** -0.5, F32)

            def project(xs, inp, n_tok, rope_tabs, tag):
                low = _fused_mm(xs, w_down, tn=384, name="mla_down_" + tag, **inp)
                rp = dict(rope=rope_tabs) if rope_tabs is not None else {}
                tmx = inp.get("tm", 1024)
                q = _fused_mm(low, wq, a_cols=(0, q_rank), gain=mla_q_norm[j], colscale=qscale, rope_cols=kw,
                              rows_per_batch=inp["rows_per_batch"], tm=tmx, out_dtype=BF16, name="mla_q_" + tag, **rp)
                kv = _fused_mm(low, w_kv, a_cols=(q_rank, 384), gain=kv_gain, norm_cols=kv_rank, rope_cols=kw,
                               rows_per_batch=inp["rows_per_batch"], tm=tmx, out_dtype=BF16, name="mla_kv_" + tag, **rp)
                return q, kv

            q_l, kv_l = project(xl, lat_in, n_lat, rope, "lat")
            q_c, kv_c = project(xc, ctx_in, n_ctx, None, "ctx")
            groups = MLA_HEADS // 2
            ctx_kv = (kv_c, 0, kv_c, 2 * groups, n_ctx)
            lat_kv = (kv_l, 0, kv_l, 2 * groups, n_lat)
            o_l = _attention(q_l, 0, [ctx_kv, lat_kv], batch=batch, n_q=n_lat, groups=groups, slice_mode=True,
                             name="mla_attn_lat")
            o_c = _attention(q_c, 0, [ctx_kv], batch=batch, n_q=n_ctx, groups=groups, slice_mode=True,
                             name="mla_attn_ctx") if keep_ctx else None
            w_o = mla_w_o

        xl = _fused_mm(o_l, w_o, w_index=j, res=xl, gate=g1, rows_per_batch=n_lat, name="mix_out_lat")
        if keep_ctx:
            xc = _fused_mm(o_c, w_o, w_index=j, res=xc, gate=jnp.broadcast_to(cg1, (1, 1, d)),
                           rows_per_batch=batch * n_ctx, tm=n_ctx, name="mix_out_ctx")

        h_l, aff_l = _norm_router(xl, norm_ffn[i], sc2, sh2, moe_w_router, i, rows_per_batch=n_lat)
        gate_l, idx_l, xs_l, pos_l = _route(aff_l, h_l, batch, n_lat)
        xs, gates = [xs_l], [gate_l[..., None]]
        if keep_ctx:
            h_c, aff_c = _norm_router(xc, norm_ffn[i], csc2, csh2, moe_w_router, i, rows_per_batch=batch * n_ctx,
                                      tm=n_ctx)
            gate_c, idx_c, xs_c, _ = _route(aff_c, h_c, batch, n_ctx)
            xs.append(xs_c)
            gates.append(gate_c[..., None])
        ys = _moe_ffn(xs, gates, moe_w_gate, moe_w_up, moe_w_down, i)
        xl = _combine_lat(xl, g2, ys[0], idx_l, pos_l, batch, n_lat,
                          final_gain=norm_final if i == depth - 1 else None)
        if keep_ctx:
            xc = _combine(xc, cg2, ys[1], idx_c, batch, n_ctx)

    return xl.reshape(batch, n_lat, d)
```
